```python
import math
import jax, jax.numpy as jnp
from jax import lax
import numpy as np

D_MODEL = 1024
BATCH = 1
SEQ = 16384
DEPTH = 1
DEC_BATCH = 128
DEC_SEQ = 4
PAST_LEN = 16384
PAGE_SIZE = 128

ATT_WIDTH = D_MODEL // 2
MLSTM_WIDTH = D_MODEL - ATT_WIDTH
HEAD_DIM = 64
N_HEADS_ATT = ATT_WIDTH // HEAD_DIM
N_KV_HEADS = 2
GQA_GROUP = N_HEADS_ATT // N_KV_HEADS
WINDOW = 128
ATT_BLOCK = WINDOW
NUM_BUCKETS = 32
MAX_DISTANCE = 128
N_HEADS_M = 4
M_HEAD_DIM = MLSTM_WIDTH // N_HEADS_M
MLSTM_CHUNK = 128
PEER_HEADS = 8
N_KEYS = 128
N_EXPERTS = N_KEYS * N_KEYS
PEER_TOPK = 16
PEER_QDIM = 256
PEER_HALF = PEER_QDIM // 2
PEER_BLOCK = 128
DN_ALPHA = (2 * DEPTH) ** 0.25
DN_BETA = (8 * DEPTH) ** -0.25
LN_EPS = 1e-5
NEG_INF = -1e30
Q_A = N_HEADS_ATT * HEAD_DIM
KV_A = N_KV_HEADS * HEAD_DIM
PROJ_WIDTH = Q_A + 2 * KV_A + 4 * MLSTM_WIDTH + 2 * N_HEADS_M

kernel_name = 'hymba_swa_mlstm_peer_step'


def layer_norm(x, g, b):
    xf = x.astype(jnp.float32)
    mu = xf.mean(-1, keepdims=True)
    var = jnp.mean(jnp.square(xf - mu), -1, keepdims=True)
    y = (xf - mu) * lax.rsqrt(var + LN_EPS) * g.astype(jnp.float32) + b.astype(jnp.float32)
    return y.astype(x.dtype)


def t5_bucket(dist):
    n = jnp.maximum(dist, 0)
    exact = NUM_BUCKETS // 2
    nf = jnp.maximum(n, 1).astype(jnp.float32)
    large = exact + (jnp.log(nf / exact) / math.log(MAX_DISTANCE / exact)
                     * (NUM_BUCKETS - exact)).astype(jnp.int32)
    large = jnp.minimum(large, NUM_BUCKETS - 1)
    return jnp.where(n < exact, n, large)


def split_proj(x, w_in, gate_bias):
    z = x @ w_in
    sizes = [Q_A, KV_A, KV_A, MLSTM_WIDTH, MLSTM_WIDTH, MLSTM_WIDTH, MLSTM_WIDTH, N_HEADS_M, N_HEADS_M]
    points = [int(p) for p in np.cumsum(sizes)[:-1]]
    qa, ka, va, qm, km, vm, om, ig, fg = jnp.split(z, points, axis=-1)
    lead = x.shape[:-1]
    qa = qa.reshape(*lead, N_KV_HEADS, GQA_GROUP, HEAD_DIM)
    ka = ka.reshape(*lead, N_KV_HEADS, HEAD_DIM)
    va = va.reshape(*lead, N_KV_HEADS, HEAD_DIM)
    qm = qm.reshape(*lead, N_HEADS_M, M_HEAD_DIM)
    km = km.reshape(*lead, N_HEADS_M, M_HEAD_DIM) * (M_HEAD_DIM ** -0.5)
    vm = vm.reshape(*lead, N_HEADS_M, M_HEAD_DIM)
    om = om.reshape(*lead, N_HEADS_M, M_HEAD_DIM)
    ig = (ig + gate_bias[:N_HEADS_M]).astype(jnp.float32)
    lf = jax.nn.log_sigmoid((fg + gate_bias[N_HEADS_M:]).astype(jnp.float32))
    return qa, ka, va, qm, km, vm, om, ig, lf


def swa_attend(q, k, v, qpos, kpos, rel_bias, sinks):
    dist = qpos[:, :, None] - kpos[:, None, :]
    valid = (dist >= 0) & (dist < WINDOW) & (kpos[:, None, :] >= 0)
    bias = rel_bias.astype(jnp.float32)[t5_bucket(dist)]
    bias = bias.reshape(*dist.shape, N_KV_HEADS, GQA_GROUP).transpose(0, 3, 4, 1, 2)
    logits = jnp.einsum('bnqhgd,bnkhd->bnhgqk', q, k).astype(jnp.float32) * (HEAD_DIM ** -0.5) + bias
    logits = jnp.where(valid[:, None, None], logits, NEG_INF)
    sink = sinks.astype(jnp.float32).reshape(N_KV_HEADS, GQA_GROUP)[:, :, None, None]
    mx = jnp.maximum(logits.max(-1, keepdims=True), sink)
    e = jnp.exp(logits - mx)
    p = e / (e.sum(-1, keepdims=True) + jnp.exp(sink - mx))
    return jnp.einsum('bnhgqk,bnkhd->bnqhgd', p.astype(v.dtype), v)


def mlstm_chunk(carry, inp):
    C, n, m = carry
    q, k, v, ig, lf = inp
    q = q.astype(jnp.float32)
    k = k.astype(jnp.float32)
    v = v.astype(jnp.float32)
    L = q.shape[1]
    b = jnp.cumsum(lf, axis=1).transpose(0, 2, 1)
    it = ig.transpose(0, 2, 1)
    causal = jnp.tril(jnp.ones((L, L), dtype=bool))
    logd = jnp.where(causal, b[..., :, None] - b[..., None, :] + it[..., None, :], -jnp.inf)
    inter = b + m[..., None]
    mt = jnp.maximum(inter, logd.max(-1))
    w_inter = jnp.exp(inter - mt)
    a = jnp.exp(logd - mt[..., None]) * jnp.einsum('bthd,bshd->bhts', q, k)
    num = jnp.einsum('bhts,bshv->bhtv', a, v) + w_inter[..., None] * jnp.einsum('bhvd,bthd->bhtv', C, q)
    den = a.sum(-1) + w_inter * jnp.einsum('bhd,bthd->bht', n, q)
    h = num / jnp.maximum(jnp.abs(den), jnp.exp(-mt))[..., None]
    m_new = mt[..., -1]
    w_c = jnp.exp(inter[..., -1] - m_new)
    w_s = jnp.exp(b[..., -1:] - b + it - m_new[..., None])
    C_new = w_c[..., None, None] * C + jnp.einsum('bhs,bshv,bshd->bhvd', w_s, v, k)
    n_new = w_c[..., None] * n + jnp.einsum('bhs,bshd->bhd', w_s, k)
    return (C_new, n_new, m_new), h.transpose(0, 2, 1, 3)


def mlstm_output(h, o, norm_w, dtype):
    mu = h.mean(-1, keepdims=True)
    var = jnp.mean(jnp.square(h - mu), -1, keepdims=True)
    hn = (h - mu) * lax.rsqrt(var + LN_EPS) * norm_w.astype(jnp.float32)
    out = hn * jax.nn.sigmoid(o.astype(jnp.float32))
    return out.astype(dtype).reshape(*h.shape[:-2], MLSTM_WIDTH)


def peer_ffn(h, wq, keys, u_tab, v_tab):
    T = h.shape[0]
    q = (h @ wq).reshape(T, PEER_HEADS, 2, PEER_HALF)
    s = jnp.einsum('thcd,hcnd->thcn', q, keys).astype(jnp.float32)
    s_top, i_top = lax.top_k(s, PEER_TOPK)
    cand = s_top[:, :, 0, :, None] + s_top[:, :, 1, None, :]
    cidx = i_top[:, :, 0, :, None] * N_KEYS + i_top[:, :, 1, None, :]
    cs, pos = lax.top_k(cand.reshape(T, PEER_HEADS, PEER_TOPK * PEER_TOPK), PEER_TOPK)
    eidx = jnp.take_along_axis(cidx.reshape(T, PEER_HEADS, PEER_TOPK * PEER_TOPK), pos, axis=-1)
    g = jax.nn.softmax(cs, axis=-1)
    n_sel = PEER_HEADS * PEER_TOPK
    pad = (-T) % PEER_BLOCK
    hp = jnp.pad(h, ((0, pad), (0, 0)))
    ip = jnp.pad(eidx.reshape(T, n_sel), ((0, pad), (0, 0)))
    gp = jnp.pad(g.reshape(T, n_sel), ((0, pad), (0, 0)))
    nb = (T + pad) // PEER_BLOCK

    def block(args):
        xb, ib, gb = args
        act = jax.nn.gelu(jnp.einsum('td,ted->te', xb, u_tab[ib]).astype(jnp.float32), approximate=False)
        return jnp.einsum('te,ted->td', (gb * act).astype(v_tab.dtype), v_tab[ib])

    out = lax.map(block, (hp.reshape(nb, PEER_BLOCK, D_MODEL),
                          ip.reshape(nb, PEER_BLOCK, n_sel),
                          gp.reshape(nb, PEER_BLOCK, n_sel)))
    return out.reshape(nb * PEER_BLOCK, D_MODEL)[:T].astype(h.dtype)


def finish_layer(x, att, mls, w_out, g1, b1, wq, keys, u_tab, v_tab, g2, b2):
    mix = jnp.concatenate([att, mls], axis=-1) @ w_out
    h = layer_norm(DN_ALPHA * x + mix, g1, b1)
    f = peer_ffn(h.reshape(-1, D_MODEL), wq, keys, u_tab, v_tab).reshape(h.shape)
    return layer_norm(DN_ALPHA * h + f, g2, b2)


def shift_block(t):
    return jnp.concatenate([jnp.zeros_like(t[:, :1]), t[:, :-1]], axis=1)


def setup_inputs(seed: int = 0) -> dict:
    key = jax.random.key(seed)
    ks = jax.random.split(key, 24)
    nrm = jax.random.normal
    f32 = jnp.float32
    gate_bias = jnp.concatenate([
        0.1 * nrm(ks[8], (DEPTH, N_HEADS_M), f32),
        jnp.linspace(3.0, 6.0, N_HEADS_M, dtype=f32)[None, :] + 0.01 * nrm(ks[9], (DEPTH, N_HEADS_M), f32)], axis=-1)
    return {
        'x_prompt': nrm(ks[0], (BATCH, SEQ, D_MODEL), f32),
        'x_sample': nrm(ks[1], (DEC_BATCH, DEC_SEQ, D_MODEL), f32),
        'cache_k_win': nrm(ks[2], (DEPTH, DEC_BATCH, WINDOW, N_KV_HEADS, HEAD_DIM), f32),
        'cache_v_win': nrm(ks[3], (DEPTH, DEC_BATCH, WINDOW, N_KV_HEADS, HEAD_DIM), f32),
        'state_C': 0.05 * nrm(ks[4], (DEPTH, DEC_BATCH, N_HEADS_M, M_HEAD_DIM, M_HEAD_DIM), f32),
        'state_n': 0.05 * nrm(ks[5], (DEPTH, DEC_BATCH, N_HEADS_M, M_HEAD_DIM), f32),
        'state_m': nrm(ks[6], (DEPTH, DEC_BATCH, N_HEADS_M), f32),
        'w_in': nrm(ks[7], (DEPTH, D_MODEL, PROJ_WIDTH), f32) * D_MODEL ** -0.5,
        'gate_bias': gate_bias,
        'attn_sinks': 0.5 * nrm(ks[10], (DEPTH, N_HEADS_ATT), f32),
        'rel_bias': 0.1 * nrm(ks[11], (NUM_BUCKETS, N_HEADS_ATT), f32),
        'mlstm_norm_w': 1.0 + 0.01 * nrm(ks[12], (DEPTH, N_HEADS_M, M_HEAD_DIM), f32),
        'w_out': nrm(ks[13], (DEPTH, ATT_WIDTH + MLSTM_WIDTH, D_MODEL), f32) * (D_MODEL ** -0.5) * DN_BETA,
        'ln1_g': 1.0 + 0.01 * nrm(ks[14], (DEPTH, D_MODEL), f32),
        'ln1_b': 0.01 * nrm(ks[15], (DEPTH, D_MODEL), f32),
        'peer_wq': nrm(ks[16], (DEPTH, D_MODEL, PEER_HEADS * PEER_QDIM), f32) * D_MODEL ** -0.5,
        'peer_keys': nrm(ks[17], (DEPTH, PEER_HEADS, 2, N_KEYS, PEER_HALF), f32) * PEER_HALF ** -0.5,
        'peer_u': nrm(ks[18], (DEPTH, N_EXPERTS, D_MODEL), f32) * D_MODEL ** -0.5,
        'peer_v': nrm(ks[19], (DEPTH, N_EXPERTS, D_MODEL), f32) * DN_BETA,
        'ln2_g': 1.0 + 0.01 * nrm(ks[20], (DEPTH, D_MODEL), f32),
        'ln2_b': 0.01 * nrm(ks[21], (DEPTH, D_MODEL), f32),
    }


def reference(x_prompt, x_sample, cache_k_win, cache_v_win, state_C, state_n, state_m,
              w_in, gate_bias, attn_sinks, rel_bias, mlstm_norm_w, w_out, ln1_g, ln1_b,
              peer_wq, peer_keys, peer_u, peer_v, ln2_g, ln2_b):
    xp, xs = x_prompt, x_sample
    B, S, _ = xp.shape
    DB, DS, _ = xs.shape
    nb = S // ATT_BLOCK
    nc = S // MLSTM_CHUNK
    pk, pv, pC, pn, pm = [], [], [], [], []
    sk, sv, sC, sn, sm = [], [], [], [], []

    def to_chunks(t):
        return jnp.moveaxis(t.reshape(B, nc, MLSTM_CHUNK, *t.shape[2:]), 1, 0)

    for l in range(DEPTH):
        qa, ka, va, qm, km, vm, om, ig, lf = split_proj(xp, w_in[l], gate_bias[l])
        qb = qa.reshape(B, nb, ATT_BLOCK, N_KV_HEADS, GQA_GROUP, HEAD_DIM)
        kb = ka.reshape(B, nb, ATT_BLOCK, N_KV_HEADS, HEAD_DIM)
        vb = va.reshape(B, nb, ATT_BLOCK, N_KV_HEADS, HEAD_DIM)
        k_band = jnp.concatenate([shift_block(kb), kb], axis=2)
        v_band = jnp.concatenate([shift_block(vb), vb], axis=2)
        qpos = jnp.arange(S, dtype=jnp.int32).reshape(nb, ATT_BLOCK)
        kpos = jnp.concatenate([qpos - ATT_BLOCK, qpos], axis=1)
        att_p = swa_attend(qb, k_band, v_band, qpos, kpos, rel_bias, attn_sinks[l]).reshape(B, S, ATT_WIDTH)
        init = (jnp.zeros((B, N_HEADS_M, M_HEAD_DIM, M_HEAD_DIM), jnp.float32),
                jnp.zeros((B, N_HEADS_M, M_HEAD_DIM), jnp.float32),
                jnp.zeros((B, N_HEADS_M), jnp.float32))
        (Cp, n_p, m_p), hs = lax.scan(mlstm_chunk, init,
                                      (to_chunks(qm), to_chunks(km), to_chunks(vm), to_chunks(ig), to_chunks(lf)))
        hs = jnp.moveaxis(hs, 0, 1).reshape(B, S, N_HEADS_M, M_HEAD_DIM)
        mls_p = mlstm_output(hs, om, mlstm_norm_w[l], xp.dtype)
        pk.append(ka[:, -WINDOW:])
        pv.append(va[:, -WINDOW:])
        pC.append(Cp)
        pn.append(n_p)
        pm.append(m_p)
        xp = finish_layer(xp, att_p, mls_p, w_out[l], ln1_g[l], ln1_b[l], peer_wq[l], peer_keys[l],
                          peer_u[l], peer_v[l], ln2_g[l], ln2_b[l])

        qa, ka, va, qm, km, vm, om, ig, lf = split_proj(xs, w_in[l], gate_bias[l])
        k_all = jnp.concatenate([cache_k_win[l].astype(ka.dtype), ka], axis=1)
        v_all = jnp.concatenate([cache_v_win[l].astype(va.dtype), va], axis=1)
        new_pos = PAST_LEN + jnp.arange(DS, dtype=jnp.int32)
        qpos_s = new_pos[None]
        kpos_s = jnp.concatenate([PAST_LEN - WINDOW + jnp.arange(WINDOW, dtype=jnp.int32), new_pos])[None]
        att_s = swa_attend(qa[:, None], k_all[:, None], v_all[:, None], qpos_s, kpos_s,
                           rel_bias, attn_sinks[l])[:, 0].reshape(DB, DS, ATT_WIDTH)
        carry = (state_C[l].astype(jnp.float32), state_n[l].astype(jnp.float32), state_m[l].astype(jnp.float32))
        (Cs, n_s, m_s), hsm = mlstm_chunk(carry, (qm, km, vm, ig, lf))
        mls_s = mlstm_output(hsm, om, mlstm_norm_w[l], xs.dtype)
        sk.append(k_all[:, -WINDOW:])
        sv.append(v_all[:, -WINDOW:])
        sC.append(Cs)
        sn.append(n_s)
        sm.append(m_s)
        xs = finish_layer(xs, att_s, mls_s, w_out[l], ln1_g[l], ln1_b[l], peer_wq[l], peer_keys[l],
                          peer_u[l], peer_v[l], ln2_g[l], ln2_b[l])

    return (xp, xs, jnp.stack(pk), jnp.stack(pv), jnp.stack(pC), jnp.stack(pn), jnp.stack(pm),
            jnp.stack(sk), jnp.stack(sv), jnp.stack(sC), jnp.stack(sn), jnp.stack(sm))
```

```python
import functools
import math

import numpy as np
import jax
import jax.numpy as jnp
from jax import lax
from jax.experimental import pallas as pl
from jax.experimental.pallas import tpu as pltpu

F32 = jnp.float32
BF16 = jnp.bfloat16

D_MODEL = 1024
HEAD_DIM = 64
N_HEADS_ATT = 8
N_KV_HEADS = 2
WINDOW = 128
NUM_BUCKETS = 32
MAX_DISTANCE = 128
N_HEADS_M = 4
M_HEAD_DIM = 128
MLSTM_WIDTH = N_HEADS_M * M_HEAD_DIM
ATT_WIDTH = N_HEADS_ATT * HEAD_DIM
PEER_HEADS = 8
N_KEYS = 128
PEER_TOPK = 16
PEER_HALF = 128
LN_EPS = 1e-5
NEG_INF = -1e30
PAST_LEN = 16384

LANES = 128
SUBLANES = 8
VMEM_LIMIT = 56 * 1024 * 1024

TOKEN_TILE = 512
EXPERT_BLOCK = 1024
SAMPLE_PAD = 8


def _cparams(sem):
    return pltpu.CompilerParams(dimension_semantics=sem, vmem_limit_bytes=VMEM_LIMIT)


def _dot(a, b):
    return jnp.dot(a, b, preferred_element_type=F32)


def _dot_nt(a, b):
    return lax.dot_general(a, b, (((1,), (1,)), ((), ())), preferred_element_type=F32)


def _dot_tn(a, b):
    return lax.dot_general(a, b, (((0,), (0,)), ((), ())), preferred_element_type=F32)


def _layer_norm(x, g, b):
    mu = jnp.mean(x, axis=-1, keepdims=True)
    xc = x - mu
    var = jnp.mean(xc * xc, axis=-1, keepdims=True)
    return xc * lax.rsqrt(var + LN_EPS) * g + b


def _t5_bucket_np(dist):
    n = np.maximum(dist, 0)
    exact = NUM_BUCKETS // 2
    nf = np.maximum(n, 1).astype(np.float64)
    val = np.log(nf / exact) / math.log(MAX_DISTANCE / exact) * (NUM_BUCKETS - exact)
    frac = np.abs(val - np.round(val))
    assert not np.any((frac < 1e-6) & (n > exact) & (n != MAX_DISTANCE)), "bucket boundary is rounding sensitive"
    large = np.minimum(exact + np.floor(val + 1e-9).astype(np.int64), NUM_BUCKETS - 1)
    return np.where(n < exact, n, large).astype(np.int32)


def _bias_index_table(dist, valid):
    return np.where(valid, _t5_bucket_np(dist), -1).astype(np.int32)


def _bias_table_kernel(idx_ref, rb_ref, out_ref):
    idx = idx_ref[...]
    for h in range(N_HEADS_ATT):
        acc = jnp.full(idx.shape, NEG_INF, F32)
        for b in range(NUM_BUCKETS):
            acc = jnp.where(idx == b, rb_ref[b, h], acc)
        out_ref[h] = acc


def _bias_table(idx_np, rel_bias):
    r, c = idx_np.shape
    return pl.pallas_call(
        _bias_table_kernel,
        out_shape=jax.ShapeDtypeStruct((N_HEADS_ATT, r, c), F32),
        in_specs=[pl.BlockSpec(memory_space=pltpu.VMEM), pl.BlockSpec(memory_space=pltpu.SMEM)],
        out_specs=pl.BlockSpec(memory_space=pltpu.VMEM),
        name="bias_table",
    )(jnp.asarray(idx_np), rel_bias.astype(F32))


def _proj_kernel(x_ref, w_ref, wg_ref, q_ref, k_ref, v_ref, qm_ref, km_ref, vm_ref, om_ref, g_ref):
    x = x_ref[...]
    xb = x.astype(BF16)
    outs = (q_ref, k_ref, v_ref, qm_ref, km_ref, vm_ref, om_ref)
    for n, o_ref in enumerate(outs):
        z = _dot(xb, w_ref[:, n * 512:(n + 1) * 512])
        if o_ref is km_ref:
            z = z * (M_HEAD_DIM ** -0.5)
        o_ref[...] = z
    g_ref[...] = jnp.dot(x, wg_ref[...], precision=lax.Precision.HIGHEST, preferred_element_type=F32)


def _project(x, w_main, w_gate):
    t = x.shape[0]
    tm = TOKEN_TILE
    wide = jax.ShapeDtypeStruct((t, 512), F32)
    row = lambda i: (i, 0)
    fixed = lambda i: (0, 0)
    return pl.pallas_call(
        _proj_kernel,
        grid=(t // tm,),
        in_specs=[pl.BlockSpec((tm, D_MODEL), row),
                  pl.BlockSpec(w_main.shape, fixed),
                  pl.BlockSpec(w_gate.shape, fixed)],
        out_specs=[pl.BlockSpec((tm, 512), row)] * 7 + [pl.BlockSpec((tm, LANES), row)],
        out_shape=[wide] * 7 + [jax.ShapeDtypeStruct((t, LANES), F32)],
        compiler_params=_cparams(("parallel",)),
        name="in_proj",
    )(x, w_main, w_gate)


def _projection_weights(w_in):
    q_a = w_in[:, 0:512]
    k_a = w_in[:, 512:640]
    v_a = w_in[:, 640:768]
    rest = w_in[:, 768:768 + 4 * MLSTM_WIDTH]
    gates = w_in[:, 768 + 4 * MLSTM_WIDTH:]
    zero = jnp.zeros((D_MODEL, HEAD_DIM), w_in.dtype)

    def lo_hi(w):
        cols = []
        for kv in range(N_KV_HEADS):
            wk = w[:, kv * HEAD_DIM:(kv + 1) * HEAD_DIM]
            cols += [wk, zero, zero, wk]
        return jnp.concatenate(cols, axis=1)

    w_main = jnp.concatenate([q_a, lo_hi(k_a), lo_hi(v_a), rest], axis=1).astype(BF16)
    w_gate = jnp.pad(gates, ((0, 0), (0, LANES - gates.shape[1]))).astype(F32)
    return w_main, w_gate


def _softmax_pieces(pieces, sink):
    mx = sink
    for s in pieces:
        mx = jnp.maximum(mx, jnp.max(s, axis=-1, keepdims=True))
    es = [jnp.exp(s - mx) for s in pieces]
    den = jnp.exp(sink - mx)
    for e in es:
        den = den + jnp.sum(e, axis=-1, keepdims=True)
    inv = 1.0 / den
    return [e * inv for e in es]


def _attn_prompt_kernel(q_ref, kp_ref, kc_ref, vp_ref, vc_ref, bias_ref, sink_ref, o_ref):
    n = pl.program_id(0)
    scale = HEAD_DIM ** -0.5
    q = q_ref[...].astype(BF16)
    kp = kp_ref[...].astype(BF16)
    kc = kc_ref[...].astype(BF16)
    vp = vp_ref[...].astype(BF16)
    vc = vc_ref[...].astype(BF16)
    for pair in range(N_HEADS_ATT // 2):
        kv = pair // 2
        qp = q[:, pair * LANES:(pair + 1) * LANES]
        acc = jnp.zeros((WINDOW, LANES), F32)
        for x in range(2):
            h = pair * 2 + x
            col = (kv * 2 + x) * LANES
            sink = sink_ref[h][0:1, 0:1]
            bias = bias_ref[h]
            s_prev = _dot_nt(qp, kp[:, col:col + LANES]) * scale + bias[:, :WINDOW]
            s_prev = jnp.where(n > 0, s_prev, NEG_INF)
            s_cur = _dot_nt(qp, kc[:, col:col + LANES]) * scale + bias[:, WINDOW:]
            p_prev, p_cur = _softmax_pieces([s_prev, s_cur], sink)
            acc = acc + _dot(p_prev.astype(BF16), vp[:, col:col + LANES])
            acc = acc + _dot(p_cur.astype(BF16), vc[:, col:col + LANES])
        o_ref[:, pair * LANES:(pair + 1) * LANES] = acc


def _attn_prompt(q, kx, vx, bias, sinks):
    s = q.shape[0]
    nb = s // WINDOW
    cur = lambda n: (n, 0)
    prev = lambda n: (jnp.maximum(n - 1, 0), 0)
    blk = (WINDOW, 512)
    return pl.pallas_call(
        _attn_prompt_kernel,
        grid=(nb,),
        in_specs=[pl.BlockSpec(blk, cur), pl.BlockSpec(blk, prev), pl.BlockSpec(blk, cur),
                  pl.BlockSpec(blk, prev), pl.BlockSpec(blk, cur),
                  pl.BlockSpec(bias.shape, lambda n: (0, 0, 0)),
                  pl.BlockSpec(sinks.shape, lambda n: (0, 0, 0))],
        out_specs=pl.BlockSpec(blk, cur),
        out_shape=jax.ShapeDtypeStruct((s, ATT_WIDTH), F32),
        compiler_params=_cparams(("parallel",)),
        name="attn_prompt",
    )(q, kx, kx, vx, vx, bias, sinks)


def _attn_sample_kernel(q_ref, kn_ref, vn_ref, ck_ref, cv_ref, bc_ref, bn_ref, sink_ref, o_ref):
    scale = HEAD_DIM ** -0.5
    q_all = q_ref[0].astype(BF16)
    kn = kn_ref[0]
    vn = vn_ref[0]
    ck = ck_ref[0]
    cv = cv_ref[0]
    lane = lax.broadcasted_iota(jnp.int32, (WINDOW, LANES), 1)
    low = lane < HEAD_DIM
    ck_sw = pltpu.roll(ck, HEAD_DIM, 1)
    cv_sw = pltpu.roll(cv, HEAD_DIM, 1)
    pad = jnp.zeros((WINDOW - SAMPLE_PAD, LANES), F32)
    rows = 2 * SAMPLE_PAD
    for kv in range(N_KV_HEADS):
        qk = q_all[kv * rows:(kv + 1) * rows]
        acc = jnp.zeros((rows, LANES), F32)
        for x in range(2):
            col = (kv * 2 + x) * LANES
            keep = low if x == 0 else jnp.logical_not(low)
            src_k, src_v = (ck, cv) if kv == x else (ck_sw, cv_sw)
            kc = jnp.where(keep, src_k, 0.0).astype(BF16)
            vc = jnp.where(keep, src_v, 0.0).astype(BF16)
            knx = jnp.concatenate([kn[:, col:col + LANES], pad], axis=0).astype(BF16)
            vnx = jnp.concatenate([vn[:, col:col + LANES], pad], axis=0).astype(BF16)
            s_c = _dot_nt(qk, kc) * scale + bc_ref[kv, x]
            s_n = _dot_nt(qk, knx) * scale + bn_ref[kv, x]
            p_c, p_n = _softmax_pieces([s_c, s_n], sink_ref[kv, x][:, 0:1])
            acc = acc + _dot(p_c.astype(BF16), vc) + _dot(p_n.astype(BF16), vnx)
        o_ref[0, kv * rows:(kv + 1) * rows, :] = acc


def _attn_sample(q, knew, vnew, cache_k, cache_v, bias_c, bias_n, sinks):
    db = q.shape[0]
    b3 = lambda b: (b, 0, 0)
    f4 = lambda b: (0, 0, 0, 0)
    return pl.pallas_call(
        _attn_sample_kernel,
        grid=(db,),
        in_specs=[pl.BlockSpec((1,) + q.shape[1:], b3),
                  pl.BlockSpec((1,) + knew.shape[1:], b3),
                  pl.BlockSpec((1,) + vnew.shape[1:], b3),
                  pl.BlockSpec((1, WINDOW, LANES), b3),
                  pl.BlockSpec((1, WINDOW, LANES), b3),
                  pl.BlockSpec(bias_c.shape, f4),
                  pl.BlockSpec(bias_n.shape, f4),
                  pl.BlockSpec(sinks.shape, f4)],
        out_specs=pl.BlockSpec((1,) + q.shape[1:], b3),
        out_shape=jax.ShapeDtypeStruct(q.shape, F32),
        compiler_params=_cparams(("parallel",)),
        name="attn_sample",
    )(q, knew, vnew, cache_k, cache_v, bias_c, bias_n, sinks)


def _log_sigmoid(x):
    return jnp.minimum(x, 0.0) - jnp.log1p(jnp.exp(-jnp.abs(x)))


def _cumsum_rows(x):
    rows = x.shape[0]
    idx = lax.broadcasted_iota(jnp.int32, x.shape, 0)
    k = 1
    while k < rows:
        x = x + jnp.where(idx >= k, pltpu.roll(x, k, 0), 0.0)
        k *= 2
    return x


def _mlstm_kernel(q_ref, k_ref, v_ref, o_ref, g_ref, gb_ref, nw_ref, c0_ref, n0_ref, m0_ref,
                  h_ref, c_out_ref, n_out_ref, m_out_ref, c_scr, n_scr, m_scr, *, rows, valid):
    c_idx = pl.program_id(1)
    keys = M_HEAD_DIM

    @pl.when(c_idx == 0)
    def _():
        c_scr[...] = c0_ref[0]
        n_scr[...] = n0_ref[0]
        m_scr[...] = m0_ref[0]

    g = g_ref[...] + gb_ref[...]
    ridx = lax.broadcasted_iota(jnp.int32, g.shape, 0)
    ig_all = g
    lf_all = _log_sigmoid(g)
    if valid < rows:
        ig_all = jnp.where(ridx < valid, ig_all, NEG_INF)
        lf_all = jnp.where(ridx < valid, lf_all, 0.0)
    b_all = _cumsum_rows(lf_all)

    t_idx = lax.broadcasted_iota(jnp.int32, (rows, keys), 0)
    s_idx = lax.broadcasted_iota(jnp.int32, (rows, keys), 1)
    eye = t_idx == s_idx
    causal = s_idx <= t_idx

    def pad_rows(x):
        if rows == keys:
            return x
        return jnp.concatenate([x, jnp.zeros((keys - rows, x.shape[1]), x.dtype)], axis=0)

    for h in range(N_HEADS_M):
        sl = slice(h * M_HEAD_DIM, (h + 1) * M_HEAD_DIM)
        q = q_ref[:, sl]
        k = k_ref[:, sl]
        v = v_ref[:, sl]
        qb = q.astype(BF16)
        kb = pad_rows(k).astype(BF16)
        vb = pad_rows(v).astype(BF16)
        bc = b_all[:, N_HEADS_M + h:N_HEADS_M + h + 1]
        igc = ig_all[:, h:h + 1]
        brow = jnp.sum(jnp.where(eye, bc, 0.0), axis=0, keepdims=True)
        igrow = jnp.sum(jnp.where(eye, igc, 0.0), axis=0, keepdims=True)
        logd = jnp.where(causal, bc - brow + igrow, -jnp.inf)
        m_prev = m_scr[h][:, 0:1]
        inter = bc + m_prev
        mt = jnp.maximum(inter, jnp.max(logd, axis=-1, keepdims=True))
        w_inter = jnp.exp(inter - mt)
        a = jnp.exp(logd - mt) * _dot_nt(qb, kb)
        c_old = c_scr[h]
        n_old = n_scr[h]
        num = _dot(a.astype(BF16), vb) + w_inter * _dot_nt(qb, c_old.astype(BF16))
        den = jnp.sum(a, axis=-1, keepdims=True) + w_inter * jnp.sum(q * n_old, axis=-1, keepdims=True)
        hid = num / jnp.maximum(jnp.abs(den), jnp.exp(-mt))
        mu = jnp.mean(hid, axis=-1, keepdims=True)
        hc = hid - mu
        var = jnp.mean(hc * hc, axis=-1, keepdims=True)
        hn = hc * lax.rsqrt(var + LN_EPS) * nw_ref[:, sl]
        h_ref[:, sl] = hn * jax.nn.sigmoid(o_ref[:, sl])
        m_new = mt[rows - 1:rows]
        w_c = jnp.exp(inter[rows - 1:rows] - m_new)
        w_s = jnp.exp(bc[rows - 1:rows] - bc + igc - m_new)
        vs = pad_rows(v * w_s).astype(BF16)
        c_scr[h] = w_c * c_old + _dot_tn(vs, kb)
        n_scr[h] = w_c * n_old + jnp.sum(w_s * k, axis=0, keepdims=True)
        m_scr[h] = jnp.broadcast_to(m_new, (1, LANES))

    @pl.when(c_idx == pl.num_programs(1) - 1)
    def _():
        c_out_ref[0] = c_scr[...]
        n_out_ref[0] = n_scr[...]
        m_out_ref[0] = m_scr[...]


def _mlstm(q, k, v, o, gates, gate_bias_row, norm_w_row, c0, n0, m0, *, rows, valid):
    batch = c0.shape[0]
    total = q.shape[0]
    nc = total // (batch * rows)
    seq = lambda b, c: (b * nc + c, 0)
    fixed = lambda b, c: (0, 0)
    st4 = lambda b, c: (b, 0, 0, 0)
    wide = pl.BlockSpec((rows, MLSTM_WIDTH), seq)
    c_spec = pl.BlockSpec((1, N_HEADS_M, M_HEAD_DIM, M_HEAD_DIM), st4)
    v_spec = pl.BlockSpec((1, N_HEADS_M, 1, LANES), st4)
    return pl.pallas_call(
        functools.partial(_mlstm_kernel, rows=rows, valid=valid),
        grid=(batch, nc),
        in_specs=[wide, wide, wide, wide,
                  pl.BlockSpec((rows, LANES), seq),
                  pl.BlockSpec((1, LANES), fixed),
                  pl.BlockSpec((1, MLSTM_WIDTH), fixed),
                  c_spec, v_spec, v_spec],
        out_specs=[wide, c_spec, v_spec, v_spec],
        out_shape=[jax.ShapeDtypeStruct((total, MLSTM_WIDTH), F32),
                   jax.ShapeDtypeStruct(c0.shape, F32),
                   jax.ShapeDtypeStruct(n0.shape, F32),
                   jax.ShapeDtypeStruct(m0.shape, F32)],
        scratch_shapes=[pltpu.VMEM((N_HEADS_M, M_HEAD_DIM, M_HEAD_DIM), F32),
                        pltpu.VMEM((N_HEADS_M, 1, LANES), F32),
                        pltpu.VMEM((N_HEADS_M, 1, LANES), F32)],
        compiler_params=_cparams(("parallel", "arbitrary")),
        name="mlstm",
    )(q, k, v, o, gates, gate_bias_row, norm_w_row, c0, n0, m0)


def _top_values(x, count):
    tops = []
    for _ in range(count):
        m = jnp.max(x, axis=0, keepdims=True)
        tops.append(m)
        x = jnp.where(x == m, -jnp.inf, x)
    return jnp.concatenate(tops, axis=0)


def _mid_kernel(att_ref, mls_ref, x_ref, woa_ref, wom_ref, g1_ref, b1_ref, wqt_ref, keys_ref,
                h_ref, s1_ref, s2_ref, aux_ref, s_scr, top_scr, *, alpha):
    tm = x_ref.shape[0]
    nch = tm // LANES
    mix = _dot(att_ref[...].astype(BF16), woa_ref[...]) + _dot(mls_ref[...].astype(BF16), wom_ref[...])
    hid = _layer_norm(alpha * x_ref[...] + mix, g1_ref[...], b1_ref[...])
    h_ref[...] = hid
    q_t = _dot_nt(wqt_ref[...], hid.astype(BF16))
    for hc in range(2 * PEER_HEADS):
        s_t = _dot(keys_ref[hc], q_t[hc * PEER_HALF:(hc + 1) * PEER_HALF].astype(BF16))
        out = s1_ref if hc % 2 == 0 else s2_ref
        for tc in range(nch):
            piece = s_t[:, tc * LANES:(tc + 1) * LANES]
            s_scr[hc, tc] = piece
            out[hc // 2, tc] = piece

    def top_body(it, carry):
        hc = it // nch
        tc = it % nch
        top_scr[hc, tc] = _top_values(s_scr[hc, tc], PEER_TOPK)
        return carry

    lax.fori_loop(0, 2 * PEER_HEADS * nch, top_body, 0)

    def cand_body(it, carry):
        h = it // nch
        tc = it % nch
        t1 = top_scr[2 * h, tc]
        t2 = top_scr[2 * h + 1, tc]
        cands = [t1[r:r + 1] + t2[0:SUBLANES] for r in range(SUBLANES)]
        cands.append(t1[0:1] + t2[SUBLANES:])
        cands.append(t1[SUBLANES:] + t2[0:1])
        cand = jnp.concatenate(cands, axis=0)
        cmax = t1[0:1] + t2[0:1]
        z = jnp.zeros_like(cmax)
        tau = cmax
        for _ in range(PEER_TOPK):
            tau = jnp.max(cand, axis=0, keepdims=True)
            z = z + jnp.exp(tau - cmax)
            cand = jnp.where(cand == tau, -jnp.inf, cand)
        fill = jnp.zeros((SUBLANES - 3, LANES), F32)
        aux_ref[h, tc] = jnp.concatenate([tau, t1[0:1] + jnp.log(z), t2[0:1], fill], axis=0)
        return carry

    lax.fori_loop(0, PEER_HEADS * nch, cand_body, 0)


def _mid(att, mls, x, wo_a, wo_m, g1, b1, wq_t, keys, alpha):
    t = x.shape[0]
    tm = TOKEN_TILE
    nch = tm // LANES
    row = lambda i: (i, 0)
    fixed2 = lambda i: (0, 0)
    fixed3 = lambda i: (0, 0, 0)
    tile4 = lambda i: (0, i, 0, 0)
    s_shape = jax.ShapeDtypeStruct((PEER_HEADS, t // LANES, N_KEYS, LANES), F32)
    return pl.pallas_call(
        functools.partial(_mid_kernel, alpha=alpha),
        grid=(t // tm,),
        in_specs=[pl.BlockSpec((tm, ATT_WIDTH), row), pl.BlockSpec((tm, MLSTM_WIDTH), row),
                  pl.BlockSpec((tm, D_MODEL), row),
                  pl.BlockSpec(wo_a.shape, fixed2), pl.BlockSpec(wo_m.shape, fixed2),
                  pl.BlockSpec(g1.shape, fixed2), pl.BlockSpec(b1.shape, fixed2),
                  pl.BlockSpec(wq_t.shape, fixed2), pl.BlockSpec(keys.shape, fixed3)],
        out_specs=[pl.BlockSpec((tm, D_MODEL), row),
                   pl.BlockSpec((PEER_HEADS, nch, N_KEYS, LANES), tile4),
                   pl.BlockSpec((PEER_HEADS, nch, N_KEYS, LANES), tile4),
                   pl.BlockSpec((PEER_HEADS, nch, SUBLANES, LANES), tile4)],
        out_shape=[jax.ShapeDtypeStruct((t, D_MODEL), F32), s_shape, s_shape,
                   jax.ShapeDtypeStruct((PEER_HEADS, t // LANES, SUBLANES, LANES), F32)],
        scratch_shapes=[pltpu.VMEM((2 * PEER_HEADS, nch, N_KEYS, LANES), F32),
                        pltpu.VMEM((2 * PEER_HEADS, nch, PEER_TOPK, LANES), F32)],
        compiler_params=_cparams(("parallel",)),
        name="out_proj_router",
    )(att, mls, x, wo_a, wo_m, g1, b1, wq_t, keys)


def _gelu(x):
    return 0.5 * x * (1.0 + lax.erf(x * (2.0 ** -0.5)))


def _peer_kernel(h_ref, s1_ref, s2_ref, aux_ref, u_ref, vt_ref, g2_ref, b2_ref, y_ref,
                 hb_scr, a_scr, b_scr, act_scr, w_scr, acc_scr, *, alpha):
    j = pl.program_id(1)
    tm = h_ref.shape[0]
    nch = tm // LANES
    rows_per_step = EXPERT_BLOCK // N_KEYS

    @pl.when(j == 0)
    def _():
        hb_scr[...] = h_ref[...].astype(BF16)
        acc_scr[...] = jnp.zeros_like(acc_scr)
        for h in range(PEER_HEADS):
            for tc in range(nch):
                aux = aux_ref[h, tc]
                a_scr[h, tc] = jnp.exp(s1_ref[h, tc] - aux[1:2])
                b_scr[h, tc] = jnp.exp(s2_ref[h, tc] - aux[2:3])

    act_scr[...] = _dot_nt(u_ref[...], hb_scr[...])
    i0 = pl.multiple_of(j * rows_per_step, rows_per_step)

    def body(it, carry):
        il = it // nch
        tc = it % nch
        gate = jnp.zeros((N_KEYS, LANES), F32)
        for h in range(PEER_HEADS):
            s1 = s1_ref[h, tc, pl.ds(i0 + il, 1), :]
            a = a_scr[h, tc, pl.ds(i0 + il, 1), :]
            tau = aux_ref[h, tc, 0:1, :]
            sel = (s1 + s2_ref[h, tc]) >= tau
            gate = gate + jnp.where(sel, a * b_scr[h, tc], 0.0)
        r0 = pl.multiple_of(il * N_KEYS, N_KEYS)
        c0 = pl.multiple_of(tc * LANES, LANES)
        act = act_scr[pl.ds(r0, N_KEYS), pl.ds(c0, LANES)]
        w_scr[pl.ds(r0, N_KEYS), pl.ds(c0, LANES)] = (gate * _gelu(act)).astype(BF16)
        return carry

    lax.fori_loop(0, rows_per_step * nch, body, 0)
    acc_scr[...] += _dot(vt_ref[...], w_scr[...])

    @pl.when(j == pl.num_programs(1) - 1)
    def _():
        f = acc_scr[...].T
        y_ref[...] = _layer_norm(alpha * h_ref[...] + f, g2_ref[...], b2_ref[...])


def _peer(hid, s1, s2, aux, u_b, vt_b, g2, b2, alpha):
    t = hid.shape[0]
    tm = TOKEN_TILE
    nch = tm // LANES
    n_exp = u_b.shape[0]
    row = lambda i, j: (i, 0)
    tile4 = lambda i, j: (0, i, 0, 0)
    fixed2 = lambda i, j: (0, 0)
    return pl.pallas_call(
        functools.partial(_peer_kernel, alpha=alpha),
        grid=(t // tm, n_exp // EXPERT_BLOCK),
        in_specs=[pl.BlockSpec((tm, D_MODEL), row),
                  pl.BlockSpec((PEER_HEADS, nch, N_KEYS, LANES), tile4),
                  pl.BlockSpec((PEER_HEADS, nch, N_KEYS, LANES), tile4),
                  pl.BlockSpec((PEER_HEADS, nch, SUBLANES, LANES), tile4),
                  pl.BlockSpec((EXPERT_BLOCK, D_MODEL), lambda i, j: (j, 0)),
                  pl.BlockSpec((D_MODEL, EXPERT_BLOCK), lambda i, j: (0, j)),
                  pl.BlockSpec(g2.shape, fixed2), pl.BlockSpec(b2.shape, fixed2)],
        out_specs=pl.BlockSpec((tm, D_MODEL), row),
        out_shape=jax.ShapeDtypeStruct((t, D_MODEL), F32),
        scratch_shapes=[pltpu.VMEM((tm, D_MODEL), BF16),
                        pltpu.VMEM((PEER_HEADS, nch, N_KEYS, LANES), F32),
                        pltpu.VMEM((PEER_HEADS, nch, N_KEYS, LANES), F32),
                        pltpu.VMEM((EXPERT_BLOCK, tm), F32),
                        pltpu.VMEM((EXPERT_BLOCK, tm), BF16),
                        pltpu.VMEM((D_MODEL, tm), F32)],
        compiler_params=_cparams(("parallel", "arbitrary")),
        name="peer_experts",
    )(hid, s1, s2, aux, u_b, vt_b, g2, b2)


def _finish(x, att, mls, lw, alpha):
    hid, s1, s2, aux = _mid(att, mls, x, lw["wo_a"], lw["wo_m"], lw["g1"], lw["b1"], lw["wq_t"], lw["keys"], alpha)
    return _peer(hid, s1, s2, aux, lw["u"], lw["v_t"], lw["g2"], lw["b2"], alpha)


def _prompt_bias_index():
    q = np.arange(WINDOW)[:, None]
    kband = np.arange(2 * WINDOW)[None, :]
    dist = q + WINDOW - kband
    return _bias_index_table(dist, (dist >= 0) & (dist < WINDOW))


def _sample_bias_index(steps):
    q = np.arange(SAMPLE_PAD)[:, None]
    c = np.arange(WINDOW)[None, :]
    dist_c = WINDOW - c + q
    idx_c = _bias_index_table(dist_c, (dist_c >= 0) & (dist_c < WINDOW))
    jn = np.arange(LANES)[None, :]
    dist_n = q - jn
    idx_n = _bias_index_table(dist_n, (dist_n >= 0) & (dist_n < WINDOW) & (jn < steps))
    return np.concatenate([idx_c, idx_n], axis=1)


def _per_row_head_layout(tab):
    _, steps, nk = tab.shape
    t = tab.reshape(N_KV_HEADS, 2, 2, steps, nk)
    return t.transpose(0, 2, 1, 3, 4).reshape(N_KV_HEADS, 2, 2 * steps, nk)


def kernel(x_prompt, x_sample, cache_k_win, cache_v_win, state_C, state_n, state_m, w_in, gate_bias,
           attn_sinks, rel_bias, mlstm_norm_w, w_out, ln1_g, ln1_b, peer_wq, peer_keys, peer_u, peer_v,
           ln2_g, ln2_b):
    depth = w_in.shape[0]
    alpha = (2 * depth) ** 0.25
    bsz, seq, _ = x_prompt.shape
    db, ds, _ = x_sample.shape
    assert bsz == 1 and seq % TOKEN_TILE == 0 and ds <= SAMPLE_PAD and (db * ds) % TOKEN_TILE == 0
    assert PAST_LEN >= WINDOW

    bias_p = _bias_table(_prompt_bias_index(), rel_bias)
    bias_s = _bias_table(_sample_bias_index(ds), rel_bias)
    bias_sc = _per_row_head_layout(bias_s[:, :, :WINDOW])
    bias_sn = _per_row_head_layout(bias_s[:, :, WINDOW:])

    xp = x_prompt[0]
    xs = x_sample
    pk, pv, pc, pn, pm = [], [], [], [], []
    sk, sv, sc, sn, sm = [], [], [], [], []
    for l in range(depth):
        w_main, w_gate = _projection_weights(w_in[l])
        gb_row = jnp.pad(gate_bias[l], (0, LANES - 2 * N_HEADS_M))[None, :].astype(F32)
        nw_row = mlstm_norm_w[l].reshape(1, MLSTM_WIDTH).astype(F32)
        sinks = attn_sinks[l].astype(F32)
        sinks_p = jnp.broadcast_to(sinks[:, None, None], (N_HEADS_ATT, 1, LANES))
        sinks_s = _per_row_head_layout(
            jnp.broadcast_to(sinks[:, None, None], (N_HEADS_ATT, SAMPLE_PAD, LANES)))
        lw = dict(
            wo_a=w_out[l, :ATT_WIDTH].astype(BF16), wo_m=w_out[l, ATT_WIDTH:].astype(BF16),
            g1=ln1_g[l][None, :], b1=ln1_b[l][None, :],
            wq_t=peer_wq[l].T.astype(BF16),
            keys=peer_keys[l].reshape(2 * PEER_HEADS, N_KEYS, PEER_HALF).astype(BF16),
            u=peer_u[l].astype(BF16), v_t=peer_v[l].T.astype(BF16),
            g2=ln2_g[l][None, :], b2=ln2_b[l][None, :])

        q_a, k_x, v_x, q_m, k_m, v_m, o_m, gates = _project(xp, w_main, w_gate)
        att = _attn_prompt(q_a, k_x, v_x, bias_p, sinks_p)
        zeros_c = jnp.zeros((1, N_HEADS_M, M_HEAD_DIM, M_HEAD_DIM), F32)
        zeros_v = jnp.zeros((1, N_HEADS_M, 1, LANES), F32)
        mls, c_p, n_p, m_p = _mlstm(q_m, k_m, v_m, o_m, gates, gb_row, nw_row, zeros_c, zeros_v, zeros_v,
                                    rows=M_HEAD_DIM, valid=M_HEAD_DIM)
        tail_k = k_x[seq - WINDOW:].reshape(WINDOW, N_KV_HEADS, 2, LANES)[:, :, 0, :HEAD_DIM]
        tail_v = v_x[seq - WINDOW:].reshape(WINDOW, N_KV_HEADS, 2, LANES)[:, :, 0, :HEAD_DIM]
        pk.append(tail_k[None])
        pv.append(tail_v[None])
        pc.append(c_p)
        pn.append(n_p[:, :, 0, :])
        pm.append(m_p[:, :, 0, 0])
        xp = _finish(xp, att, mls, lw, alpha)

        xs_pad = jnp.pad(xs, ((0, 0), (0, SAMPLE_PAD - ds), (0, 0))).reshape(db * SAMPLE_PAD, D_MODEL)
        q_a, k_x, v_x, q_m, k_m, v_m, o_m, gates = _project(xs_pad, w_main, w_gate)
        q_rows = q_a.reshape(db, SAMPLE_PAD, 4, LANES).transpose(0, 2, 1, 3).reshape(db, 4 * SAMPLE_PAD, LANES)
        att_rows = _attn_sample(q_rows, k_x.reshape(db, SAMPLE_PAD, 512), v_x.reshape(db, SAMPLE_PAD, 512),
                                cache_k_win[l].reshape(db, WINDOW, LANES).astype(F32),
                                cache_v_win[l].reshape(db, WINDOW, LANES).astype(F32),
                                bias_sc, bias_sn, sinks_s)
        att = att_rows.reshape(db, 4, SAMPLE_PAD, LANES).transpose(0, 2, 1, 3)[:, :ds].reshape(db * ds, ATT_WIDTH)
        c0 = state_C[l].astype(F32)
        n0 = state_n[l].astype(F32)[:, :, None, :]
        m0 = jnp.broadcast_to(state_m[l].astype(F32)[:, :, None, None], (db, N_HEADS_M, 1, LANES))
        mls, c_s, n_s, m_s = _mlstm(q_m, k_m, v_m, o_m, gates, gb_row, nw_row, c0, n0, m0,
                                    rows=SAMPLE_PAD, valid=ds)
        mls = mls.reshape(db, SAMPLE_PAD, MLSTM_WIDTH)[:, :ds].reshape(db * ds, MLSTM_WIDTH)
        k_new = k_x.reshape(db, SAMPLE_PAD, N_KV_HEADS, 2, LANES)[:, :ds, :, 0, :HEAD_DIM]
        v_new = v_x.reshape(db, SAMPLE_PAD, N_KV_HEADS, 2, LANES)[:, :ds, :, 0, :HEAD_DIM]
        sk.append(jnp.concatenate([cache_k_win[l].astype(F32), k_new], axis=1)[:, -WINDOW:])
        sv.append(jnp.concatenate([cache_v_win[l].astype(F32), v_new], axis=1)[:, -WINDOW:])
        sc.append(c_s)
        sn.append(n_s[:, :, 0, :])
        sm.append(m_s[:, :, 0, 0])
        xs = _finish(xs.reshape(db * ds, D_MODEL), att, mls, lw, alpha).reshape(db, ds, D_MODEL)

    return (xp[None], xs, jnp.stack(pk), jnp.stack(pv), jnp.stack(pc), jnp.stack(pn), jnp.stack(pm),
            jnp.stack(sk), jnp.stack(sv), jnp.stack(sc), jnp.stack(sn), jnp.stack(sm))
```

```python
import functools
import math

import numpy as np
import jax
import jax.numpy as jnp
from jax import lax
from jax.experimental import pallas as pl
from jax.experimental.pallas import tpu as pltpu

F32 = jnp.float32
BF16 = jnp.bfloat16

D_MODEL = 1024
HEAD_DIM = 64
N_HEADS_ATT = 8
N_KV_HEADS = 2
WINDOW = 128
NUM_BUCKETS = 32
MAX_DISTANCE = 128
N_HEADS_M = 4
M_HEAD_DIM = 128
MLSTM_WIDTH = N_HEADS_M * M_HEAD_DIM
ATT_WIDTH = N_HEADS_ATT * HEAD_DIM
PEER_HEADS = 8
N_KEYS = 128
PEER_TOPK = 16
PEER_HALF = 128
LN_EPS = 1e-5
NEG_INF = -1e30
PAST_LEN = 16384

LANES = 128
SUBLANES = 8
VMEM_LIMIT = 56 * 1024 * 1024

TOKEN_TILE = 512
EXPERT_BLOCK = 1024
SAMPLE_PAD = 8


def _cparams(sem, flags=None):
    return pltpu.CompilerParams(dimension_semantics=sem, vmem_limit_bytes=VMEM_LIMIT, flags=flags)


def _dot(a, b):
    return jnp.dot(a, b, preferred_element_type=F32)


def _dot_nt(a, b):
    return lax.dot_general(a, b, (((1,), (1,)), ((), ())), preferred_element_type=F32)


def _dot_tn(a, b):
    return lax.dot_general(a, b, (((0,), (0,)), ((), ())), preferred_element_type=F32)


def _layer_norm(x, g, b):
    mu = jnp.mean(x, axis=-1, keepdims=True)
    xc = x - mu
    var = jnp.mean(xc * xc, axis=-1, keepdims=True)
    return xc * lax.rsqrt(var + LN_EPS) * g + b


def _t5_bucket_np(dist):
    n = np.maximum(dist, 0)
    exact = NUM_BUCKETS // 2
    nf = np.maximum(n, 1).astype(np.float64)
    val = np.log(nf / exact) / math.log(MAX_DISTANCE / exact) * (NUM_BUCKETS - exact)
    frac = np.abs(val - np.round(val))
    assert not np.any((frac < 1e-6) & (n > exact) & (n != MAX_DISTANCE)), "bucket boundary is rounding sensitive"
    large = np.minimum(exact + np.floor(val + 1e-9).astype(np.int64), NUM_BUCKETS - 1)
    return np.where(n < exact, n, large).astype(np.int32)


def _bias_index_table(dist, valid):
    return np.where(valid, _t5_bucket_np(dist), -1).astype(np.int32)


def _bias_table_kernel(idx_ref, rb_ref, out_ref):
    idx = idx_ref[...]
    for h in range(N_HEADS_ATT):
        acc = jnp.full(idx.shape, NEG_INF, F32)
        for b in range(NUM_BUCKETS):
            acc = jnp.where(idx == b, rb_ref[b, h], acc)
        out_ref[h] = acc


def _bias_table(idx_np, rel_bias):
    r, c = idx_np.shape
    return pl.pallas_call(
        _bias_table_kernel,
        out_shape=jax.ShapeDtypeStruct((N_HEADS_ATT, r, c), F32),
        in_specs=[pl.BlockSpec(memory_space=pltpu.VMEM), pl.BlockSpec(memory_space=pltpu.SMEM)],
        out_specs=pl.BlockSpec(memory_space=pltpu.VMEM),
        name="bias_table",
    )(jnp.asarray(idx_np), rel_bias.astype(F32))


def _proj_kernel(x_ref, w_ref, wg_ref, q_ref, k_ref, v_ref, qm_ref, km_ref, vm_ref, om_ref, g_ref):
    x = x_ref[...]
    xb = x.astype(BF16)
    outs = (q_ref, k_ref, v_ref, qm_ref, km_ref, vm_ref, om_ref)
    for n, o_ref in enumerate(outs):
        z = _dot(xb, w_ref[:, n * 512:(n + 1) * 512])
        if o_ref is km_ref:
            z = z * (M_HEAD_DIM ** -0.5)
        o_ref[...] = z
    g_ref[...] = jnp.dot(x, wg_ref[...], precision=lax.Precision.HIGHEST, preferred_element_type=F32)


def _project(x, w_main, w_gate):
    t = x.shape[0]
    tm = TOKEN_TILE
    wide = jax.ShapeDtypeStruct((t, 512), F32)
    row = lambda i: (i, 0)
    fixed = lambda i: (0, 0)
    return pl.pallas_call(
        _proj_kernel,
        grid=(t // tm,),
        in_specs=[pl.BlockSpec((tm, D_MODEL), row),
                  pl.BlockSpec(w_main.shape, fixed),
                  pl.BlockSpec(w_gate.shape, fixed)],
        out_specs=[pl.BlockSpec((tm, 512), row)] * 7 + [pl.BlockSpec((tm, LANES), row)],
        out_shape=[wide] * 7 + [jax.ShapeDtypeStruct((t, LANES), F32)],
        compiler_params=_cparams(("parallel",)),
        name="in_proj",
    )(x, w_main, w_gate)


def _projection_weights(w_in):
    q_a = w_in[:, 0:512]
    k_a = w_in[:, 512:640]
    v_a = w_in[:, 640:768]
    rest = w_in[:, 768:768 + 4 * MLSTM_WIDTH]
    gates = w_in[:, 768 + 4 * MLSTM_WIDTH:]
    zero = jnp.zeros((D_MODEL, HEAD_DIM), w_in.dtype)

    def lo_hi(w):
        cols = []
        for kv in range(N_KV_HEADS):
            wk = w[:, kv * HEAD_DIM:(kv + 1) * HEAD_DIM]
            cols += [wk, zero, zero, wk]
        return jnp.concatenate(cols, axis=1)

    w_main = jnp.concatenate([q_a, lo_hi(k_a), lo_hi(v_a), rest], axis=1).astype(BF16)
    w_gate = jnp.pad(gates, ((0, 0), (0, LANES - gates.shape[1]))).astype(F32)
    return w_main, w_gate


def _softmax_pieces(pieces, sink):
    mx = sink
    for s in pieces:
        mx = jnp.maximum(mx, jnp.max(s, axis=-1, keepdims=True))
    es = [jnp.exp(s - mx) for s in pieces]
    den = jnp.exp(sink - mx)
    for e in es:
        den = den + jnp.sum(e, axis=-1, keepdims=True)
    inv = 1.0 / den
    return [e * inv for e in es]


def _attn_prompt_kernel(q_ref, kp_ref, kc_ref, vp_ref, vc_ref, bias_ref, sink_ref, o_ref):
    n = pl.program_id(0)
    scale = HEAD_DIM ** -0.5
    q = q_ref[...].astype(BF16)
    kp = kp_ref[...].astype(BF16)
    kc = kc_ref[...].astype(BF16)
    vp = vp_ref[...].astype(BF16)
    vc = vc_ref[...].astype(BF16)
    for pair in range(N_HEADS_ATT // 2):
        kv = pair // 2
        qp = q[:, pair * LANES:(pair + 1) * LANES]
        acc = jnp.zeros((WINDOW, LANES), F32)
        for x in range(2):
            h = pair * 2 + x
            col = (kv * 2 + x) * LANES
            sink = sink_ref[h][0:1, 0:1]
            bias = bias_ref[h]
            s_prev = _dot_nt(qp, kp[:, col:col + LANES]) * scale + bias[:, :WINDOW]
            s_prev = jnp.where(n > 0, s_prev, NEG_INF)
            s_cur = _dot_nt(qp, kc[:, col:col + LANES]) * scale + bias[:, WINDOW:]
            p_prev, p_cur = _softmax_pieces([s_prev, s_cur], sink)
            acc = acc + _dot(p_prev.astype(BF16), vp[:, col:col + LANES])
            acc = acc + _dot(p_cur.astype(BF16), vc[:, col:col + LANES])
        o_ref[:, pair * LANES:(pair + 1) * LANES] = acc


def _attn_prompt(q, kx, vx, bias, sinks):
    s = q.shape[0]
    nb = s // WINDOW
    cur = lambda n: (n, 0)
    prev = lambda n: (jnp.maximum(n - 1, 0), 0)
    blk = (WINDOW, 512)
    return pl.pallas_call(
        _attn_prompt_kernel,
        grid=(nb,),
        in_specs=[pl.BlockSpec(blk, cur), pl.BlockSpec(blk, prev), pl.BlockSpec(blk, cur),
                  pl.BlockSpec(blk, prev), pl.BlockSpec(blk, cur),
                  pl.BlockSpec(bias.shape, lambda n: (0, 0, 0)),
                  pl.BlockSpec(sinks.shape, lambda n: (0, 0, 0))],
        out_specs=pl.BlockSpec(blk, cur),
        out_shape=jax.ShapeDtypeStruct((s, ATT_WIDTH), F32),
        compiler_params=_cparams(("parallel",)),
        name="attn_prompt",
    )(q, kx, kx, vx, vx, bias, sinks)


def _attn_sample_kernel(q_ref, kn_ref, vn_ref, ck_ref, cv_ref, bc_ref, bn_ref, sink_ref, o_ref):
    scale = HEAD_DIM ** -0.5
    q_all = q_ref[0].astype(BF16)
    kn = kn_ref[0]
    vn = vn_ref[0]
    ck = ck_ref[0]
    cv = cv_ref[0]
    lane = lax.broadcasted_iota(jnp.int32, (WINDOW, LANES), 1)
    low = lane < HEAD_DIM
    ck_sw = pltpu.roll(ck, HEAD_DIM, 1)
    cv_sw = pltpu.roll(cv, HEAD_DIM, 1)
    pad = jnp.zeros((WINDOW - SAMPLE_PAD, LANES), F32)
    rows = 2 * SAMPLE_PAD
    for kv in range(N_KV_HEADS):
        qk = q_all[kv * rows:(kv + 1) * rows]
        acc = jnp.zeros((rows, LANES), F32)
        for x in range(2):
            col = (kv * 2 + x) * LANES
            keep = low if x == 0 else jnp.logical_not(low)
            src_k, src_v = (ck, cv) if kv == x else (ck_sw, cv_sw)
            kc = jnp.where(keep, src_k, 0.0).astype(BF16)
            vc = jnp.where(keep, src_v, 0.0).astype(BF16)
            knx = jnp.concatenate([kn[:, col:col + LANES], pad], axis=0).astype(BF16)
            vnx = jnp.concatenate([vn[:, col:col + LANES], pad], axis=0).astype(BF16)
            s_c = _dot_nt(qk, kc) * scale + bc_ref[kv, x]
            s_n = _dot_nt(qk, knx) * scale + bn_ref[kv, x]
            p_c, p_n = _softmax_pieces([s_c, s_n], sink_ref[kv, x][:, 0:1])
            acc = acc + _dot(p_c.astype(BF16), vc) + _dot(p_n.astype(BF16), vnx)
        o_ref[0, kv * rows:(kv + 1) * rows, :] = acc


def _attn_sample(q, knew, vnew, cache_k, cache_v, bias_c, bias_n, sinks):
    db = q.shape[0]
    b3 = lambda b: (b, 0, 0)
    f4 = lambda b: (0, 0, 0, 0)
    return pl.pallas_call(
        _attn_sample_kernel,
        grid=(db,),
        in_specs=[pl.BlockSpec((1,) + q.shape[1:], b3),
                  pl.BlockSpec((1,) + knew.shape[1:], b3),
                  pl.BlockSpec((1,) + vnew.shape[1:], b3),
                  pl.BlockSpec((1, WINDOW, LANES), b3),
                  pl.BlockSpec((1, WINDOW, LANES), b3),
                  pl.BlockSpec(bias_c.shape, f4),
                  pl.BlockSpec(bias_n.shape, f4),
                  pl.BlockSpec(sinks.shape, f4)],
        out_specs=pl.BlockSpec((1,) + q.shape[1:], b3),
        out_shape=jax.ShapeDtypeStruct(q.shape, F32),
        compiler_params=_cparams(("parallel",)),
        name="attn_sample",
    )(q, knew, vnew, cache_k, cache_v, bias_c, bias_n, sinks)


def _log_sigmoid(x):
    return jnp.minimum(x, 0.0) - jnp.log1p(jnp.exp(-jnp.abs(x)))


def _cumsum_rows(x):
    rows = x.shape[0]
    idx = lax.broadcasted_iota(jnp.int32, x.shape, 0)
    k = 1
    while k < rows:
        x = x + jnp.where(idx >= k, pltpu.roll(x, k, 0), 0.0)
        k *= 2
    return x


def _mlstm_kernel(q_ref, k_ref, v_ref, o_ref, g_ref, gb_ref, nw_ref, c0_ref, n0_ref, m0_ref,
                  h_ref, c_out_ref, n_out_ref, m_out_ref, c_scr, n_scr, m_scr, *, rows, valid):
    c_idx = pl.program_id(1)
    keys = M_HEAD_DIM

    @pl.when(c_idx == 0)
    def _():
        c_scr[...] = c0_ref[0]
        n_scr[...] = n0_ref[0]
        m_scr[...] = m0_ref[0]

    g = g_ref[...] + gb_ref[...]
    ridx = lax.broadcasted_iota(jnp.int32, g.shape, 0)
    ig_all = g
    lf_all = _log_sigmoid(g)
    if valid < rows:
        ig_all = jnp.where(ridx < valid, ig_all, NEG_INF)
        lf_all = jnp.where(ridx < valid, lf_all, 0.0)
    b_all = _cumsum_rows(lf_all)

    t_idx = lax.broadcasted_iota(jnp.int32, (rows, keys), 0)
    s_idx = lax.broadcasted_iota(jnp.int32, (rows, keys), 1)
    eye = t_idx == s_idx
    causal = s_idx <= t_idx

    def pad_rows(x):
        if rows == keys:
            return x
        return jnp.concatenate([x, jnp.zeros((keys - rows, x.shape[1]), x.dtype)], axis=0)

    for h in range(N_HEADS_M):
        sl = slice(h * M_HEAD_DIM, (h + 1) * M_HEAD_DIM)
        q = q_ref[:, sl]
        k = k_ref[:, sl]
        v = v_ref[:, sl]
        qb = q.astype(BF16)
        kb = pad_rows(k).astype(BF16)
        vb = pad_rows(v).astype(BF16)
        bc = b_all[:, N_HEADS_M + h:N_HEADS_M + h + 1]
        igc = ig_all[:, h:h + 1]
        brow = jnp.sum(jnp.where(eye, bc, 0.0), axis=0, keepdims=True)
        igrow = jnp.sum(jnp.where(eye, igc, 0.0), axis=0, keepdims=True)
        logd = jnp.where(causal, bc - brow + igrow, -jnp.inf)
        m_prev = m_scr[h][:, 0:1]
        inter = bc + m_prev
        mt = jnp.maximum(inter, jnp.max(logd, axis=-1, keepdims=True))
        w_inter = jnp.exp(inter - mt)
        a = jnp.exp(logd - mt) * _dot_nt(qb, kb)
        c_old = c_scr[h]
        n_old = n_scr[h]
        num = _dot(a.astype(BF16), vb) + w_inter * _dot_nt(qb, c_old.astype(BF16))
        den = jnp.sum(a, axis=-1, keepdims=True) + w_inter * jnp.sum(q * n_old, axis=-1, keepdims=True)
        hid = num / jnp.maximum(jnp.abs(den), jnp.exp(-mt))
        mu = jnp.mean(hid, axis=-1, keepdims=True)
        hc = hid - mu
        var = jnp.mean(hc * hc, axis=-1, keepdims=True)
        hn = hc * lax.rsqrt(var + LN_EPS) * nw_ref[:, sl]
        h_ref[:, sl] = hn * jax.nn.sigmoid(o_ref[:, sl])
        m_new = mt[rows - 1:rows]
        w_c = jnp.exp(inter[rows - 1:rows] - m_new)
        w_s = jnp.exp(bc[rows - 1:rows] - bc + igc - m_new)
        vs = pad_rows(v * w_s).astype(BF16)
        c_scr[h] = w_c * c_old + _dot_tn(vs, kb)
        n_scr[h] = w_c * n_old + jnp.sum(w_s * k, axis=0, keepdims=True)
        m_scr[h] = jnp.broadcast_to(m_new, (1, LANES))

    @pl.when(c_idx == pl.num_programs(1) - 1)
    def _():
        c_out_ref[0] = c_scr[...]
        n_out_ref[0] = n_scr[...]
        m_out_ref[0] = m_scr[...]


def _mlstm(q, k, v, o, gates, gate_bias_row, norm_w_row, c0, n0, m0, *, rows, valid):
    batch = c0.shape[0]
    total = q.shape[0]
    nc = total // (batch * rows)
    seq = lambda b, c: (b * nc + c, 0)
    fixed = lambda b, c: (0, 0)
    st4 = lambda b, c: (b, 0, 0, 0)
    wide = pl.BlockSpec((rows, MLSTM_WIDTH), seq)
    c_spec = pl.BlockSpec((1, N_HEADS_M, M_HEAD_DIM, M_HEAD_DIM), st4)
    v_spec = pl.BlockSpec((1, N_HEADS_M, 1, LANES), st4)
    return pl.pallas_call(
        functools.partial(_mlstm_kernel, rows=rows, valid=valid),
        grid=(batch, nc),
        in_specs=[wide, wide, wide, wide,
                  pl.BlockSpec((rows, LANES), seq),
                  pl.BlockSpec((1, LANES), fixed),
                  pl.BlockSpec((1, MLSTM_WIDTH), fixed),
                  c_spec, v_spec, v_spec],
        out_specs=[wide, c_spec, v_spec, v_spec],
        out_shape=[jax.ShapeDtypeStruct((total, MLSTM_WIDTH), F32),
                   jax.ShapeDtypeStruct(c0.shape, F32),
                   jax.ShapeDtypeStruct(n0.shape, F32),
                   jax.ShapeDtypeStruct(m0.shape, F32)],
        scratch_shapes=[pltpu.VMEM((N_HEADS_M, M_HEAD_DIM, M_HEAD_DIM), F32),
                        pltpu.VMEM((N_HEADS_M, 1, LANES), F32),
                        pltpu.VMEM((N_HEADS_M, 1, LANES), F32)],
        compiler_params=_cparams(("parallel", "arbitrary")),
        name="mlstm",
    )(q, k, v, o, gates, gate_bias_row, norm_w_row, c0, n0, m0)


TOP_ROWS = 24


def _top_values(x, count):
    tops = []
    for _ in range(count):
        m = jnp.max(x, axis=0, keepdims=True)
        tops.append(m)
        x = jnp.where(x == m, -jnp.inf, x)
    tops.append(jnp.full((TOP_ROWS - count, x.shape[1]), -jnp.inf, F32))
    return jnp.concatenate(tops, axis=0)


def _mid_kernel(att_ref, mls_ref, x_ref, woa_ref, wom_ref, g1_ref, b1_ref, wqt_ref, keys_ref,
                h_ref, s2_ref, b_ref, a_ref, c_ref, s_scr, top_scr, *, alpha):
    tm = x_ref.shape[0]
    nch = tm // LANES
    mix = _dot(att_ref[...].astype(BF16), woa_ref[...]) + _dot(mls_ref[...].astype(BF16), wom_ref[...])
    hid = _layer_norm(alpha * x_ref[...] + mix, g1_ref[...], b1_ref[...])
    h_ref[...] = hid
    q_t = _dot_nt(wqt_ref[...], hid.astype(BF16))
    for hc in range(2 * PEER_HEADS):
        s_t = _dot(keys_ref[hc], q_t[hc * PEER_HALF:(hc + 1) * PEER_HALF].astype(BF16))
        for tc in range(nch):
            s_scr[hc, tc] = s_t[:, tc * LANES:(tc + 1) * LANES]

    def top_body(it, carry):
        hc = it // (nch // 2)
        tc = (it % (nch // 2)) * 2
        for d in range(2):
            top_scr[hc, tc + d] = _top_values(s_scr[hc, tc + d], PEER_TOPK + 1)
        return carry

    lax.fori_loop(0, 2 * PEER_HEADS * (nch // 2), top_body, 0)

    def cand_body(it, carry):
        h = it // nch
        tc = it % nch
        t1 = top_scr[2 * h, tc]
        t2 = top_scr[2 * h + 1, tc]
        cands = [t1[r:r + 1] + t2[0:SUBLANES] for r in range(SUBLANES)]
        cands += [t1[0:1] + t2[SUBLANES:], t1[SUBLANES:] + t2[0:1]]
        cand = jnp.concatenate(cands, axis=0)
        cmax = t1[0:1] + t2[0:1]
        z = jnp.zeros_like(cmax)
        tau = cmax
        for _ in range(PEER_TOPK):
            tau = jnp.max(cand, axis=0, keepdims=True)
            z = z + jnp.exp(tau - cmax)
            cand = jnp.where(cand == tau, -jnp.inf, cand)
        below = jnp.max(cand, axis=0, keepdims=True)
        s1 = s_scr[2 * h, tc]
        s2 = s_scr[2 * h + 1, tc]
        cut = 0.5 * (tau + below)
        s2_ref[h, tc] = s2
        b_ref[h, tc] = jnp.exp(s2 - t2[0:1])
        a_ref[h, tc] = jnp.exp(s1 - (t1[0:1] + jnp.log(z)))
        c_ref[h, tc] = cut - s1
        return carry

    lax.fori_loop(0, PEER_HEADS * nch, cand_body, 0)


def _mid(att, mls, x, wo_a, wo_m, g1, b1, wq_t, keys, alpha):
    t = x.shape[0]
    tm = TOKEN_TILE
    nch = tm // LANES
    row = lambda i: (i, 0)
    fixed2 = lambda i: (0, 0)
    fixed3 = lambda i: (0, 0, 0)
    tile4 = lambda i: (0, i, 0, 0)
    s_shape = jax.ShapeDtypeStruct((PEER_HEADS, t // LANES, N_KEYS, LANES), F32)
    s_spec = pl.BlockSpec((PEER_HEADS, nch, N_KEYS, LANES), tile4)
    return pl.pallas_call(
        functools.partial(_mid_kernel, alpha=alpha),
        grid=(t // tm,),
        in_specs=[pl.BlockSpec((tm, ATT_WIDTH), row), pl.BlockSpec((tm, MLSTM_WIDTH), row),
                  pl.BlockSpec((tm, D_MODEL), row),
                  pl.BlockSpec(wo_a.shape, fixed2), pl.BlockSpec(wo_m.shape, fixed2),
                  pl.BlockSpec(g1.shape, fixed2), pl.BlockSpec(b1.shape, fixed2),
                  pl.BlockSpec(wq_t.shape, fixed2), pl.BlockSpec(keys.shape, fixed3)],
        out_specs=[pl.BlockSpec((tm, D_MODEL), row), s_spec, s_spec, s_spec, s_spec],
        out_shape=[jax.ShapeDtypeStruct((t, D_MODEL), F32), s_shape, s_shape, s_shape, s_shape],
        scratch_shapes=[pltpu.VMEM((2 * PEER_HEADS, nch, N_KEYS, LANES), F32),
                        pltpu.VMEM((2 * PEER_HEADS, nch, TOP_ROWS, LANES), F32)],
        compiler_params=_cparams(("parallel",)),
        name="out_proj_router",
    )(att, mls, x, wo_a, wo_m, g1, b1, wq_t, keys)


def _gelu(x):
    return 0.5 * x * (1.0 + lax.erf(x * (2.0 ** -0.5)))


ACT_TOKENS = 256
GATE_ROWS = 64


def _peer_kernel(h_ref, s2_ref, b_ref, a_ref, c_ref, u_ref, vt_ref, g2_ref, b2_ref, y_ref,
                 hb_scr, w_scr, acc_scr, *, alpha):
    j = pl.program_id(1)
    tm = h_ref.shape[0]

    @pl.when(j == 0)
    def _():
        hb_scr[...] = h_ref[...].astype(BF16)
        acc_scr[...] = jnp.zeros_like(acc_scr)

    for il in range(EXPERT_BLOCK // N_KEYS):
        for tp in range(tm // ACT_TOKENS):
            act = _dot_nt(u_ref[il * N_KEYS:(il + 1) * N_KEYS, :], hb_scr[tp * ACT_TOKENS:(tp + 1) * ACT_TOKENS, :])
            for half in range(ACT_TOKENS // LANES):
                tc = tp * (ACT_TOKENS // LANES) + half
                for jh in range(N_KEYS // GATE_ROWS):
                    jr = slice(jh * GATE_ROWS, (jh + 1) * GATE_ROWS)
                    gate = jnp.zeros((GATE_ROWS, LANES), F32)
                    for h in range(PEER_HEADS):
                        sel = s2_ref[h, tc, jr, :] >= c_ref[h, tc, il:il + 1, :]
                        gate = gate + jnp.where(sel, a_ref[h, tc, il:il + 1, :] * b_ref[h, tc, jr, :], 0.0)
                    piece = act[jr, half * LANES:(half + 1) * LANES]
                    w_scr[il * N_KEYS + jh * GATE_ROWS:il * N_KEYS + (jh + 1) * GATE_ROWS,
                          tc * LANES:(tc + 1) * LANES] = (gate * _gelu(piece)).astype(BF16)
    acc_scr[...] += _dot(vt_ref[...], w_scr[...])

    @pl.when(j == pl.num_programs(1) - 1)
    def _():
        f = acc_scr[...].T
        y_ref[...] = _layer_norm(alpha * h_ref[...] + f, g2_ref[...], b2_ref[...])


def _peer(hid, s2, b, a, c, u_b, vt_b, g2, b2, alpha):
    t = hid.shape[0]
    tm = TOKEN_TILE
    nch = tm // LANES
    nblk = u_b.shape[0] // EXPERT_BLOCK
    rows = EXPERT_BLOCK // N_KEYS
    row = lambda i, j: (i, 0)
    fixed2 = lambda i, j: (0, 0)
    full_spec = pl.BlockSpec((PEER_HEADS, nch, N_KEYS, LANES), lambda i, j: (0, i, 0, 0))
    rows_spec = pl.BlockSpec((PEER_HEADS, nch, rows, LANES), lambda i, j: (0, i, j, 0))
    return pl.pallas_call(
        functools.partial(_peer_kernel, alpha=alpha),
        grid=(t // tm, nblk),
        in_specs=[pl.BlockSpec((tm, D_MODEL), row), full_spec, full_spec, rows_spec, rows_spec,
                  pl.BlockSpec((EXPERT_BLOCK, D_MODEL), lambda i, j: (j, 0)),
                  pl.BlockSpec((D_MODEL, EXPERT_BLOCK), lambda i, j: (0, j)),
                  pl.BlockSpec(g2.shape, fixed2), pl.BlockSpec(b2.shape, fixed2)],
        out_specs=pl.BlockSpec((tm, D_MODEL), row),
        out_shape=jax.ShapeDtypeStruct((t, D_MODEL), F32),
        scratch_shapes=[pltpu.VMEM((tm, D_MODEL), BF16),
                        pltpu.VMEM((EXPERT_BLOCK, tm), BF16),
                        pltpu.VMEM((D_MODEL, tm), F32)],
        compiler_params=_cparams(("parallel", "arbitrary")),
        name="peer_experts",
    )(hid, s2, b, a, c, u_b, vt_b, g2, b2)


def _finish(x, att, mls, lw, alpha):
    hid, s2, b, a, c = _mid(att, mls, x, lw["wo_a"], lw["wo_m"], lw["g1"], lw["b1"], lw["wq_t"], lw["keys"], alpha)
    return _peer(hid, s2, b, a, c, lw["u"], lw["v_t"], lw["g2"], lw["b2"], alpha)


def _prompt_bias_index():
    q = np.arange(WINDOW)[:, None]
    kband = np.arange(2 * WINDOW)[None, :]
    dist = q + WINDOW - kband
    return _bias_index_table(dist, (dist >= 0) & (dist < WINDOW))


def _sample_bias_index(steps):
    q = np.arange(SAMPLE_PAD)[:, None]
    c = np.arange(WINDOW)[None, :]
    dist_c = WINDOW - c + q
    idx_c = _bias_index_table(dist_c, (dist_c >= 0) & (dist_c < WINDOW))
    jn = np.arange(LANES)[None, :]
    dist_n = q - jn
    idx_n = _bias_index_table(dist_n, (dist_n >= 0) & (dist_n < WINDOW) & (jn < steps))
    return np.concatenate([idx_c, idx_n], axis=1)


def _per_row_head_layout(tab):
    _, steps, nk = tab.shape
    t = tab.reshape(N_KV_HEADS, 2, 2, steps, nk)
    return t.transpose(0, 2, 1, 3, 4).reshape(N_KV_HEADS, 2, 2 * steps, nk)


def kernel(x_prompt, x_sample, cache_k_win, cache_v_win, state_C, state_n, state_m, w_in, gate_bias,
           attn_sinks, rel_bias, mlstm_norm_w, w_out, ln1_g, ln1_b, peer_wq, peer_keys, peer_u, peer_v,
           ln2_g, ln2_b):
    depth = w_in.shape[0]
    alpha = (2 * depth) ** 0.25
    bsz, seq, _ = x_prompt.shape
    db, ds, _ = x_sample.shape
    assert bsz == 1 and seq % TOKEN_TILE == 0 and ds <= SAMPLE_PAD and (db * ds) % TOKEN_TILE == 0
    assert PAST_LEN >= WINDOW

    bias_p = _bias_table(_prompt_bias_index(), rel_bias)
    bias_s = _bias_table(_sample_bias_index(ds), rel_bias)
    bias_sc = _per_row_head_layout(bias_s[:, :, :WINDOW])
    bias_sn = _per_row_head_layout(bias_s[:, :, WINDOW:])

    xp = x_prompt[0]
    xs = x_sample
    pk, pv, pc, pn, pm = [], [], [], [], []
    sk, sv, sc, sn, sm = [], [], [], [], []
    for l in range(depth):
        w_main, w_gate = _projection_weights(w_in[l])
        gb_row = jnp.pad(gate_bias[l], (0, LANES - 2 * N_HEADS_M))[None, :].astype(F32)
        nw_row = mlstm_norm_w[l].reshape(1, MLSTM_WIDTH).astype(F32)
        sinks = attn_sinks[l].astype(F32)
        sinks_p = jnp.broadcast_to(sinks[:, None, None], (N_HEADS_ATT, 1, LANES))
        sinks_s = _per_row_head_layout(
            jnp.broadcast_to(sinks[:, None, None], (N_HEADS_ATT, SAMPLE_PAD, LANES)))
        lw = dict(
            wo_a=w_out[l, :ATT_WIDTH].astype(BF16), wo_m=w_out[l, ATT_WIDTH:].astype(BF16),
            g1=ln1_g[l][None, :], b1=ln1_b[l][None, :],
            wq_t=peer_wq[l].T.astype(BF16),
            keys=peer_keys[l].reshape(2 * PEER_HEADS, N_KEYS, PEER_HALF).astype(BF16),
            u=peer_u[l].astype(BF16), v_t=peer_v[l].T.astype(BF16),
            g2=ln2_g[l][None, :], b2=ln2_b[l][None, :])

        q_a, k_x, v_x, q_m, k_m, v_m, o_m, gates = _project(xp, w_main, w_gate)
        att = _attn_prompt(q_a, k_x, v_x, bias_p, sinks_p)
        zeros_c = jnp.zeros((1, N_HEADS_M, M_HEAD_DIM, M_HEAD_DIM), F32)
        zeros_v = jnp.zeros((1, N_HEADS_M, 1, LANES), F32)
        mls, c_p, n_p, m_p = _mlstm(q_m, k_m, v_m, o_m, gates, gb_row, nw_row, zeros_c, zeros_v, zeros_v,
                                    rows=M_HEAD_DIM, valid=M_HEAD_DIM)
        tail_k = k_x[seq - WINDOW:].reshape(WINDOW, N_KV_HEADS, 2, LANES)[:, :, 0, :HEAD_DIM]
        tail_v = v_x[seq - WINDOW:].reshape(WINDOW, N_KV_HEADS, 2, LANES)[:, :, 0, :HEAD_DIM]
        pk.append(tail_k[None])
        pv.append(tail_v[None])
        pc.append(c_p)
        pn.append(n_p[:, :, 0, :])
        pm.append(m_p[:, :, 0, 0])
        xp = _finish(xp, att, mls, lw, alpha)

        xs_pad = jnp.pad(xs, ((0, 0), (0, SAMPLE_PAD - ds), (0, 0))).reshape(db * SAMPLE_PAD, D_MODEL)
        q_a, k_x, v_x, q_m, k_m, v_m, o_m, gates = _project(xs_pad, w_main, w_gate)
        q_rows = q_a.reshape(db, SAMPLE_PAD, 4, LANES).transpose(0, 2, 1, 3).reshape(db, 4 * SAMPLE_PAD, LANES)
        att_rows = _attn_sample(q_rows, k_x.reshape(db, SAMPLE_PAD, 512), v_x.reshape(db, SAMPLE_PAD, 512),
                                cache_k_win[l].reshape(db, WINDOW, LANES).astype(F32),
                                cache_v_win[l].reshape(db, WINDOW, LANES).astype(F32),
                                bias_sc, bias_sn, sinks_s)
        att = att_rows.reshape(db, 4, SAMPLE_PAD, LANES).transpose(0, 2, 1, 3)[:, :ds].reshape(db * ds, ATT_WIDTH)
        c0 = state_C[l].astype(F32)
        n0 = state_n[l].astype(F32)[:, :, None, :]
        m0 = jnp.broadcast_to(state_m[l].astype(F32)[:, :, None, None], (db, N_HEADS_M, 1, LANES))
        mls, c_s, n_s, m_s = _mlstm(q_m, k_m, v_m, o_m, gates, gb_row, nw_row, c0, n0, m0,
                                    rows=SAMPLE_PAD, valid=ds)
        mls = mls.reshape(db, SAMPLE_PAD, MLSTM_WIDTH)[:, :ds].reshape(db * ds, MLSTM_WIDTH)
        k_new = k_x.reshape(db, SAMPLE_PAD, N_KV_HEADS, 2, LANES)[:, :ds, :, 0, :HEAD_DIM]
        v_new = v_x.reshape(db, SAMPLE_PAD, N_KV_HEADS, 2, LANES)[:, :ds, :, 0, :HEAD_DIM]
        sk.append(jnp.concatenate([cache_k_win[l].astype(F32), k_new], axis=1)[:, -WINDOW:])
        sv.append(jnp.concatenate([cache_v_win[l].astype(F32), v_new], axis=1)[:, -WINDOW:])
        sc.append(c_s)
        sn.append(n_s[:, :, 0, :])
        sm.append(m_s[:, :, 0, 0])
        xs = _finish(xs.reshape(db * ds, D_MODEL), att, mls, lw, alpha).reshape(db, ds, D_MODEL)

    return (xp[None], xs, jnp.stack(pk), jnp.stack(pv), jnp.stack(pc), jnp.stack(pn), jnp.stack(pm),
            jnp.stack(sk), jnp.stack(sv), jnp.stack(sc), jnp.stack(sn), jnp.stack(sm))
```

```python
import functools
import math

import numpy as np
import jax
import jax.numpy as jnp
from jax import lax
from jax.experimental import pallas as pl
from jax.experimental.pallas import tpu as pltpu

F32 = jnp.float32
BF16 = jnp.bfloat16

D_MODEL = 1024
HEAD_DIM = 64
N_HEADS_ATT = 8
N_KV_HEADS = 2
WINDOW = 128
NUM_BUCKETS = 32
MAX_DISTANCE = 128
N_HEADS_M = 4
M_HEAD_DIM = 128
MLSTM_WIDTH = N_HEADS_M * M_HEAD_DIM
ATT_WIDTH = N_HEADS_ATT * HEAD_DIM
PEER_HEADS = 8
N_KEYS = 128
PEER_TOPK = 16
PEER_HALF = 128
LN_EPS = 1e-5
NEG_INF = -1e30
PAST_LEN = 16384

LANES = 128
SUBLANES = 8
VMEM_LIMIT = 56 * 1024 * 1024

TOKEN_TILE = 512
EXPERT_BLOCK = 1024
SAMPLE_PAD = 8


def _cparams(sem, flags=None):
    return pltpu.CompilerParams(dimension_semantics=sem, vmem_limit_bytes=VMEM_LIMIT, flags=flags)


def _dot(a, b):
    return jnp.dot(a, b, preferred_element_type=F32)


def _dot_nt(a, b):
    return lax.dot_general(a, b, (((1,), (1,)), ((), ())), preferred_element_type=F32)


def _dot_tn(a, b):
    return lax.dot_general(a, b, (((0,), (0,)), ((), ())), preferred_element_type=F32)


def _layer_norm(x, g, b):
    mu = jnp.mean(x, axis=-1, keepdims=True)
    xc = x - mu
    var = jnp.mean(xc * xc, axis=-1, keepdims=True)
    return xc * lax.rsqrt(var + LN_EPS) * g + b


def _t5_bucket_np(dist):
    n = np.maximum(dist, 0)
    exact = NUM_BUCKETS // 2
    nf = np.maximum(n, 1).astype(np.float64)
    val = np.log(nf / exact) / math.log(MAX_DISTANCE / exact) * (NUM_BUCKETS - exact)
    frac = np.abs(val - np.round(val))
    assert not np.any((frac < 1e-6) & (n > exact) & (n != MAX_DISTANCE)), "bucket boundary is rounding sensitive"
    large = np.minimum(exact + np.floor(val + 1e-9).astype(np.int64), NUM_BUCKETS - 1)
    return np.where(n < exact, n, large).astype(np.int32)


def _bias_index_table(dist, valid):
    return np.where(valid, _t5_bucket_np(dist), -1).astype(np.int32)


def _bias_table_kernel(idx_ref, rb_ref, out_ref):
    idx = idx_ref[...]
    for h in range(N_HEADS_ATT):
        acc = jnp.full(idx.shape, NEG_INF, F32)
        for b in range(NUM_BUCKETS):
            acc = jnp.where(idx == b, rb_ref[b, h], acc)
        out_ref[h] = acc


def _bias_table(idx_np, rel_bias):
    r, c = idx_np.shape
    return pl.pallas_call(
        _bias_table_kernel,
        out_shape=jax.ShapeDtypeStruct((N_HEADS_ATT, r, c), F32),
        in_specs=[pl.BlockSpec(memory_space=pltpu.VMEM), pl.BlockSpec(memory_space=pltpu.SMEM)],
        out_specs=pl.BlockSpec(memory_space=pltpu.VMEM),
        name="bias_table",
    )(jnp.asarray(idx_np), rel_bias.astype(F32))


def _proj_kernel(x_ref, w_ref, wg_ref, q_ref, k_ref, v_ref, qm_ref, km_ref, vm_ref, om_ref, g_ref):
    x = x_ref[...]
    xb = x.astype(BF16)
    outs = (q_ref, k_ref, v_ref, qm_ref, km_ref, vm_ref, om_ref)
    for n, o_ref in enumerate(outs):
        z = _dot(xb, w_ref[:, n * 512:(n + 1) * 512])
        if o_ref is km_ref:
            z = z * (M_HEAD_DIM ** -0.5)
        o_ref[...] = z
    g_ref[...] = jnp.dot(x, wg_ref[...], precision=lax.Precision.HIGHEST, preferred_element_type=F32)


def _project(x, w_main, w_gate):
    t = x.shape[0]
    tm = TOKEN_TILE
    wide = jax.ShapeDtypeStruct((t, 512), F32)
    row = lambda i: (i, 0)
    fixed = lambda i: (0, 0)
    return pl.pallas_call(
        _proj_kernel,
        grid=(t // tm,),
        in_specs=[pl.BlockSpec((tm, D_MODEL), row),
                  pl.BlockSpec(w_main.shape, fixed),
                  pl.BlockSpec(w_gate.shape, fixed)],
        out_specs=[pl.BlockSpec((tm, 512), row)] * 7 + [pl.BlockSpec((tm, LANES), row)],
        out_shape=[wide] * 7 + [jax.ShapeDtypeStruct((t, LANES), F32)],
        compiler_params=_cparams(("parallel",)),
        name="in_proj",
    )(x, w_main, w_gate)


def _projection_weights(w_in):
    q_a = w_in[:, 0:512]
    k_a = w_in[:, 512:640]
    v_a = w_in[:, 640:768]
    rest = w_in[:, 768:768 + 4 * MLSTM_WIDTH]
    gates = w_in[:, 768 + 4 * MLSTM_WIDTH:]
    zero = jnp.zeros((D_MODEL, HEAD_DIM), w_in.dtype)

    def lo_hi(w):
        cols = []
        for kv in range(N_KV_HEADS):
            wk = w[:, kv * HEAD_DIM:(kv + 1) * HEAD_DIM]
            cols += [wk, zero, zero, wk]
        return jnp.concatenate(cols, axis=1)

    w_main = jnp.concatenate([q_a, lo_hi(k_a), lo_hi(v_a), rest], axis=1).astype(BF16)
    w_gate = jnp.pad(gates, ((0, 0), (0, LANES - gates.shape[1]))).astype(F32)
    return w_main, w_gate


def _softmax_pieces(pieces, sink):
    mx = sink
    for s in pieces:
        mx = jnp.maximum(mx, jnp.max(s, axis=-1, keepdims=True))
    es = [jnp.exp(s - mx) for s in pieces]
    den = jnp.exp(sink - mx)
    for e in es:
        den = den + jnp.sum(e, axis=-1, keepdims=True)
    inv = 1.0 / den
    return [e * inv for e in es]


def _attn_prompt_kernel(q_ref, kp_ref, kc_ref, vp_ref, vc_ref, bias_ref, sink_ref, o_ref):
    n = pl.program_id(0)
    scale = HEAD_DIM ** -0.5
    q = q_ref[...].astype(BF16)
    kp = kp_ref[...].astype(BF16)
    kc = kc_ref[...].astype(BF16)
    vp = vp_ref[...].astype(BF16)
    vc = vc_ref[...].astype(BF16)
    for pair in range(N_HEADS_ATT // 2):
        kv = pair // 2
        qp = q[:, pair * LANES:(pair + 1) * LANES]
        acc = jnp.zeros((WINDOW, LANES), F32)
        for x in range(2):
            h = pair * 2 + x
            col = (kv * 2 + x) * LANES
            sink = sink_ref[h][0:1, 0:1]
            bias = bias_ref[h]
            s_prev = _dot_nt(qp, kp[:, col:col + LANES]) * scale + bias[:, :WINDOW]
            s_prev = jnp.where(n > 0, s_prev, NEG_INF)
            s_cur = _dot_nt(qp, kc[:, col:col + LANES]) * scale + bias[:, WINDOW:]
            p_prev, p_cur = _softmax_pieces([s_prev, s_cur], sink)
            acc = acc + _dot(p_prev.astype(BF16), vp[:, col:col + LANES])
            acc = acc + _dot(p_cur.astype(BF16), vc[:, col:col + LANES])
        o_ref[:, pair * LANES:(pair + 1) * LANES] = acc


def _attn_prompt(q, kx, vx, bias, sinks):
    s = q.shape[0]
    nb = s // WINDOW
    cur = lambda n: (n, 0)
    prev = lambda n: (jnp.maximum(n - 1, 0), 0)
    blk = (WINDOW, 512)
    return pl.pallas_call(
        _attn_prompt_kernel,
        grid=(nb,),
        in_specs=[pl.BlockSpec(blk, cur), pl.BlockSpec(blk, prev), pl.BlockSpec(blk, cur),
                  pl.BlockSpec(blk, prev), pl.BlockSpec(blk, cur),
                  pl.BlockSpec(bias.shape, lambda n: (0, 0, 0)),
                  pl.BlockSpec(sinks.shape, lambda n: (0, 0, 0))],
        out_specs=pl.BlockSpec(blk, cur),
        out_shape=jax.ShapeDtypeStruct((s, ATT_WIDTH), F32),
        compiler_params=_cparams(("parallel",)),
        name="attn_prompt",
    )(q, kx, kx, vx, vx, bias, sinks)


def _attn_sample_kernel(q_ref, kn_ref, vn_ref, ck_ref, cv_ref, bc_ref, bn_ref, sink_ref, o_ref):
    scale = HEAD_DIM ** -0.5
    q_all = q_ref[0].astype(BF16)
    kn = kn_ref[0]
    vn = vn_ref[0]
    ck = ck_ref[0]
    cv = cv_ref[0]
    lane = lax.broadcasted_iota(jnp.int32, (WINDOW, LANES), 1)
    low = lane < HEAD_DIM
    ck_sw = pltpu.roll(ck, HEAD_DIM, 1)
    cv_sw = pltpu.roll(cv, HEAD_DIM, 1)
    pad = jnp.zeros((WINDOW - SAMPLE_PAD, LANES), F32)
    rows = 2 * SAMPLE_PAD
    for kv in range(N_KV_HEADS):
        qk = q_all[kv * rows:(kv + 1) * rows]
        acc = jnp.zeros((rows, LANES), F32)
        for x in range(2):
            col = (kv * 2 + x) * LANES
            keep = low if x == 0 else jnp.logical_not(low)
            src_k, src_v = (ck, cv) if kv == x else (ck_sw, cv_sw)
            kc = jnp.where(keep, src_k, 0.0).astype(BF16)
            vc = jnp.where(keep, src_v, 0.0).astype(BF16)
            knx = jnp.concatenate([kn[:, col:col + LANES], pad], axis=0).astype(BF16)
            vnx = jnp.concatenate([vn[:, col:col + LANES], pad], axis=0).astype(BF16)
            s_c = _dot_nt(qk, kc) * scale + bc_ref[kv, x]
            s_n = _dot_nt(qk, knx) * scale + bn_ref[kv, x]
            p_c, p_n = _softmax_pieces([s_c, s_n], sink_ref[kv, x][:, 0:1])
            acc = acc + _dot(p_c.astype(BF16), vc) + _dot(p_n.astype(BF16), vnx)
        o_ref[0, kv * rows:(kv + 1) * rows, :] = acc


def _attn_sample(q, knew, vnew, cache_k, cache_v, bias_c, bias_n, sinks):
    db = q.shape[0]
    b3 = lambda b: (b, 0, 0)
    f4 = lambda b: (0, 0, 0, 0)
    return pl.pallas_call(
        _attn_sample_kernel,
        grid=(db,),
        in_specs=[pl.BlockSpec((1,) + q.shape[1:], b3),
                  pl.BlockSpec((1,) + knew.shape[1:], b3),
                  pl.BlockSpec((1,) + vnew.shape[1:], b3),
                  pl.BlockSpec((1, WINDOW, LANES), b3),
                  pl.BlockSpec((1, WINDOW, LANES), b3),
                  pl.BlockSpec(bias_c.shape, f4),
                  pl.BlockSpec(bias_n.shape, f4),
                  pl.BlockSpec(sinks.shape, f4)],
        out_specs=pl.BlockSpec((1,) + q.shape[1:], b3),
        out_shape=jax.ShapeDtypeStruct(q.shape, F32),
        compiler_params=_cparams(("parallel",)),
        name="attn_sample",
    )(q, knew, vnew, cache_k, cache_v, bias_c, bias_n, sinks)


def _log_sigmoid(x):
    return jnp.minimum(x, 0.0) - jnp.log1p(jnp.exp(-jnp.abs(x)))


def _cumsum_rows(x):
    rows = x.shape[0]
    idx = lax.broadcasted_iota(jnp.int32, x.shape, 0)
    k = 1
    while k < rows:
        x = x + jnp.where(idx >= k, pltpu.roll(x, k, 0), 0.0)
        k *= 2
    return x


def _mlstm_kernel(q_ref, k_ref, v_ref, o_ref, g_ref, gb_ref, nw_ref, c0_ref, n0_ref, m0_ref,
                  h_ref, c_out_ref, n_out_ref, m_out_ref, c_scr, n_scr, m_scr, *, rows, valid):
    c_idx = pl.program_id(1)
    keys = M_HEAD_DIM

    @pl.when(c_idx == 0)
    def _():
        c_scr[...] = c0_ref[0]
        n_scr[...] = n0_ref[0]
        m_scr[...] = m0_ref[0]

    g = g_ref[...] + gb_ref[...]
    ridx = lax.broadcasted_iota(jnp.int32, g.shape, 0)
    ig_all = g
    lf_all = _log_sigmoid(g)
    if valid < rows:
        ig_all = jnp.where(ridx < valid, ig_all, NEG_INF)
        lf_all = jnp.where(ridx < valid, lf_all, 0.0)
    b_all = _cumsum_rows(lf_all)

    t_idx = lax.broadcasted_iota(jnp.int32, (rows, keys), 0)
    s_idx = lax.broadcasted_iota(jnp.int32, (rows, keys), 1)
    eye = t_idx == s_idx
    causal = s_idx <= t_idx

    def pad_rows(x):
        if rows == keys:
            return x
        return jnp.concatenate([x, jnp.zeros((keys - rows, x.shape[1]), x.dtype)], axis=0)

    for h in range(N_HEADS_M):
        sl = slice(h * M_HEAD_DIM, (h + 1) * M_HEAD_DIM)
        q = q_ref[:, sl]
        k = k_ref[:, sl]
        v = v_ref[:, sl]
        qb = q.astype(BF16)
        kb = pad_rows(k).astype(BF16)
        vb = pad_rows(v).astype(BF16)
        bc = b_all[:, N_HEADS_M + h:N_HEADS_M + h + 1]
        igc = ig_all[:, h:h + 1]
        brow = jnp.sum(jnp.where(eye, bc, 0.0), axis=0, keepdims=True)
        igrow = jnp.sum(jnp.where(eye, igc, 0.0), axis=0, keepdims=True)
        logd = jnp.where(causal, bc - brow + igrow, -jnp.inf)
        m_prev = m_scr[h][:, 0:1]
        inter = bc + m_prev
        mt = jnp.maximum(inter, jnp.max(logd, axis=-1, keepdims=True))
        w_inter = jnp.exp(inter - mt)
        a = jnp.exp(logd - mt) * _dot_nt(qb, kb)
        c_old = c_scr[h]
        n_old = n_scr[h]
        num = _dot(a.astype(BF16), vb) + w_inter * _dot_nt(qb, c_old.astype(BF16))
        den = jnp.sum(a, axis=-1, keepdims=True) + w_inter * jnp.sum(q * n_old, axis=-1, keepdims=True)
        hid = num / jnp.maximum(jnp.abs(den), jnp.exp(-mt))
        mu = jnp.mean(hid, axis=-1, keepdims=True)
        hc = hid - mu
        var = jnp.mean(hc * hc, axis=-1, keepdims=True)
        hn = hc * lax.rsqrt(var + LN_EPS) * nw_ref[:, sl]
        h_ref[:, sl] = hn * jax.nn.sigmoid(o_ref[:, sl])
        m_new = mt[rows - 1:rows]
        w_c = jnp.exp(inter[rows - 1:rows] - m_new)
        w_s = jnp.exp(bc[rows - 1:rows] - bc + igc - m_new)
        vs = pad_rows(v * w_s).astype(BF16)
        c_scr[h] = w_c * c_old + _dot_tn(vs, kb)
        n_scr[h] = w_c * n_old + jnp.sum(w_s * k, axis=0, keepdims=True)
        m_scr[h] = jnp.broadcast_to(m_new, (1, LANES))

    @pl.when(c_idx == pl.num_programs(1) - 1)
    def _():
        c_out_ref[0] = c_scr[...]
        n_out_ref[0] = n_scr[...]
        m_out_ref[0] = m_scr[...]


def _mlstm(q, k, v, o, gates, gate_bias_row, norm_w_row, c0, n0, m0, *, rows, valid):
    batch = c0.shape[0]
    total = q.shape[0]
    nc = total // (batch * rows)
    seq = lambda b, c: (b * nc + c, 0)
    fixed = lambda b, c: (0, 0)
    st4 = lambda b, c: (b, 0, 0, 0)
    wide = pl.BlockSpec((rows, MLSTM_WIDTH), seq)
    c_spec = pl.BlockSpec((1, N_HEADS_M, M_HEAD_DIM, M_HEAD_DIM), st4)
    v_spec = pl.BlockSpec((1, N_HEADS_M, 1, LANES), st4)
    return pl.pallas_call(
        functools.partial(_mlstm_kernel, rows=rows, valid=valid),
        grid=(batch, nc),
        in_specs=[wide, wide, wide, wide,
                  pl.BlockSpec((rows, LANES), seq),
                  pl.BlockSpec((1, LANES), fixed),
                  pl.BlockSpec((1, MLSTM_WIDTH), fixed),
                  c_spec, v_spec, v_spec],
        out_specs=[wide, c_spec, v_spec, v_spec],
        out_shape=[jax.ShapeDtypeStruct((total, MLSTM_WIDTH), F32),
                   jax.ShapeDtypeStruct(c0.shape, F32),
                   jax.ShapeDtypeStruct(n0.shape, F32),
                   jax.ShapeDtypeStruct(m0.shape, F32)],
        scratch_shapes=[pltpu.VMEM((N_HEADS_M, M_HEAD_DIM, M_HEAD_DIM), F32),
                        pltpu.VMEM((N_HEADS_M, 1, LANES), F32),
                        pltpu.VMEM((N_HEADS_M, 1, LANES), F32)],
        compiler_params=_cparams(("parallel", "arbitrary")),
        name="mlstm",
    )(q, k, v, o, gates, gate_bias_row, norm_w_row, c0, n0, m0)


TOP_ROWS = 24


def _top_values(x, count):
    tops = []
    for _ in range(count):
        m = jnp.max(x, axis=0, keepdims=True)
        tops.append(m)
        x = jnp.where(x == m, -jnp.inf, x)
    tops.append(jnp.full((TOP_ROWS - count, x.shape[1]), -jnp.inf, F32))
    return jnp.concatenate(tops, axis=0)


def _mid_kernel(att_ref, mls_ref, x_ref, woa_ref, wom_ref, g1_ref, b1_ref, wqt_ref, keys_ref,
                h_ref, s2_ref, b_ref, a_ref, c_ref, s_scr, top_scr, *, alpha):
    tm = x_ref.shape[0]
    nch = tm // LANES
    mix = _dot(att_ref[...].astype(BF16), woa_ref[...]) + _dot(mls_ref[...].astype(BF16), wom_ref[...])
    hid = _layer_norm(alpha * x_ref[...] + mix, g1_ref[...], b1_ref[...])
    h_ref[...] = hid
    q_t = _dot_nt(wqt_ref[...], hid.astype(BF16))
    for hc in range(2 * PEER_HEADS):
        s_t = _dot(keys_ref[hc], q_t[hc * PEER_HALF:(hc + 1) * PEER_HALF].astype(BF16))
        for tc in range(nch):
            s_scr[hc, tc] = s_t[:, tc * LANES:(tc + 1) * LANES]

    def top_body(it, carry):
        hc = it // (nch // 2)
        tc = (it % (nch // 2)) * 2
        for d in range(2):
            top_scr[hc, tc + d] = _top_values(s_scr[hc, tc + d], PEER_TOPK + 1)
        return carry

    lax.fori_loop(0, 2 * PEER_HEADS * (nch // 2), top_body, 0)

    def cand_body(it, carry):
        h = it // nch
        tc = it % nch
        t1 = top_scr[2 * h, tc]
        t2 = top_scr[2 * h + 1, tc]
        cands = [t1[r:r + 1] + t2[0:SUBLANES] for r in range(SUBLANES)]
        cands += [t1[0:1] + t2[SUBLANES:], t1[SUBLANES:] + t2[0:1]]
        cand = jnp.concatenate(cands, axis=0)
        cmax = t1[0:1] + t2[0:1]
        z = jnp.zeros_like(cmax)
        tau = cmax
        for _ in range(PEER_TOPK):
            tau = jnp.max(cand, axis=0, keepdims=True)
            z = z + jnp.exp(tau - cmax)
            cand = jnp.where(cand == tau, -jnp.inf, cand)
        below = jnp.max(cand, axis=0, keepdims=True)
        s1 = s_scr[2 * h, tc]
        s2 = s_scr[2 * h + 1, tc]
        cut = 0.5 * (tau + below)
        s2_ref[h, tc] = s2
        b_ref[h, tc] = jnp.exp(s2 - t2[0:1])
        a_ref[h, tc] = jnp.exp(s1 - (t1[0:1] + jnp.log(z)))
        c_ref[h, tc] = cut - s1
        return carry

    lax.fori_loop(0, PEER_HEADS * nch, cand_body, 0)


def _mid(att, mls, x, wo_a, wo_m, g1, b1, wq_t, keys, alpha):
    t = x.shape[0]
    tm = TOKEN_TILE
    nch = tm // LANES
    row = lambda i: (i, 0)
    fixed2 = lambda i: (0, 0)
    fixed3 = lambda i: (0, 0, 0)
    tile4 = lambda i: (0, i, 0, 0)
    s_shape = jax.ShapeDtypeStruct((PEER_HEADS, t // LANES, N_KEYS, LANES), F32)
    s_spec = pl.BlockSpec((PEER_HEADS, nch, N_KEYS, LANES), tile4)
    return pl.pallas_call(
        functools.partial(_mid_kernel, alpha=alpha),
        grid=(t // tm,),
        in_specs=[pl.BlockSpec((tm, ATT_WIDTH), row), pl.BlockSpec((tm, MLSTM_WIDTH), row),
                  pl.BlockSpec((tm, D_MODEL), row),
                  pl.BlockSpec(wo_a.shape, fixed2), pl.BlockSpec(wo_m.shape, fixed2),
                  pl.BlockSpec(g1.shape, fixed2), pl.BlockSpec(b1.shape, fixed2),
                  pl.BlockSpec(wq_t.shape, fixed2), pl.BlockSpec(keys.shape, fixed3)],
        out_specs=[pl.BlockSpec((tm, D_MODEL), row), s_spec, s_spec, s_spec, s_spec],
        out_shape=[jax.ShapeDtypeStruct((t, D_MODEL), F32), s_shape, s_shape, s_shape, s_shape],
        scratch_shapes=[pltpu.VMEM((2 * PEER_HEADS, nch, N_KEYS, LANES), F32),
                        pltpu.VMEM((2 * PEER_HEADS, nch, TOP_ROWS, LANES), F32)],
        compiler_params=_cparams(("parallel",)),
        name="out_proj_router",
    )(att, mls, x, wo_a, wo_m, g1, b1, wq_t, keys)


def _gelu(x):
    return 0.5 * x * (1.0 + lax.erf(x * (2.0 ** -0.5)))


ACT_TOKENS = 256
GATE_ROWS = 64


def _zero_after(x):
    bits = lax.bitcast_convert_type(x, jnp.uint32)
    bits = lax.shift_right_logical(lax.shift_right_logical(bits, jnp.uint32(16)), jnp.uint32(16))
    return lax.bitcast_convert_type(bits, F32)


def _peer_block(u_ref, vt_ref, a_ref, c_ref, s2_ref, b_ref, ht_scr, w_new, w_old, acc_scr):
    tm = ht_scr.shape[1]
    n_tp = tm // ACT_TOKENS
    tiles = [(il, tp) for il in range(EXPERT_BLOCK // N_KEYS) for tp in range(n_tp)]
    slice_at = {(sl + 1) * len(tiles) // (n_tp + 1): sl for sl in range(n_tp)}
    for k, (il, tp) in enumerate(tiles):
        zero = None
        if k in slice_at:
            cols = slice(slice_at[k] * ACT_TOKENS, (slice_at[k] + 1) * ACT_TOKENS)
            upd = acc_scr[:, cols] + _dot(vt_ref[...], w_old[:, cols])
            acc_scr[:, cols] = upd
            zero = _zero_after(upd[0:1, 0:LANES])
        act = _dot(u_ref[il * N_KEYS:(il + 1) * N_KEYS, :], ht_scr[:, tp * ACT_TOKENS:(tp + 1) * ACT_TOKENS])
        for half in range(ACT_TOKENS // LANES):
            tc = tp * (ACT_TOKENS // LANES) + half
            for jh in range(N_KEYS // GATE_ROWS):
                jr = slice(jh * GATE_ROWS, (jh + 1) * GATE_ROWS)
                gate = jnp.zeros((GATE_ROWS, LANES), F32)
                if zero is not None:
                    gate = gate + zero
                for h in range(PEER_HEADS):
                    sel = s2_ref[h, tc, jr, :] >= c_ref[h, tc, il:il + 1, :]
                    gate = gate + jnp.where(sel, a_ref[h, tc, il:il + 1, :] * b_ref[h, tc, jr, :], 0.0)
                piece = act[jr, half * LANES:(half + 1) * LANES]
                w_new[il * N_KEYS + jh * GATE_ROWS:il * N_KEYS + (jh + 1) * GATE_ROWS,
                      tc * LANES:(tc + 1) * LANES] = (gate * _gelu(piece)).astype(BF16)


def _peer_kernel(h_ref, s2_ref, b_ref, a0_ref, c0_ref, a1_ref, c1_ref, u0_ref, u1_ref,
                 vtp_ref, vt0_ref, vtl_ref, g2_ref, b2_ref, y_ref, ht_scr, w0, w1, acc_scr, *, alpha):
    s = pl.program_id(1)

    @pl.when(s == 0)
    def _():
        ht_scr[...] = h_ref[...].T.astype(BF16)
        acc_scr[...] = jnp.zeros_like(acc_scr)
        w1[...] = jnp.zeros_like(w1)

    _peer_block(u0_ref, vtp_ref, a0_ref, c0_ref, s2_ref, b_ref, ht_scr, w0, w1, acc_scr)
    _peer_block(u1_ref, vt0_ref, a1_ref, c1_ref, s2_ref, b_ref, ht_scr, w1, w0, acc_scr)

    @pl.when(s == pl.num_programs(1) - 1)
    def _():
        f = (acc_scr[...] + _dot(vtl_ref[...], w1[...])).T
        y_ref[...] = _layer_norm(alpha * h_ref[...] + f, g2_ref[...], b2_ref[...])


def _peer(hid, s2, b, a, c, u_b, vt_b, g2, b2, alpha):
    t = hid.shape[0]
    tm = TOKEN_TILE
    nch = tm // LANES
    nblk = u_b.shape[0] // EXPERT_BLOCK
    rows = EXPERT_BLOCK // N_KEYS
    row = lambda i, s: (i, 0)
    fixed2 = lambda i, s: (0, 0)
    full_spec = pl.BlockSpec((PEER_HEADS, nch, N_KEYS, LANES), lambda i, s: (0, i, 0, 0))
    rows_spec = lambda f: pl.BlockSpec((PEER_HEADS, nch, rows, LANES), lambda i, s: (0, i, f(s), 0))
    u_spec = lambda f: pl.BlockSpec((EXPERT_BLOCK, D_MODEL), lambda i, s: (f(s), 0))
    vt_spec = lambda f: pl.BlockSpec((D_MODEL, EXPERT_BLOCK), lambda i, s: (0, f(s)))
    even = lambda s: 2 * s
    odd = lambda s: 2 * s + 1
    return pl.pallas_call(
        functools.partial(_peer_kernel, alpha=alpha),
        grid=(t // tm, nblk // 2),
        in_specs=[pl.BlockSpec((tm, D_MODEL), row), full_spec, full_spec,
                  rows_spec(even), rows_spec(even), rows_spec(odd), rows_spec(odd),
                  u_spec(even), u_spec(odd),
                  vt_spec(lambda s: jnp.maximum(2 * s - 1, 0)), vt_spec(even), vt_spec(lambda s: nblk - 1),
                  pl.BlockSpec(g2.shape, fixed2), pl.BlockSpec(b2.shape, fixed2)],
        out_specs=pl.BlockSpec((tm, D_MODEL), row),
        out_shape=jax.ShapeDtypeStruct((t, D_MODEL), F32),
        scratch_shapes=[pltpu.VMEM((D_MODEL, tm), BF16),
                        pltpu.VMEM((EXPERT_BLOCK, tm), BF16),
                        pltpu.VMEM((EXPERT_BLOCK, tm), BF16),
                        pltpu.VMEM((D_MODEL, tm), F32)],
        compiler_params=_cparams(("parallel", "arbitrary")),
        name="peer_experts",
    )(hid, s2, b, a, c, a, c, u_b, u_b, vt_b, vt_b, vt_b, g2, b2)


def _finish(x, att, mls, lw, alpha):
    hid, s2, b, a, c = _mid(att, mls, x, lw["wo_a"], lw["wo_m"], lw["g1"], lw["b1"], lw["wq_t"], lw["keys"], alpha)
    return _peer(hid, s2, b, a, c, lw["u"], lw["v_t"], lw["g2"], lw["b2"], alpha)


def _prompt_bias_index():
    q = np.arange(WINDOW)[:, None]
    kband = np.arange(2 * WINDOW)[None, :]
    dist = q + WINDOW - kband
    return _bias_index_table(dist, (dist >= 0) & (dist < WINDOW))


def _sample_bias_index(steps):
    q = np.arange(SAMPLE_PAD)[:, None]
    c = np.arange(WINDOW)[None, :]
    dist_c = WINDOW - c + q
    idx_c = _bias_index_table(dist_c, (dist_c >= 0) & (dist_c < WINDOW))
    jn = np.arange(LANES)[None, :]
    dist_n = q - jn
    idx_n = _bias_index_table(dist_n, (dist_n >= 0) & (dist_n < WINDOW) & (jn < steps))
    return np.concatenate([idx_c, idx_n], axis=1)


def _per_row_head_layout(tab):
    _, steps, nk = tab.shape
    t = tab.reshape(N_KV_HEADS, 2, 2, steps, nk)
    return t.transpose(0, 2, 1, 3, 4).reshape(N_KV_HEADS, 2, 2 * steps, nk)


def kernel(x_prompt, x_sample, cache_k_win, cache_v_win, state_C, state_n, state_m, w_in, gate_bias,
           attn_sinks, rel_bias, mlstm_norm_w, w_out, ln1_g, ln1_b, peer_wq, peer_keys, peer_u, peer_v,
           ln2_g, ln2_b):
    depth = w_in.shape[0]
    alpha = (2 * depth) ** 0.25
    bsz, seq, _ = x_prompt.shape
    db, ds, _ = x_sample.shape
    assert bsz == 1 and seq % TOKEN_TILE == 0 and ds <= SAMPLE_PAD and (db * ds) % TOKEN_TILE == 0
    assert PAST_LEN >= WINDOW

    bias_p = _bias_table(_prompt_bias_index(), rel_bias)
    bias_s = _bias_table(_sample_bias_index(ds), rel_bias)
    bias_sc = _per_row_head_layout(bias_s[:, :, :WINDOW])
    bias_sn = _per_row_head_layout(bias_s[:, :, WINDOW:])

    xp = x_prompt[0]
    xs = x_sample
    pk, pv, pc, pn, pm = [], [], [], [], []
    sk, sv, sc, sn, sm = [], [], [], [], []
    for l in range(depth):
        w_main, w_gate = _projection_weights(w_in[l])
        gb_row = jnp.pad(gate_bias[l], (0, LANES - 2 * N_HEADS_M))[None, :].astype(F32)
        nw_row = mlstm_norm_w[l].reshape(1, MLSTM_WIDTH).astype(F32)
        sinks = attn_sinks[l].astype(F32)
        sinks_p = jnp.broadcast_to(sinks[:, None, None], (N_HEADS_ATT, 1, LANES))
        sinks_s = _per_row_head_layout(
            jnp.broadcast_to(sinks[:, None, None], (N_HEADS_ATT, SAMPLE_PAD, LANES)))
        lw = dict(
            wo_a=w_out[l, :ATT_WIDTH].astype(BF16), wo_m=w_out[l, ATT_WIDTH:].astype(BF16),
            g1=ln1_g[l][None, :], b1=ln1_b[l][None, :],
            wq_t=peer_wq[l].T.astype(BF16),
            keys=peer_keys[l].reshape(2 * PEER_HEADS, N_KEYS, PEER_HALF).astype(BF16),
            u=peer_u[l].astype(BF16), v_t=peer_v[l].T.astype(BF16),
            g2=ln2_g[l][None, :], b2=ln2_b[l][None, :])

        q_a, k_x, v_x, q_m, k_m, v_m, o_m, gates = _project(xp, w_main, w_gate)
        att = _attn_prompt(q_a, k_x, v_x, bias_p, sinks_p)
        zeros_c = jnp.zeros((1, N_HEADS_M, M_HEAD_DIM, M_HEAD_DIM), F32)
        zeros_v = jnp.zeros((1, N_HEADS_M, 1, LANES), F32)
        mls, c_p, n_p, m_p = _mlstm(q_m, k_m, v_m, o_m, gates, gb_row, nw_row, zeros_c, zeros_v, zeros_v,
                                    rows=M_HEAD_DIM, valid=M_HEAD_DIM)
        tail_k = k_x[seq - WINDOW:].reshape(WINDOW, N_KV_HEADS, 2, LANES)[:, :, 0, :HEAD_DIM]
        tail_v = v_x[seq - WINDOW:].reshape(WINDOW, N_KV_HEADS, 2, LANES)[:, :, 0, :HEAD_DIM]
        pk.append(tail_k[None])
        pv.append(tail_v[None])
        pc.append(c_p)
        pn.append(n_p[:, :, 0, :])
        pm.append(m_p[:, :, 0, 0])
        xp = _finish(xp, att, mls, lw, alpha)

        xs_pad = jnp.pad(xs, ((0, 0), (0, SAMPLE_PAD - ds), (0, 0))).reshape(db * SAMPLE_PAD, D_MODEL)
        q_a, k_x, v_x, q_m, k_m, v_m, o_m, gates = _project(xs_pad, w_main, w_gate)
        q_rows = q_a.reshape(db, SAMPLE_PAD, 4, LANES).transpose(0, 2, 1, 3).reshape(db, 4 * SAMPLE_PAD, LANES)
        att_rows = _attn_sample(q_rows, k_x.reshape(db, SAMPLE_PAD, 512), v_x.reshape(db, SAMPLE_PAD, 512),
                                cache_k_win[l].reshape(db, WINDOW, LANES).astype(F32),
                                cache_v_win[l].reshape(db, WINDOW, LANES).astype(F32),
                                bias_sc, bias_sn, sinks_s)
        att = att_rows.reshape(db, 4, SAMPLE_PAD, LANES).transpose(0, 2, 1, 3)[:, :ds].reshape(db * ds, ATT_WIDTH)
        c0 = state_C[l].astype(F32)
        n0 = state_n[l].astype(F32)[:, :, None, :]
        m0 = jnp.broadcast_to(state_m[l].astype(F32)[:, :, None, None], (db, N_HEADS_M, 1, LANES))
        mls, c_s, n_s, m_s = _mlstm(q_m, k_m, v_m, o_m, gates, gb_row, nw_row, c0, n0, m0,
                                    rows=SAMPLE_PAD, valid=ds)
        mls = mls.reshape(db, SAMPLE_PAD, MLSTM_WIDTH)[:, :ds].reshape(db * ds, MLSTM_WIDTH)
        k_new = k_x.reshape(db, SAMPLE_PAD, N_KV_HEADS, 2, LANES)[:, :ds, :, 0, :HEAD_DIM]
        v_new = v_x.reshape(db, SAMPLE_PAD, N_KV_HEADS, 2, LANES)[:, :ds, :, 0, :HEAD_DIM]
        sk.append(jnp.concatenate([cache_k_win[l].astype(F32), k_new], axis=1)[:, -WINDOW:])
        sv.append(jnp.concatenate([cache_v_win[l].astype(F32), v_new], axis=1)[:, -WINDOW:])
        sc.append(c_s)
        sn.append(n_s[:, :, 0, :])
        sm.append(m_s[:, :, 0, 0])
        xs = _finish(xs.reshape(db * ds, D_MODEL), att, mls, lw, alpha).reshape(db, ds, D_MODEL)

    return (xp[None], xs, jnp.stack(pk), jnp.stack(pv), jnp.stack(pc), jnp.stack(pn), jnp.stack(pm),
            jnp.stack(sk), jnp.stack(sv), jnp.stack(sc), jnp.stack(sn), jnp.stack(sm))
```

```python
import functools
import math

import numpy as np
import jax
import jax.numpy as jnp
from jax import lax
from jax.experimental import pallas as pl
from jax.experimental.pallas import tpu as pltpu

F32 = jnp.float32
BF16 = jnp.bfloat16

D_MODEL = 1024
HEAD_DIM = 64
N_HEADS_ATT = 8
N_KV_HEADS = 2
WINDOW = 128
NUM_BUCKETS = 32
MAX_DISTANCE = 128
N_HEADS_M = 4
M_HEAD_DIM = 128
MLSTM_WIDTH = N_HEADS_M * M_HEAD_DIM
ATT_WIDTH = N_HEADS_ATT * HEAD_DIM
PEER_HEADS = 8
N_KEYS = 128
PEER_TOPK = 16
PEER_HALF = 128
LN_EPS = 1e-5
NEG_INF = -1e30
PAST_LEN = 16384

LANES = 128
SUBLANES = 8
VMEM_LIMIT = 56 * 1024 * 1024

TOKEN_TILE = 512
EXPERT_BLOCK = 1024
SAMPLE_PAD = 8


def _cparams(sem, flags=None):
    return pltpu.CompilerParams(dimension_semantics=sem, vmem_limit_bytes=VMEM_LIMIT, flags=flags)


def _dot(a, b):
    return jnp.dot(a, b, preferred_element_type=F32)


def _dot_nt(a, b):
    return lax.dot_general(a, b, (((1,), (1,)), ((), ())), preferred_element_type=F32)


def _dot_tn(a, b):
    return lax.dot_general(a, b, (((0,), (0,)), ((), ())), preferred_element_type=F32)


def _layer_norm(x, g, b):
    mu = jnp.mean(x, axis=-1, keepdims=True)
    xc = x - mu
    var = jnp.mean(xc * xc, axis=-1, keepdims=True)
    return xc * lax.rsqrt(var + LN_EPS) * g + b


def _t5_bucket_np(dist):
    n = np.maximum(dist, 0)
    exact = NUM_BUCKETS // 2
    nf = np.maximum(n, 1).astype(np.float64)
    val = np.log(nf / exact) / math.log(MAX_DISTANCE / exact) * (NUM_BUCKETS - exact)
    frac = np.abs(val - np.round(val))
    assert not np.any((frac < 1e-6) & (n > exact) & (n != MAX_DISTANCE)), "bucket boundary is rounding sensitive"
    large = np.minimum(exact + np.floor(val + 1e-9).astype(np.int64), NUM_BUCKETS - 1)
    return np.where(n < exact, n, large).astype(np.int32)


def _bias_index_table(dist, valid):
    return np.where(valid, _t5_bucket_np(dist), -1).astype(np.int32)


def _bias_table_kernel(idx_ref, rb_ref, out_ref):
    idx = idx_ref[...]
    for h in range(N_HEADS_ATT):
        acc = jnp.full(idx.shape, NEG_INF, F32)
        for b in range(NUM_BUCKETS):
            acc = jnp.where(idx == b, rb_ref[b, h], acc)
        out_ref[h] = acc


def _bias_table(idx_np, rel_bias):
    r, c = idx_np.shape
    return pl.pallas_call(
        _bias_table_kernel,
        out_shape=jax.ShapeDtypeStruct((N_HEADS_ATT, r, c), F32),
        in_specs=[pl.BlockSpec(memory_space=pltpu.VMEM), pl.BlockSpec(memory_space=pltpu.SMEM)],
        out_specs=pl.BlockSpec(memory_space=pltpu.VMEM),
        name="bias_table",
    )(jnp.asarray(idx_np), rel_bias.astype(F32))


def _proj_kernel(x_ref, w_ref, wg_ref, q_ref, k_ref, v_ref, qm_ref, km_ref, vm_ref, om_ref, g_ref):
    x = x_ref[...]
    xb = x.astype(BF16)
    outs = (q_ref, k_ref, v_ref, qm_ref, km_ref, vm_ref, om_ref)
    for n, o_ref in enumerate(outs):
        z = _dot(xb, w_ref[:, n * 512:(n + 1) * 512])
        if o_ref is km_ref:
            z = z * (M_HEAD_DIM ** -0.5)
        o_ref[...] = z
    g_ref[...] = jnp.dot(x, wg_ref[...], precision=lax.Precision.HIGHEST, preferred_element_type=F32)


def _project(x, w_main, w_gate):
    t = x.shape[0]
    tm = TOKEN_TILE
    wide = jax.ShapeDtypeStruct((t, 512), F32)
    row = lambda i: (i, 0)
    fixed = lambda i: (0, 0)
    return pl.pallas_call(
        _proj_kernel,
        grid=(t // tm,),
        in_specs=[pl.BlockSpec((tm, D_MODEL), row),
                  pl.BlockSpec(w_main.shape, fixed),
                  pl.BlockSpec(w_gate.shape, fixed)],
        out_specs=[pl.BlockSpec((tm, 512), row)] * 7 + [pl.BlockSpec((tm, LANES), row)],
        out_shape=[wide] * 7 + [jax.ShapeDtypeStruct((t, LANES), F32)],
        compiler_params=_cparams(("parallel",)),
        name="in_proj",
    )(x, w_main, w_gate)


def _projection_weights(w_in):
    q_a = w_in[:, 0:512]
    k_a = w_in[:, 512:640]
    v_a = w_in[:, 640:768]
    rest = w_in[:, 768:768 + 4 * MLSTM_WIDTH]
    gates = w_in[:, 768 + 4 * MLSTM_WIDTH:]
    zero = jnp.zeros((D_MODEL, HEAD_DIM), w_in.dtype)

    def lo_hi(w):
        cols = []
        for kv in range(N_KV_HEADS):
            wk = w[:, kv * HEAD_DIM:(kv + 1) * HEAD_DIM]
            cols += [wk, zero, zero, wk]
        return jnp.concatenate(cols, axis=1)

    w_main = jnp.concatenate([q_a, lo_hi(k_a), lo_hi(v_a), rest], axis=1).astype(BF16)
    w_gate = jnp.pad(gates, ((0, 0), (0, LANES - gates.shape[1]))).astype(F32)
    return w_main, w_gate


def _softmax_pieces(pieces, sink):
    mx = sink
    for s in pieces:
        mx = jnp.maximum(mx, jnp.max(s, axis=-1, keepdims=True))
    es = [jnp.exp(s - mx) for s in pieces]
    den = jnp.exp(sink - mx)
    for e in es:
        den = den + jnp.sum(e, axis=-1, keepdims=True)
    inv = 1.0 / den
    return [e * inv for e in es]


def _attn_prompt_kernel(q_ref, kp_ref, kc_ref, vp_ref, vc_ref, bias_ref, sink_ref, o_ref):
    n = pl.program_id(0)
    scale = HEAD_DIM ** -0.5
    q = q_ref[...].astype(BF16)
    kp = kp_ref[...].astype(BF16)
    kc = kc_ref[...].astype(BF16)
    vp = vp_ref[...].astype(BF16)
    vc = vc_ref[...].astype(BF16)
    for pair in range(N_HEADS_ATT // 2):
        kv = pair // 2
        qp = q[:, pair * LANES:(pair + 1) * LANES]
        acc = jnp.zeros((WINDOW, LANES), F32)
        for x in range(2):
            h = pair * 2 + x
            col = (kv * 2 + x) * LANES
            sink = sink_ref[h][0:1, 0:1]
            bias = bias_ref[h]
            s_prev = _dot_nt(qp, kp[:, col:col + LANES]) * scale + bias[:, :WINDOW]
            s_prev = jnp.where(n > 0, s_prev, NEG_INF)
            s_cur = _dot_nt(qp, kc[:, col:col + LANES]) * scale + bias[:, WINDOW:]
            p_prev, p_cur = _softmax_pieces([s_prev, s_cur], sink)
            acc = acc + _dot(p_prev.astype(BF16), vp[:, col:col + LANES])
            acc = acc + _dot(p_cur.astype(BF16), vc[:, col:col + LANES])
        o_ref[:, pair * LANES:(pair + 1) * LANES] = acc


def _attn_prompt(q, kx, vx, bias, sinks):
    s = q.shape[0]
    nb = s // WINDOW
    cur = lambda n: (n, 0)
    prev = lambda n: (jnp.maximum(n - 1, 0), 0)
    blk = (WINDOW, 512)
    return pl.pallas_call(
        _attn_prompt_kernel,
        grid=(nb,),
        in_specs=[pl.BlockSpec(blk, cur), pl.BlockSpec(blk, prev), pl.BlockSpec(blk, cur),
                  pl.BlockSpec(blk, prev), pl.BlockSpec(blk, cur),
                  pl.BlockSpec(bias.shape, lambda n: (0, 0, 0)),
                  pl.BlockSpec(sinks.shape, lambda n: (0, 0, 0))],
        out_specs=pl.BlockSpec(blk, cur),
        out_shape=jax.ShapeDtypeStruct((s, ATT_WIDTH), F32),
        compiler_params=_cparams(("parallel",)),
        name="attn_prompt",
    )(q, kx, kx, vx, vx, bias, sinks)


def _attn_sample_kernel(q_ref, kn_ref, vn_ref, ck_ref, cv_ref, bc_ref, bn_ref, sink_ref, o_ref):
    scale = HEAD_DIM ** -0.5
    q_all = q_ref[0].astype(BF16)
    kn = kn_ref[0]
    vn = vn_ref[0]
    ck = ck_ref[0]
    cv = cv_ref[0]
    lane = lax.broadcasted_iota(jnp.int32, (WINDOW, LANES), 1)
    low = lane < HEAD_DIM
    ck_sw = pltpu.roll(ck, HEAD_DIM, 1)
    cv_sw = pltpu.roll(cv, HEAD_DIM, 1)
    pad = jnp.zeros((WINDOW - SAMPLE_PAD, LANES), F32)
    rows = 2 * SAMPLE_PAD
    for kv in range(N_KV_HEADS):
        qk = q_all[kv * rows:(kv + 1) * rows]
        acc = jnp.zeros((rows, LANES), F32)
        for x in range(2):
            col = (kv * 2 + x) * LANES
            keep = low if x == 0 else jnp.logical_not(low)
            src_k, src_v = (ck, cv) if kv == x else (ck_sw, cv_sw)
            kc = jnp.where(keep, src_k, 0.0).astype(BF16)
            vc = jnp.where(keep, src_v, 0.0).astype(BF16)
            knx = jnp.concatenate([kn[:, col:col + LANES], pad], axis=0).astype(BF16)
            vnx = jnp.concatenate([vn[:, col:col + LANES], pad], axis=0).astype(BF16)
            s_c = _dot_nt(qk, kc) * scale + bc_ref[kv, x]
            s_n = _dot_nt(qk, knx) * scale + bn_ref[kv, x]
            p_c, p_n = _softmax_pieces([s_c, s_n], sink_ref[kv, x][:, 0:1])
            acc = acc + _dot(p_c.astype(BF16), vc) + _dot(p_n.astype(BF16), vnx)
        o_ref[0, kv * rows:(kv + 1) * rows, :] = acc


def _attn_sample(q, knew, vnew, cache_k, cache_v, bias_c, bias_n, sinks):
    db = q.shape[0]
    b3 = lambda b: (b, 0, 0)
    f4 = lambda b: (0, 0, 0, 0)
    return pl.pallas_call(
        _attn_sample_kernel,
        grid=(db,),
        in_specs=[pl.BlockSpec((1,) + q.shape[1:], b3),
                  pl.BlockSpec((1,) + knew.shape[1:], b3),
                  pl.BlockSpec((1,) + vnew.shape[1:], b3),
                  pl.BlockSpec((1, WINDOW, LANES), b3),
                  pl.BlockSpec((1, WINDOW, LANES), b3),
                  pl.BlockSpec(bias_c.shape, f4),
                  pl.BlockSpec(bias_n.shape, f4),
                  pl.BlockSpec(sinks.shape, f4)],
        out_specs=pl.BlockSpec((1,) + q.shape[1:], b3),
        out_shape=jax.ShapeDtypeStruct(q.shape, F32),
        compiler_params=_cparams(("parallel",)),
        name="attn_sample",
    )(q, knew, vnew, cache_k, cache_v, bias_c, bias_n, sinks)


def _log_sigmoid(x):
    return jnp.minimum(x, 0.0) - jnp.log1p(jnp.exp(-jnp.abs(x)))


def _cumsum_rows(x):
    rows = x.shape[0]
    idx = lax.broadcasted_iota(jnp.int32, x.shape, 0)
    k = 1
    while k < rows:
        x = x + jnp.where(idx >= k, pltpu.roll(x, k, 0), 0.0)
        k *= 2
    return x


def _mlstm_kernel(q_ref, k_ref, v_ref, o_ref, g_ref, gb_ref, nw_ref, c0_ref, n0_ref, m0_ref,
                  h_ref, c_out_ref, n_out_ref, m_out_ref, c_scr, n_scr, m_scr, *, rows, valid):
    c_idx = pl.program_id(1)
    keys = M_HEAD_DIM

    @pl.when(c_idx == 0)
    def _():
        c_scr[...] = c0_ref[0]
        n_scr[...] = n0_ref[0]
        m_scr[...] = m0_ref[0]

    g = g_ref[...] + gb_ref[...]
    ridx = lax.broadcasted_iota(jnp.int32, g.shape, 0)
    ig_all = g
    lf_all = _log_sigmoid(g)
    if valid < rows:
        ig_all = jnp.where(ridx < valid, ig_all, NEG_INF)
        lf_all = jnp.where(ridx < valid, lf_all, 0.0)
    b_all = _cumsum_rows(lf_all)

    t_idx = lax.broadcasted_iota(jnp.int32, (rows, keys), 0)
    s_idx = lax.broadcasted_iota(jnp.int32, (rows, keys), 1)
    eye = t_idx == s_idx
    causal = s_idx <= t_idx

    def pad_rows(x):
        if rows == keys:
            return x
        return jnp.concatenate([x, jnp.zeros((keys - rows, x.shape[1]), x.dtype)], axis=0)

    for h in range(N_HEADS_M):
        sl = slice(h * M_HEAD_DIM, (h + 1) * M_HEAD_DIM)
        q = q_ref[:, sl]
        k = k_ref[:, sl]
        v = v_ref[:, sl]
        qb = q.astype(BF16)
        kb = pad_rows(k).astype(BF16)
        vb = pad_rows(v).astype(BF16)
        bc = b_all[:, N_HEADS_M + h:N_HEADS_M + h + 1]
        igc = ig_all[:, h:h + 1]
        brow = jnp.sum(jnp.where(eye, bc, 0.0), axis=0, keepdims=True)
        igrow = jnp.sum(jnp.where(eye, igc, 0.0), axis=0, keepdims=True)
        logd = jnp.where(causal, bc - brow + igrow, -jnp.inf)
        m_prev = m_scr[h][:, 0:1]
        inter = bc + m_prev
        mt = jnp.maximum(inter, jnp.max(logd, axis=-1, keepdims=True))
        w_inter = jnp.exp(inter - mt)
        a = jnp.exp(logd - mt) * _dot_nt(qb, kb)
        c_old = c_scr[h]
        n_old = n_scr[h]
        num = _dot(a.astype(BF16), vb) + w_inter * _dot_nt(qb, c_old.astype(BF16))
        den = jnp.sum(a, axis=-1, keepdims=True) + w_inter * jnp.sum(q * n_old, axis=-1, keepdims=True)
        hid = num / jnp.maximum(jnp.abs(den), jnp.exp(-mt))
        mu = jnp.mean(hid, axis=-1, keepdims=True)
        hc = hid - mu
        var = jnp.mean(hc * hc, axis=-1, keepdims=True)
        hn = hc * lax.rsqrt(var + LN_EPS) * nw_ref[:, sl]
        h_ref[:, sl] = hn * jax.nn.sigmoid(o_ref[:, sl])
        m_new = mt[rows - 1:rows]
        w_c = jnp.exp(inter[rows - 1:rows] - m_new)
        w_s = jnp.exp(bc[rows - 1:rows] - bc + igc - m_new)
        vs = pad_rows(v * w_s).astype(BF16)
        c_scr[h] = w_c * c_old + _dot_tn(vs, kb)
        n_scr[h] = w_c * n_old + jnp.sum(w_s * k, axis=0, keepdims=True)
        m_scr[h] = jnp.broadcast_to(m_new, (1, LANES))

    @pl.when(c_idx == pl.num_programs(1) - 1)
    def _():
        c_out_ref[0] = c_scr[...]
        n_out_ref[0] = n_scr[...]
        m_out_ref[0] = m_scr[...]


def _mlstm(q, k, v, o, gates, gate_bias_row, norm_w_row, c0, n0, m0, *, rows, valid):
    batch = c0.shape[0]
    total = q.shape[0]
    nc = total // (batch * rows)
    seq = lambda b, c: (b * nc + c, 0)
    fixed = lambda b, c: (0, 0)
    st4 = lambda b, c: (b, 0, 0, 0)
    wide = pl.BlockSpec((rows, MLSTM_WIDTH), seq)
    c_spec = pl.BlockSpec((1, N_HEADS_M, M_HEAD_DIM, M_HEAD_DIM), st4)
    v_spec = pl.BlockSpec((1, N_HEADS_M, 1, LANES), st4)
    return pl.pallas_call(
        functools.partial(_mlstm_kernel, rows=rows, valid=valid),
        grid=(batch, nc),
        in_specs=[wide, wide, wide, wide,
                  pl.BlockSpec((rows, LANES), seq),
                  pl.BlockSpec((1, LANES), fixed),
                  pl.BlockSpec((1, MLSTM_WIDTH), fixed),
                  c_spec, v_spec, v_spec],
        out_specs=[wide, c_spec, v_spec, v_spec],
        out_shape=[jax.ShapeDtypeStruct((total, MLSTM_WIDTH), F32),
                   jax.ShapeDtypeStruct(c0.shape, F32),
                   jax.ShapeDtypeStruct(n0.shape, F32),
                   jax.ShapeDtypeStruct(m0.shape, F32)],
        scratch_shapes=[pltpu.VMEM((N_HEADS_M, M_HEAD_DIM, M_HEAD_DIM), F32),
                        pltpu.VMEM((N_HEADS_M, 1, LANES), F32),
                        pltpu.VMEM((N_HEADS_M, 1, LANES), F32)],
        compiler_params=_cparams(("parallel", "arbitrary")),
        name="mlstm",
    )(q, k, v, o, gates, gate_bias_row, norm_w_row, c0, n0, m0)


def _sorting_network(n):
    pairs = []

    def merge(lo, m, r):
        step = r * 2
        if step < m:
            merge(lo, m, step)
            merge(lo + r, m, step)
            pairs.extend((i, i + r) for i in range(lo + r, lo + m - r, step))
        else:
            pairs.append((lo, lo + r))

    def sort(lo, m):
        if m > 1:
            sort(lo, m // 2)
            sort(lo + m // 2, m // 2)
            merge(lo, m, 1)

    sort(0, n)
    return pairs


_SORT_PAIRS = _sorting_network(N_KEYS // SUBLANES)


def _top_values(x, count):
    v = [x[k * SUBLANES:(k + 1) * SUBLANES] for k in range(x.shape[0] // SUBLANES)]
    for i, j in _SORT_PAIRS:
        v[i], v[j] = jnp.maximum(v[i], v[j]), jnp.minimum(v[i], v[j])
    tops = []
    for r in range(count):
        m = jnp.max(v[0], axis=0, keepdims=True)
        tops.append(m)
        left = count - 1 - r
        if left:
            took = v[0] == m
            nxt = v[1:] + [jnp.full_like(v[0], -jnp.inf)]
            v = [jnp.where(took, nxt[k], v[k]) for k in range(min(len(v), left))]
    return tops


def _rows(vals):
    return jnp.concatenate(vals, axis=0)


def _mid_kernel(att_ref, mls_ref, x_ref, woa_ref, wom_ref, g1_ref, b1_ref, wqt_ref, keys_ref,
                h_ref, rank_ref, b_ref, a_ref, l_ref, s_scr, *, alpha):
    tm = x_ref.shape[0]
    nch = tm // LANES
    mix = _dot(att_ref[...].astype(BF16), woa_ref[...]) + _dot(mls_ref[...].astype(BF16), wom_ref[...])
    hid = _layer_norm(alpha * x_ref[...] + mix, g1_ref[...], b1_ref[...])
    h_ref[...] = hid
    q_t = _dot_nt(wqt_ref[...], hid.astype(BF16))
    for hc in range(2 * PEER_HEADS):
        s_t = _dot(keys_ref[hc], q_t[hc * PEER_HALF:(hc + 1) * PEER_HALF].astype(BF16))
        for tc in range(nch):
            s_scr[hc, tc] = s_t[:, tc * LANES:(tc + 1) * LANES]

    def route_body(it, carry):
        h = it // nch
        tc = it % nch
        s1 = s_scr[2 * h, tc]
        s2 = s_scr[2 * h + 1, tc]
        t1 = _top_values(s1, PEER_TOPK + 1)
        t2 = _top_values(s2, PEER_TOPK + 1)
        lo2 = _rows(t2[:SUBLANES])
        fill = [jnp.full((SUBLANES - 2, LANES), -jnp.inf, F32)]
        cand = [t1[r] + lo2 for r in range(SUBLANES)]
        cand += [t1[0] + _rows(t2[SUBLANES:PEER_TOPK]), _rows(t1[SUBLANES:PEER_TOPK]) + t2[0],
                 _rows([t1[0] + t2[PEER_TOPK], t1[PEER_TOPK] + t2[0]] + fill)]
        cmax = t1[0] + t2[0]
        z = jnp.zeros_like(cmax)
        tau = cmax
        for _ in range(PEER_TOPK):
            tau = jnp.max(functools.reduce(jnp.maximum, cand), axis=0, keepdims=True)
            z = z + jnp.exp(tau - cmax)
            cand = [jnp.where(c == tau, -jnp.inf, c) for c in cand]
        below = jnp.max(functools.reduce(jnp.maximum, cand), axis=0, keepdims=True)
        cut = 0.5 * (tau + below)
        rank = jnp.zeros_like(s2)
        count = jnp.zeros_like(s1)
        for r in range(PEER_TOPK):
            rank = rank + jnp.where(t2[r] > s2, RANK_SCALE, 0.0)
            count = count + jnp.where(s1 >= cut - t2[r], RANK_SCALE, 0.0)
        rank_ref[h, tc] = rank.astype(BF16)
        l_ref[h, tc] = count
        b_ref[h, tc] = jnp.exp(s2 - t2[0]).astype(BF16)
        a_ref[h, tc] = jnp.exp(s1 - (t1[0] + jnp.log(z)))
        return carry

    lax.fori_loop(0, PEER_HEADS * nch, route_body, 0)


def _mid(att, mls, x, wo_a, wo_m, g1, b1, wq_t, keys, alpha):
    t = x.shape[0]
    tm = TOKEN_TILE
    nch = tm // LANES
    row = lambda i: (i, 0)
    fixed2 = lambda i: (0, 0)
    fixed3 = lambda i: (0, 0, 0)
    tile4 = lambda i: (0, i, 0, 0)
    s_shape = lambda dt: jax.ShapeDtypeStruct((PEER_HEADS, t // LANES, N_KEYS, LANES), dt)
    s_spec = pl.BlockSpec((PEER_HEADS, nch, N_KEYS, LANES), tile4)
    return pl.pallas_call(
        functools.partial(_mid_kernel, alpha=alpha),
        grid=(t // tm,),
        in_specs=[pl.BlockSpec((tm, ATT_WIDTH), row), pl.BlockSpec((tm, MLSTM_WIDTH), row),
                  pl.BlockSpec((tm, D_MODEL), row),
                  pl.BlockSpec(wo_a.shape, fixed2), pl.BlockSpec(wo_m.shape, fixed2),
                  pl.BlockSpec(g1.shape, fixed2), pl.BlockSpec(b1.shape, fixed2),
                  pl.BlockSpec(wq_t.shape, fixed2), pl.BlockSpec(keys.shape, fixed3)],
        out_specs=[pl.BlockSpec((tm, D_MODEL), row), s_spec, s_spec, s_spec, s_spec],
        out_shape=[jax.ShapeDtypeStruct((t, D_MODEL), F32), s_shape(BF16), s_shape(BF16), s_shape(F32), s_shape(F32)],
        scratch_shapes=[pltpu.VMEM((2 * PEER_HEADS, nch, N_KEYS, LANES), F32)],
        compiler_params=_cparams(("parallel",)),
        name="out_proj_router",
    )(att, mls, x, wo_a, wo_m, g1, b1, wq_t, keys)


def _gelu(x):
    return 0.5 * x * (1.0 + lax.erf(x * (2.0 ** -0.5)))


ACT_TOKENS = 256


RANK_SCALE = 256.0


def _row_tile(row):
    packed = jnp.broadcast_to(row, (2 * SUBLANES, row.shape[1])).astype(BF16)
    reps = N_KEYS // (2 * SUBLANES)
    return jnp.broadcast_to(packed[None], (reps,) + packed.shape).reshape(N_KEYS, row.shape[1])


def _zero_after(x):
    bits = lax.bitcast_convert_type(x, jnp.uint32)
    bits = lax.shift_right_logical(lax.shift_right_logical(bits, jnp.uint32(16)), jnp.uint32(16))
    return lax.bitcast_convert_type(bits, F32)


def _peer_block(u_ref, vt_ref, a_ref, l_ref, rank_ref, b_ref, ht_scr, w_new, w_old, acc_scr):
    tm = ht_scr.shape[1]
    n_tp = tm // ACT_TOKENS
    tiles = [(il, tp) for il in range(EXPERT_BLOCK // N_KEYS) for tp in range(n_tp)]
    slice_at = {(sl + 1) * len(tiles) // (n_tp + 1): sl for sl in range(n_tp)}
    for k, (il, tp) in enumerate(tiles):
        zero = None
        if k in slice_at:
            cols = slice(slice_at[k] * ACT_TOKENS, (slice_at[k] + 1) * ACT_TOKENS)
            upd = acc_scr[:, cols] + _dot(vt_ref[...], w_old[:, cols])
            acc_scr[:, cols] = upd
            zero = _zero_after(upd[0:1, 0:LANES])
        act = _dot(u_ref[il * N_KEYS:(il + 1) * N_KEYS, :], ht_scr[:, tp * ACT_TOKENS:(tp + 1) * ACT_TOKENS])
        for half in range(ACT_TOKENS // LANES):
            tc = tp * (ACT_TOKENS // LANES) + half
            gate = jnp.zeros((N_KEYS, LANES), BF16)
            if zero is not None:
                gate = gate + _row_tile(zero)
            for h in range(PEER_HEADS):
                room = jnp.maximum(_row_tile(l_ref[h, tc, il:il + 1, :]) - rank_ref[h, tc], 0)
                gate = gate + jnp.minimum(_row_tile(a_ref[h, tc, il:il + 1, :]) * b_ref[h, tc], room)
            piece = act[:, half * LANES:(half + 1) * LANES]
            w_new[il * N_KEYS:(il + 1) * N_KEYS, tc * LANES:(tc + 1) * LANES] = gate * _gelu(piece).astype(BF16)


def _peer_kernel(h_ref, rank_ref, b_ref, a0_ref, l0_ref, a1_ref, l1_ref, u0_ref, u1_ref,
                 vtp_ref, vt0_ref, vtl_ref, g2_ref, b2_ref, y_ref, ht_scr, w0, w1, acc_scr, *, alpha):
    s = pl.program_id(1)

    @pl.when(s == 0)
    def _():
        ht_scr[...] = h_ref[...].T.astype(BF16)
        acc_scr[...] = jnp.zeros_like(acc_scr)
        w1[...] = jnp.zeros_like(w1)

    _peer_block(u0_ref, vtp_ref, a0_ref, l0_ref, rank_ref, b_ref, ht_scr, w0, w1, acc_scr)
    _peer_block(u1_ref, vt0_ref, a1_ref, l1_ref, rank_ref, b_ref, ht_scr, w1, w0, acc_scr)

    @pl.when(s == pl.num_programs(1) - 1)
    def _():
        f = (acc_scr[...] + _dot(vtl_ref[...], w1[...])).T
        y_ref[...] = _layer_norm(alpha * h_ref[...] + f, g2_ref[...], b2_ref[...])


def _peer(hid, rank, b, a, cnt, u_b, vt_b, g2, b2, alpha):
    t = hid.shape[0]
    tm = TOKEN_TILE
    nch = tm // LANES
    nblk = u_b.shape[0] // EXPERT_BLOCK
    rows = EXPERT_BLOCK // N_KEYS
    row = lambda i, s: (i, 0)
    fixed2 = lambda i, s: (0, 0)
    full_spec = pl.BlockSpec((PEER_HEADS, nch, N_KEYS, LANES), lambda i, s: (0, i, 0, 0))
    rows_spec = lambda f: pl.BlockSpec((PEER_HEADS, nch, rows, LANES), lambda i, s: (0, i, f(s), 0))
    u_spec = lambda f: pl.BlockSpec((EXPERT_BLOCK, D_MODEL), lambda i, s: (f(s), 0))
    vt_spec = lambda f: pl.BlockSpec((D_MODEL, EXPERT_BLOCK), lambda i, s: (0, f(s)))
    even = lambda s: 2 * s
    odd = lambda s: 2 * s + 1
    return pl.pallas_call(
        functools.partial(_peer_kernel, alpha=alpha),
        grid=(t // tm, nblk // 2),
        in_specs=[pl.BlockSpec((tm, D_MODEL), row), full_spec, full_spec,
                  rows_spec(even), rows_spec(even), rows_spec(odd), rows_spec(odd),
                  u_spec(even), u_spec(odd),
                  vt_spec(lambda s: jnp.maximum(2 * s - 1, 0)), vt_spec(even), vt_spec(lambda s: nblk - 1),
                  pl.BlockSpec(g2.shape, fixed2), pl.BlockSpec(b2.shape, fixed2)],
        out_specs=pl.BlockSpec((tm, D_MODEL), row),
        out_shape=jax.ShapeDtypeStruct((t, D_MODEL), F32),
        scratch_shapes=[pltpu.VMEM((D_MODEL, tm), BF16),
                        pltpu.VMEM((EXPERT_BLOCK, tm), BF16),
                        pltpu.VMEM((EXPERT_BLOCK, tm), BF16),
                        pltpu.VMEM((D_MODEL, tm), F32)],
        compiler_params=_cparams(("parallel", "arbitrary")),
        name="peer_experts",
    )(hid, rank, b, a, cnt, a, cnt, u_b, u_b, vt_b, vt_b, vt_b, g2, b2)


def _finish(x, att, mls, lw, alpha):
    hid, rank, b, a, cnt = _mid(att, mls, x, lw["wo_a"], lw["wo_m"], lw["g1"], lw["b1"], lw["wq_t"], lw["keys"], alpha)
    return _peer(hid, rank, b, a, cnt, lw["u"], lw["v_t"], lw["g2"], lw["b2"], alpha)


def _prompt_bias_index():
    q = np.arange(WINDOW)[:, None]
    kband = np.arange(2 * WINDOW)[None, :]
    dist = q + WINDOW - kband
    return _bias_index_table(dist, (dist >= 0) & (dist < WINDOW))


def _sample_bias_index(steps):
    q = np.arange(SAMPLE_PAD)[:, None]
    c = np.arange(WINDOW)[None, :]
    dist_c = WINDOW - c + q
    idx_c = _bias_index_table(dist_c, (dist_c >= 0) & (dist_c < WINDOW))
    jn = np.arange(LANES)[None, :]
    dist_n = q - jn
    idx_n = _bias_index_table(dist_n, (dist_n >= 0) & (dist_n < WINDOW) & (jn < steps))
    return np.concatenate([idx_c, idx_n], axis=1)


def _per_row_head_layout(tab):
    _, steps, nk = tab.shape
    t = tab.reshape(N_KV_HEADS, 2, 2, steps, nk)
    return t.transpose(0, 2, 1, 3, 4).reshape(N_KV_HEADS, 2, 2 * steps, nk)


def kernel(x_prompt, x_sample, cache_k_win, cache_v_win, state_C, state_n, state_m, w_in, gate_bias,
           attn_sinks, rel_bias, mlstm_norm_w, w_out, ln1_g, ln1_b, peer_wq, peer_keys, peer_u, peer_v,
           ln2_g, ln2_b):
    depth = w_in.shape[0]
    alpha = (2 * depth) ** 0.25
    bsz, seq, _ = x_prompt.shape
    db, ds, _ = x_sample.shape
    assert bsz == 1 and seq % TOKEN_TILE == 0 and ds <= SAMPLE_PAD and (db * ds) % TOKEN_TILE == 0
    assert PAST_LEN >= WINDOW

    bias_p = _bias_table(_prompt_bias_index(), rel_bias)
    bias_s = _bias_table(_sample_bias_index(ds), rel_bias)
    bias_sc = _per_row_head_layout(bias_s[:, :, :WINDOW])
    bias_sn = _per_row_head_layout(bias_s[:, :, WINDOW:])

    xp = x_prompt[0]
    xs = x_sample
    pk, pv, pc, pn, pm = [], [], [], [], []
    sk, sv, sc, sn, sm = [], [], [], [], []
    for l in range(depth):
        w_main, w_gate = _projection_weights(w_in[l])
        gb_row = jnp.pad(gate_bias[l], (0, LANES - 2 * N_HEADS_M))[None, :].astype(F32)
        nw_row = mlstm_norm_w[l].reshape(1, MLSTM_WIDTH).astype(F32)
        sinks = attn_sinks[l].astype(F32)
        sinks_p = jnp.broadcast_to(sinks[:, None, None], (N_HEADS_ATT, 1, LANES))
        sinks_s = _per_row_head_layout(
            jnp.broadcast_to(sinks[:, None, None], (N_HEADS_ATT, SAMPLE_PAD, LANES)))
        lw = dict(
            wo_a=w_out[l, :ATT_WIDTH].astype(BF16), wo_m=w_out[l, ATT_WIDTH:].astype(BF16),
            g1=ln1_g[l][None, :], b1=ln1_b[l][None, :],
            wq_t=peer_wq[l].T.astype(BF16),
            keys=peer_keys[l].reshape(2 * PEER_HEADS, N_KEYS, PEER_HALF).astype(BF16),
            u=peer_u[l].astype(BF16), v_t=peer_v[l].T.astype(BF16),
            g2=ln2_g[l][None, :], b2=ln2_b[l][None, :])

        q_a, k_x, v_x, q_m, k_m, v_m, o_m, gates = _project(xp, w_main, w_gate)
        att = _attn_prompt(q_a, k_x, v_x, bias_p, sinks_p)
        zeros_c = jnp.zeros((1, N_HEADS_M, M_HEAD_DIM, M_HEAD_DIM), F32)
        zeros_v = jnp.zeros((1, N_HEADS_M, 1, LANES), F32)
        mls, c_p, n_p, m_p = _mlstm(q_m, k_m, v_m, o_m, gates, gb_row, nw_row, zeros_c, zeros_v, zeros_v,
                                    rows=M_HEAD_DIM, valid=M_HEAD_DIM)
        tail_k = k_x[seq - WINDOW:].reshape(WINDOW, N_KV_HEADS, 2, LANES)[:, :, 0, :HEAD_DIM]
        tail_v = v_x[seq - WINDOW:].reshape(WINDOW, N_KV_HEADS, 2, LANES)[:, :, 0, :HEAD_DIM]
        pk.append(tail_k[None])
        pv.append(tail_v[None])
        pc.append(c_p)
        pn.append(n_p[:, :, 0, :])
        pm.append(m_p[:, :, 0, 0])
        xp = _finish(xp, att, mls, lw, alpha)

        xs_pad = jnp.pad(xs, ((0, 0), (0, SAMPLE_PAD - ds), (0, 0))).reshape(db * SAMPLE_PAD, D_MODEL)
        q_a, k_x, v_x, q_m, k_m, v_m, o_m, gates = _project(xs_pad, w_main, w_gate)
        q_rows = q_a.reshape(db, SAMPLE_PAD, 4, LANES).transpose(0, 2, 1, 3).reshape(db, 4 * SAMPLE_PAD, LANES)
        att_rows = _attn_sample(q_rows, k_x.reshape(db, SAMPLE_PAD, 512), v_x.reshape(db, SAMPLE_PAD, 512),
                                cache_k_win[l].reshape(db, WINDOW, LANES).astype(F32),
                                cache_v_win[l].reshape(db, WINDOW, LANES).astype(F32),
                                bias_sc, bias_sn, sinks_s)
        att = att_rows.reshape(db, 4, SAMPLE_PAD, LANES).transpose(0, 2, 1, 3)[:, :ds].reshape(db * ds, ATT_WIDTH)
        c0 = state_C[l].astype(F32)
        n0 = state_n[l].astype(F32)[:, :, None, :]
        m0 = jnp.broadcast_to(state_m[l].astype(F32)[:, :, None, None], (db, N_HEADS_M, 1, LANES))
        mls, c_s, n_s, m_s = _mlstm(q_m, k_m, v_m, o_m, gates, gb_row, nw_row, c0, n0, m0,
                                    rows=SAMPLE_PAD, valid=ds)
        mls = mls.reshape(db, SAMPLE_PAD, MLSTM_WIDTH)[:, :ds].reshape(db * ds, MLSTM_WIDTH)
        k_new = k_x.reshape(db, SAMPLE_PAD, N_KV_HEADS, 2, LANES)[:, :ds, :, 0, :HEAD_DIM]
        v_new = v_x.reshape(db, SAMPLE_PAD, N_KV_HEADS, 2, LANES)[:, :ds, :, 0, :HEAD_DIM]
        sk.append(jnp.concatenate([cache_k_win[l].astype(F32), k_new], axis=1)[:, -WINDOW:])
        sv.append(jnp.concatenate([cache_v_win[l].astype(F32), v_new], axis=1)[:, -WINDOW:])
        sc.append(c_s)
        sn.append(n_s[:, :, 0, :])
        sm.append(m_s[:, :, 0, 0])
        xs = _finish(xs.reshape(db * ds, D_MODEL), att, mls, lw, alpha).reshape(db, ds, D_MODEL)

    return (xp[None], xs, jnp.stack(pk), jnp.stack(pv), jnp.stack(pc), jnp.stack(pn), jnp.stack(pm),
            jnp.stack(sk), jnp.stack(sv), jnp.stack(sc), jnp.stack(sn), jnp.stack(sm))
```

```python
import functools
import math

import numpy as np
import jax
import jax.numpy as jnp
from jax import lax
from jax.experimental import pallas as pl
from jax.experimental.pallas import tpu as pltpu

F32 = jnp.float32
BF16 = jnp.bfloat16

D_MODEL = 1024
HEAD_DIM = 64
N_HEADS_ATT = 8
N_KV_HEADS = 2
GQA_GROUP = N_HEADS_ATT // N_KV_HEADS
WINDOW = 128
NUM_BUCKETS = 32
MAX_DISTANCE = 128
N_HEADS_M = 4
M_HEAD_DIM = 128
MLSTM_WIDTH = N_HEADS_M * M_HEAD_DIM
ATT_WIDTH = N_HEADS_ATT * HEAD_DIM
PEER_HEADS = 8
N_KEYS = 128
PEER_TOPK = 16
PEER_HALF = 128
LN_EPS = 1e-5
NEG_INF = -1e30
PAST_LEN = 16384

LANES = 128
SUBLANES = 8
VMEM_LIMIT = 56 * 1024 * 1024

TOKEN_TILE = 512
EXPERT_BLOCK = 1024
SAMPLE_PAD = 8


def _cparams(sem, flags=None):
    return pltpu.CompilerParams(dimension_semantics=sem, vmem_limit_bytes=VMEM_LIMIT, flags=flags)


def _dot(a, b):
    return jnp.dot(a, b, preferred_element_type=F32)


def _dot_nt(a, b):
    return lax.dot_general(a, b, (((1,), (1,)), ((), ())), preferred_element_type=F32)


def _dot_tn(a, b):
    return lax.dot_general(a, b, (((0,), (0,)), ((), ())), preferred_element_type=F32)


def _layer_norm(x, g, b):
    mu = jnp.mean(x, axis=-1, keepdims=True)
    xc = x - mu
    var = jnp.mean(xc * xc, axis=-1, keepdims=True)
    return xc * lax.rsqrt(var + LN_EPS) * g + b


def _t5_bucket_np(dist):
    n = np.maximum(dist, 0)
    exact = NUM_BUCKETS // 2
    nf = np.maximum(n, 1).astype(np.float64)
    val = np.log(nf / exact) / math.log(MAX_DISTANCE / exact) * (NUM_BUCKETS - exact)
    frac = np.abs(val - np.round(val))
    assert not np.any((frac < 1e-6) & (n > exact) & (n != MAX_DISTANCE)), "bucket boundary is rounding sensitive"
    large = np.minimum(exact + np.floor(val + 1e-9).astype(np.int64), NUM_BUCKETS - 1)
    return np.where(n < exact, n, large).astype(np.int32)


def _bias_index_table(dist, valid):
    return np.where(valid, _t5_bucket_np(dist), -1).astype(np.int32)


def _bias_table_kernel(idx_ref, rb_ref, out_ref):
    idx = idx_ref[...]
    for h in range(N_HEADS_ATT):
        acc = jnp.full(idx.shape, NEG_INF, F32)
        for b in range(NUM_BUCKETS):
            acc = jnp.where(idx == b, rb_ref[b, h], acc)
        out_ref[h] = acc


def _bias_table(idx_np, rel_bias):
    r, c = idx_np.shape
    return pl.pallas_call(
        _bias_table_kernel,
        out_shape=jax.ShapeDtypeStruct((N_HEADS_ATT, r, c), F32),
        in_specs=[pl.BlockSpec(memory_space=pltpu.VMEM), pl.BlockSpec(memory_space=pltpu.SMEM)],
        out_specs=pl.BlockSpec(memory_space=pltpu.VMEM),
        name="bias_table",
    )(jnp.asarray(idx_np), rel_bias.astype(F32))


def _proj_kernel(x_ref, w_ref, wg_ref, q_ref, k_ref, v_ref, qm_ref, km_ref, vm_ref, om_ref, g_ref):
    x = x_ref[...]
    xb = x.astype(BF16)
    outs = (q_ref, k_ref, v_ref, qm_ref, km_ref, vm_ref, om_ref)
    for n, o_ref in enumerate(outs):
        z = _dot(xb, w_ref[:, n * 512:(n + 1) * 512])
        if o_ref is km_ref:
            z = z * (M_HEAD_DIM ** -0.5)
        o_ref[...] = z
    g_ref[...] = jnp.dot(x, wg_ref[...], precision=lax.Precision.HIGHEST, preferred_element_type=F32)


def _project(x, w_main, w_gate):
    t = x.shape[0]
    tm = TOKEN_TILE
    wide = jax.ShapeDtypeStruct((t, 512), F32)
    row = lambda i: (i, 0)
    fixed = lambda i: (0, 0)
    return pl.pallas_call(
        _proj_kernel,
        grid=(t // tm,),
        in_specs=[pl.BlockSpec((tm, D_MODEL), row),
                  pl.BlockSpec(w_main.shape, fixed),
                  pl.BlockSpec(w_gate.shape, fixed)],
        out_specs=[pl.BlockSpec((tm, 512), row)] * 7 + [pl.BlockSpec((tm, LANES), row)],
        out_shape=[wide] * 7 + [jax.ShapeDtypeStruct((t, LANES), F32)],
        compiler_params=_cparams(("parallel",)),
        name="in_proj",
    )(x, w_main, w_gate)


def _projection_weights(w_in):
    q_a = w_in[:, 0:512]
    k_a = w_in[:, 512:640]
    v_a = w_in[:, 640:768]
    rest = w_in[:, 768:768 + 4 * MLSTM_WIDTH]
    gates = w_in[:, 768 + 4 * MLSTM_WIDTH:]
    zero = jnp.zeros((D_MODEL, HEAD_DIM), w_in.dtype)

    def lo_hi(w):
        cols = []
        for kv in range(N_KV_HEADS):
            wk = w[:, kv * HEAD_DIM:(kv + 1) * HEAD_DIM]
            cols += [wk, zero, zero, wk]
        return jnp.concatenate(cols, axis=1)

    w_main = jnp.concatenate([q_a, lo_hi(k_a), lo_hi(v_a), rest], axis=1).astype(BF16)
    w_gate = jnp.pad(gates, ((0, 0), (0, LANES - gates.shape[1]))).astype(F32)
    return w_main, w_gate


def _softmax_pieces(pieces, sink):
    mx = sink
    for s in pieces:
        mx = jnp.maximum(mx, jnp.max(s, axis=-1, keepdims=True))
    es = [jnp.exp(s - mx) for s in pieces]
    den = jnp.exp(sink - mx)
    for e in es:
        den = den + jnp.sum(e, axis=-1, keepdims=True)
    inv = 1.0 / den
    return [e * inv for e in es]


def _attn_prompt_kernel(q_ref, kp_ref, kc_ref, vp_ref, vc_ref, bias_ref, sink_ref, o_ref):
    n = pl.program_id(0)
    scale = HEAD_DIM ** -0.5
    nh = N_HEADS_ATT
    bmm = functools.partial(jnp.einsum, preferred_element_type=F32)

    def per_head(ref, col_of):
        x = ref[...].astype(BF16)
        return jnp.stack([x[:, col_of(h) * LANES:(col_of(h) + 1) * LANES] for h in range(nh)], axis=0)

    kv_col = lambda h: (h // GQA_GROUP) * 2 + h % 2
    q = per_head(q_ref, lambda h: h // 2)
    kp = per_head(kp_ref, kv_col)
    kc = per_head(kc_ref, kv_col)
    vp = per_head(vp_ref, kv_col)
    vc = per_head(vc_ref, kv_col)
    bias = bias_ref[...]
    sink = sink_ref[...][:, :, 0:1]
    s_prev = bmm("hqd,hkd->hqk", q, kp) * scale + bias[:, :, :WINDOW]
    s_prev = jnp.where(n > 0, s_prev, NEG_INF)
    s_cur = bmm("hqd,hkd->hqk", q, kc) * scale + bias[:, :, WINDOW:]
    p_prev, p_cur = _softmax_pieces([s_prev, s_cur], sink)
    out = bmm("hqk,hkd->hqd", p_prev.astype(BF16), vp) + bmm("hqk,hkd->hqd", p_cur.astype(BF16), vc)
    for pair in range(nh // 2):
        o_ref[:, pair * LANES:(pair + 1) * LANES] = out[2 * pair] + out[2 * pair + 1]


def _attn_prompt(q, kx, vx, bias, sinks):
    s = q.shape[0]
    nb = s // WINDOW
    cur = lambda n: (n, 0)
    prev = lambda n: (jnp.maximum(n - 1, 0), 0)
    blk = (WINDOW, 512)
    return pl.pallas_call(
        _attn_prompt_kernel,
        grid=(nb,),
        in_specs=[pl.BlockSpec(blk, cur), pl.BlockSpec(blk, prev), pl.BlockSpec(blk, cur),
                  pl.BlockSpec(blk, prev), pl.BlockSpec(blk, cur),
                  pl.BlockSpec(bias.shape, lambda n: (0, 0, 0)),
                  pl.BlockSpec(sinks.shape, lambda n: (0, 0, 0))],
        out_specs=pl.BlockSpec(blk, cur),
        out_shape=jax.ShapeDtypeStruct((s, ATT_WIDTH), F32),
        compiler_params=_cparams(("parallel",)),
        name="attn_prompt",
    )(q, kx, kx, vx, vx, bias, sinks)


SAMPLE_SEQS = 8


def _attn_sample_kernel(q_ref, kn_ref, vn_ref, ck_ref, cv_ref, bc_ref, bn_ref, sink_ref, o_ref):
    scale = HEAD_DIM ** -0.5
    lane = lax.broadcasted_iota(jnp.int32, (WINDOW, LANES), 1)
    low = lane < HEAD_DIM
    pad = jnp.zeros((WINDOW - SAMPLE_PAD, LANES), F32)
    rows = 2 * SAMPLE_PAD
    ns = q_ref.shape[0]
    qk_t = lambda a, b: jnp.einsum("bqd,bkd->bqk", a, b, preferred_element_type=F32)
    pv = lambda a, b: jnp.einsum("bqk,bkd->bqd", a, b, preferred_element_type=F32)
    q_all = q_ref[...].astype(BF16)
    kn = kn_ref[...]
    vn = vn_ref[...]
    ck = ck_ref[...]
    cv = cv_ref[...]
    ck_sw = pltpu.roll(ck, HEAD_DIM, 2)
    cv_sw = pltpu.roll(cv, HEAD_DIM, 2)
    pad = jnp.zeros((ns, WINDOW - SAMPLE_PAD, LANES), F32)
    for kv in range(N_KV_HEADS):
        qk = q_all[:, kv * rows:(kv + 1) * rows]
        acc = jnp.zeros((ns, rows, LANES), F32)
        for x in range(2):
            col = (kv * 2 + x) * LANES
            keep = low if x == 0 else jnp.logical_not(low)
            src_k, src_v = (ck, cv) if kv == x else (ck_sw, cv_sw)
            kc = jnp.where(keep, src_k, 0.0).astype(BF16)
            vc = jnp.where(keep, src_v, 0.0).astype(BF16)
            knx = jnp.concatenate([kn[:, :, col:col + LANES], pad], axis=1).astype(BF16)
            vnx = jnp.concatenate([vn[:, :, col:col + LANES], pad], axis=1).astype(BF16)
            s_c = qk_t(qk, kc) * scale + bc_ref[kv, x]
            s_n = qk_t(qk, knx) * scale + bn_ref[kv, x]
            p_c, p_n = _softmax_pieces([s_c, s_n], sink_ref[kv, x][:, 0:1])
            acc = acc + pv(p_c.astype(BF16), vc) + pv(p_n.astype(BF16), vnx)
        o_ref[:, kv * rows:(kv + 1) * rows, :] = acc


def _attn_sample(q, knew, vnew, cache_k, cache_v, bias_c, bias_n, sinks):
    db = q.shape[0]
    ns = SAMPLE_SEQS
    assert db % ns == 0
    b3 = lambda b: (b, 0, 0)
    f4 = lambda b: (0, 0, 0, 0)
    return pl.pallas_call(
        _attn_sample_kernel,
        grid=(db // ns,),
        in_specs=[pl.BlockSpec((ns,) + q.shape[1:], b3),
                  pl.BlockSpec((ns,) + knew.shape[1:], b3),
                  pl.BlockSpec((ns,) + vnew.shape[1:], b3),
                  pl.BlockSpec((ns, WINDOW, LANES), b3),
                  pl.BlockSpec((ns, WINDOW, LANES), b3),
                  pl.BlockSpec(bias_c.shape, f4),
                  pl.BlockSpec(bias_n.shape, f4),
                  pl.BlockSpec(sinks.shape, f4)],
        out_specs=pl.BlockSpec((ns,) + q.shape[1:], b3),
        out_shape=jax.ShapeDtypeStruct(q.shape, F32),
        compiler_params=_cparams(("parallel",)),
        name="attn_sample",
    )(q, knew, vnew, cache_k, cache_v, bias_c, bias_n, sinks)


def _log_sigmoid(x):
    return jnp.minimum(x, 0.0) - jnp.log1p(jnp.exp(-jnp.abs(x)))


def _cumsum_rows(x):
    axis = x.ndim - 2
    rows = x.shape[axis]
    idx = lax.broadcasted_iota(jnp.int32, x.shape, axis)
    k = 1
    while k < rows:
        x = x + jnp.where(idx >= k, pltpu.roll(x, k, axis), 0.0)
        k *= 2
    return x


def _mlstm_kernel(q_ref, k_ref, v_ref, o_ref, g_ref, gb_ref, nw_ref, c0_ref, n0_ref, m0_ref,
                  h_ref, c_out_ref, n_out_ref, m_out_ref, c_scr, n_scr, m_scr, *, rows, valid):
    c_idx = pl.program_id(1)

    @pl.when(c_idx == 0)
    def _():
        c_scr[...] = c0_ref[...]
        n_scr[...] = n0_ref[...]
        m_scr[...] = m0_ref[...]

    ns = c_scr.shape[0]
    nh = N_HEADS_M
    keys = M_HEAD_DIM
    bmm = functools.partial(jnp.einsum, preferred_element_type=F32)

    def heads(ref):
        x = ref[...].reshape(ns, rows, nh * M_HEAD_DIM)
        return jnp.concatenate([x[:, :, h * M_HEAD_DIM:(h + 1) * M_HEAD_DIM] for h in range(nh)], axis=0)

    def pad_rows(x):
        if rows == keys:
            return x
        return jnp.concatenate([x, jnp.zeros((x.shape[0], keys - rows, x.shape[2]), x.dtype)], axis=1)

    g = g_ref[...].reshape(ns, rows, LANES) + gb_ref[...]
    ridx = lax.broadcasted_iota(jnp.int32, g.shape, 1)
    ig_all = g
    lf_all = _log_sigmoid(g)
    if valid < rows:
        ig_all = jnp.where(ridx < valid, ig_all, NEG_INF)
        lf_all = jnp.where(ridx < valid, lf_all, 0.0)
    b_all = _cumsum_rows(lf_all)
    bc = jnp.concatenate([b_all[:, :, nh + h:nh + h + 1] for h in range(nh)], axis=0)
    igc = jnp.concatenate([ig_all[:, :, h:h + 1] for h in range(nh)], axis=0)

    t_idx = lax.broadcasted_iota(jnp.int32, (rows, keys), 0)
    s_idx = lax.broadcasted_iota(jnp.int32, (rows, keys), 1)
    eye = t_idx == s_idx
    causal = s_idx <= t_idx

    q = heads(q_ref)
    k = heads(k_ref)
    v = heads(v_ref)
    qb = q.astype(BF16)
    kb = pad_rows(k).astype(BF16)
    vb = pad_rows(v).astype(BF16)
    brow = jnp.sum(jnp.where(eye, bc, 0.0), axis=1, keepdims=True)
    igrow = jnp.sum(jnp.where(eye, igc, 0.0), axis=1, keepdims=True)
    logd = jnp.where(causal, bc - brow + igrow, -jnp.inf)
    c_old = jnp.concatenate([c_scr[:, h] for h in range(nh)], axis=0)
    n_old = jnp.concatenate([n_scr[:, h] for h in range(nh)], axis=0)
    m_prev = jnp.concatenate([m_scr[:, h] for h in range(nh)], axis=0)[:, :, 0:1]
    inter = bc + m_prev
    mt = jnp.maximum(inter, jnp.max(logd, axis=-1, keepdims=True))
    w_inter = jnp.exp(inter - mt)
    a = jnp.exp(logd - mt) * bmm("bqd,bkd->bqk", qb, kb)
    num = bmm("bqk,bkd->bqd", a.astype(BF16), vb) + w_inter * bmm("bqd,bvd->bqv", qb, c_old.astype(BF16))
    den = jnp.sum(a, axis=-1, keepdims=True) + w_inter * jnp.sum(q * n_old, axis=-1, keepdims=True)
    hid = num / jnp.maximum(jnp.abs(den), jnp.exp(-mt))
    mu = jnp.mean(hid, axis=-1, keepdims=True)
    hc = hid - mu
    var = jnp.mean(hc * hc, axis=-1, keepdims=True)
    hn = hc * lax.rsqrt(var + LN_EPS)
    for h in range(nh):
        sl = slice(h * M_HEAD_DIM, (h + 1) * M_HEAD_DIM)
        out = hn[h * ns:(h + 1) * ns].reshape(ns * rows, M_HEAD_DIM) * nw_ref[:, sl]
        h_ref[:, sl] = out * jax.nn.sigmoid(o_ref[:, sl])
    m_new = mt[:, rows - 1:rows]
    w_c = jnp.exp(inter[:, rows - 1:rows] - m_new)
    w_s = jnp.exp(bc[:, rows - 1:rows] - bc + igc - m_new)
    vs_t = jnp.swapaxes(pad_rows(v * w_s), 1, 2).astype(BF16)
    c_new = w_c * c_old + bmm("bvs,bsd->bvd", vs_t, kb)
    n_new = w_c * n_old + jnp.sum(w_s * k, axis=1, keepdims=True)
    m_row = jnp.broadcast_to(m_new, (nh * ns, 1, LANES))
    for h in range(nh):
        c_scr[:, h] = c_new[h * ns:(h + 1) * ns]
        n_scr[:, h] = n_new[h * ns:(h + 1) * ns]
        m_scr[:, h] = m_row[h * ns:(h + 1) * ns]

    @pl.when(c_idx == pl.num_programs(1) - 1)
    def _():
        c_out_ref[...] = c_scr[...]
        n_out_ref[...] = n_scr[...]
        m_out_ref[...] = m_scr[...]


def _mlstm(q, k, v, o, gates, gate_bias_row, norm_w_row, c0, n0, m0, *, rows, valid, seqs):
    batch = c0.shape[0]
    total = q.shape[0]
    nc = total // (batch * rows)
    assert batch % seqs == 0 and (seqs == 1 or nc == 1)
    seq = lambda b, c: (b * nc + c, 0)
    fixed = lambda b, c: (0, 0)
    st4 = lambda b, c: (b, 0, 0, 0)
    wide = pl.BlockSpec((seqs * rows, MLSTM_WIDTH), seq)
    c_spec = pl.BlockSpec((seqs, N_HEADS_M, M_HEAD_DIM, M_HEAD_DIM), st4)
    v_spec = pl.BlockSpec((seqs, N_HEADS_M, 1, LANES), st4)
    return pl.pallas_call(
        functools.partial(_mlstm_kernel, rows=rows, valid=valid),
        grid=(batch // seqs, nc),
        in_specs=[wide, wide, wide, wide,
                  pl.BlockSpec((seqs * rows, LANES), seq),
                  pl.BlockSpec((1, LANES), fixed),
                  pl.BlockSpec((1, MLSTM_WIDTH), fixed),
                  c_spec, v_spec, v_spec],
        out_specs=[wide, c_spec, v_spec, v_spec],
        out_shape=[jax.ShapeDtypeStruct((total, MLSTM_WIDTH), F32),
                   jax.ShapeDtypeStruct(c0.shape, F32),
                   jax.ShapeDtypeStruct(n0.shape, F32),
                   jax.ShapeDtypeStruct(m0.shape, F32)],
        scratch_shapes=[pltpu.VMEM((seqs, N_HEADS_M, M_HEAD_DIM, M_HEAD_DIM), F32),
                        pltpu.VMEM((seqs, N_HEADS_M, 1, LANES), F32),
                        pltpu.VMEM((seqs, N_HEADS_M, 1, LANES), F32)],
        compiler_params=_cparams(("parallel", "arbitrary")),
        name="mlstm",
    )(q, k, v, o, gates, gate_bias_row, norm_w_row, c0, n0, m0)


def _sorting_network(n):
    pairs = []

    def merge(lo, m, r):
        step = r * 2
        if step < m:
            merge(lo, m, step)
            merge(lo + r, m, step)
            pairs.extend((i, i + r) for i in range(lo + r, lo + m - r, step))
        else:
            pairs.append((lo, lo + r))

    def sort(lo, m):
        if m > 1:
            sort(lo, m // 2)
            sort(lo + m // 2, m // 2)
            merge(lo, m, 1)

    sort(0, n)
    return pairs


_SORT_PAIRS = _sorting_network(N_KEYS // SUBLANES)
TOP_ROWS = 24


def _top_values(x, count):
    v = [x[k * SUBLANES:(k + 1) * SUBLANES] for k in range(x.shape[0] // SUBLANES)]
    for i, j in _SORT_PAIRS:
        v[i], v[j] = jnp.maximum(v[i], v[j]), jnp.minimum(v[i], v[j])
    tops = []
    for r in range(count):
        m = jnp.max(v[0], axis=0, keepdims=True)
        tops.append(m)
        left = count - 1 - r
        if left:
            took = v[0] == m
            nxt = v[1:] + [jnp.full_like(v[0], -jnp.inf)]
            v = [jnp.where(took, nxt[k], v[k]) for k in range(min(len(v), left))]
    return tops


def _rows(vals):
    return jnp.concatenate(vals, axis=0)


def _tree_max(vals):
    while len(vals) > 1:
        vals = [jnp.maximum(vals[k], vals[k + 1]) for k in range(0, len(vals) - 1, 2)] + vals[len(vals) & ~1:]
    return vals[0]


def _mid_kernel(att_ref, mls_ref, x_ref, woa_ref, wom_ref, g1_ref, b1_ref, wqt_ref, keys_ref,
                h_ref, rank_ref, b_ref, a_ref, l_ref, s_scr, top_scr, *, alpha):
    tm = x_ref.shape[0]
    nch = tm // LANES
    mix = _dot(att_ref[...].astype(BF16), woa_ref[...]) + _dot(mls_ref[...].astype(BF16), wom_ref[...])
    hid = _layer_norm(alpha * x_ref[...] + mix, g1_ref[...], b1_ref[...])
    h_ref[...] = hid
    q_t = _dot_nt(wqt_ref[...], hid.astype(BF16))
    for hc in range(2 * PEER_HEADS):
        s_t = _dot(keys_ref[hc], q_t[hc * PEER_HALF:(hc + 1) * PEER_HALF].astype(BF16))
        for tc in range(nch):
            s_scr[hc, tc] = s_t[:, tc * LANES:(tc + 1) * LANES]

    n_items = PEER_HEADS * nch
    pad = [jnp.full((TOP_ROWS - PEER_TOPK - 1, LANES), -jnp.inf, F32)]

    def find_tops(item, slot):
        h = item // nch
        tc = item % nch
        for c in range(2):
            top_scr[slot, c] = _rows(_top_values(s_scr[2 * h + c, tc], PEER_TOPK + 1) + pad)

    find_tops(0, 0)

    def route_body(it, carry):
        find_tops(jnp.minimum(it + 1, n_items - 1), (it + 1) % 2)
        h = it // nch
        tc = it % nch
        s1 = s_scr[2 * h, tc]
        s2 = s_scr[2 * h + 1, tc]
        tops1 = top_scr[it % 2, 0]
        tops2 = top_scr[it % 2, 1]
        t1 = [tops1[r:r + 1] for r in range(PEER_TOPK + 1)]
        t2 = [tops2[r:r + 1] for r in range(PEER_TOPK + 1)]
        lo2 = tops2[:SUBLANES]
        fill = [jnp.full((SUBLANES - 2, LANES), -jnp.inf, F32)]
        cand = [t1[r] + lo2 for r in range(SUBLANES)]
        cand += [t1[0] + tops2[SUBLANES:PEER_TOPK], tops1[SUBLANES:PEER_TOPK] + t2[0],
                 _rows([t1[0] + t2[PEER_TOPK], t1[PEER_TOPK] + t2[0]] + fill)]
        cmax = t1[0] + t2[0]
        z = jnp.zeros_like(cmax)
        tau = cmax
        for _ in range(PEER_TOPK):
            tau = jnp.max(_tree_max(cand), axis=0, keepdims=True)
            z = z + jnp.exp(tau - cmax)
            cand = [jnp.where(c == tau, -jnp.inf, c) for c in cand]
        below = jnp.max(_tree_max(cand), axis=0, keepdims=True)
        cut = 0.5 * (tau + below)
        rank = jnp.zeros_like(s2)
        count = jnp.zeros_like(s1)
        for r in range(PEER_TOPK):
            rank = jnp.where(t2[r] > s2, (r + 1) * RANK_SCALE, rank)
            count = jnp.where(s1 >= cut - t2[r], (r + 1) * RANK_SCALE, count)
        rank_ref[h, tc] = rank.astype(BF16)
        l_ref[h, tc] = count
        b_ref[h, tc] = jnp.exp(s2 - t2[0]).astype(BF16)
        a_ref[h, tc] = jnp.exp(s1 - (t1[0] + jnp.log(z)))
        return carry

    lax.fori_loop(0, n_items, route_body, 0)


def _mid(att, mls, x, wo_a, wo_m, g1, b1, wq_t, keys, alpha):
    t = x.shape[0]
    tm = TOKEN_TILE
    nch = tm // LANES
    row = lambda i: (i, 0)
    fixed2 = lambda i: (0, 0)
    fixed3 = lambda i: (0, 0, 0)
    tile4 = lambda i: (0, i, 0, 0)
    s_shape = lambda dt: jax.ShapeDtypeStruct((PEER_HEADS, t // LANES, N_KEYS, LANES), dt)
    s_spec = pl.BlockSpec((PEER_HEADS, nch, N_KEYS, LANES), tile4)
    return pl.pallas_call(
        functools.partial(_mid_kernel, alpha=alpha),
        grid=(t // tm,),
        in_specs=[pl.BlockSpec((tm, ATT_WIDTH), row), pl.BlockSpec((tm, MLSTM_WIDTH), row),
                  pl.BlockSpec((tm, D_MODEL), row),
                  pl.BlockSpec(wo_a.shape, fixed2), pl.BlockSpec(wo_m.shape, fixed2),
                  pl.BlockSpec(g1.shape, fixed2), pl.BlockSpec(b1.shape, fixed2),
                  pl.BlockSpec(wq_t.shape, fixed2), pl.BlockSpec(keys.shape, fixed3)],
        out_specs=[pl.BlockSpec((tm, D_MODEL), row), s_spec, s_spec, s_spec, s_spec],
        out_shape=[jax.ShapeDtypeStruct((t, D_MODEL), F32), s_shape(BF16), s_shape(BF16), s_shape(F32), s_shape(F32)],
        scratch_shapes=[pltpu.VMEM((2 * PEER_HEADS, nch, N_KEYS, LANES), F32),
                        pltpu.VMEM((2, 2, TOP_ROWS, LANES), F32)],
        compiler_params=_cparams(("parallel",)),
        name="out_proj_router",
    )(att, mls, x, wo_a, wo_m, g1, b1, wq_t, keys)


def _gelu(x):
    return 0.5 * x * (1.0 + lax.erf(x * (2.0 ** -0.5)))


ACT_TOKENS = 256


RANK_SCALE = 256.0


def _row_tile(row):
    packed = jnp.broadcast_to(row, (2 * SUBLANES, row.shape[1])).astype(BF16)
    reps = N_KEYS // (2 * SUBLANES)
    return jnp.broadcast_to(packed[None], (reps,) + packed.shape).reshape(N_KEYS, row.shape[1])


def _zero_after(x):
    bits = lax.bitcast_convert_type(x, jnp.uint32)
    bits = lax.shift_right_logical(lax.shift_right_logical(bits, jnp.uint32(16)), jnp.uint32(16))
    return lax.bitcast_convert_type(bits, F32)


def _peer_block(u_ref, vt_ref, a_ref, l_ref, rank_ref, b_ref, ht_scr, w_new, w_old, acc_scr):
    tm = ht_scr.shape[1]
    n_tp = tm // ACT_TOKENS
    tiles = [(il, tp) for il in range(EXPERT_BLOCK // N_KEYS) for tp in range(n_tp)]
    slice_at = {(sl + 1) * len(tiles) // (n_tp + 1): sl for sl in range(n_tp)}
    for k, (il, tp) in enumerate(tiles):
        zero = None
        if k in slice_at:
            cols = slice(slice_at[k] * ACT_TOKENS, (slice_at[k] + 1) * ACT_TOKENS)
            upd = acc_scr[:, cols] + _dot(vt_ref[...], w_old[:, cols])
            acc_scr[:, cols] = upd
            zero = _zero_after(upd[0:1, 0:LANES])
        act = _dot(u_ref[il * N_KEYS:(il + 1) * N_KEYS, :], ht_scr[:, tp * ACT_TOKENS:(tp + 1) * ACT_TOKENS])
        for half in range(ACT_TOKENS // LANES):
            tc = tp * (ACT_TOKENS // LANES) + half
            gate = jnp.zeros((N_KEYS, LANES), BF16)
            if zero is not None:
                gate = gate + _row_tile(zero)
            for h in range(PEER_HEADS):
                room = jnp.maximum(_row_tile(l_ref[h, tc, il:il + 1, :]) - rank_ref[h, tc], 0)
                gate = gate + jnp.minimum(_row_tile(a_ref[h, tc, il:il + 1, :]) * b_ref[h, tc], room)
            piece = act[:, half * LANES:(half + 1) * LANES]
            w_new[il * N_KEYS:(il + 1) * N_KEYS, tc * LANES:(tc + 1) * LANES] = gate * _gelu(piece).astype(BF16)


def _peer_kernel(h_ref, rank_ref, b_ref, a0_ref, l0_ref, a1_ref, l1_ref, u0_ref, u1_ref,
                 vtp_ref, vt0_ref, vtl_ref, g2_ref, b2_ref, y_ref, ht_scr, w0, w1, acc_scr, *, alpha):
    s = pl.program_id(1)

    @pl.when(s == 0)
    def _():
        ht_scr[...] = h_ref[...].T.astype(BF16)
        acc_scr[...] = jnp.zeros_like(acc_scr)
        w1[...] = jnp.zeros_like(w1)

    _peer_block(u0_ref, vtp_ref, a0_ref, l0_ref, rank_ref, b_ref, ht_scr, w0, w1, acc_scr)
    _peer_block(u1_ref, vt0_ref, a1_ref, l1_ref, rank_ref, b_ref, ht_scr, w1, w0, acc_scr)

    @pl.when(s == pl.num_programs(1) - 1)
    def _():
        f = (acc_scr[...] + _dot(vtl_ref[...], w1[...])).T
        y_ref[...] = _layer_norm(alpha * h_ref[...] + f, g2_ref[...], b2_ref[...])


def _peer(hid, rank, b, a, cnt, u_b, vt_b, g2, b2, alpha):
    t = hid.shape[0]
    tm = TOKEN_TILE
    nch = tm // LANES
    nblk = u_b.shape[0] // EXPERT_BLOCK
    rows = EXPERT_BLOCK // N_KEYS
    row = lambda i, s: (i, 0)
    fixed2 = lambda i, s: (0, 0)
    full_spec = pl.BlockSpec((PEER_HEADS, nch, N_KEYS, LANES), lambda i, s: (0, i, 0, 0))
    rows_spec = lambda f: pl.BlockSpec((PEER_HEADS, nch, rows, LANES), lambda i, s: (0, i, f(s), 0))
    u_spec = lambda f: pl.BlockSpec((EXPERT_BLOCK, D_MODEL), lambda i, s: (f(s), 0))
    vt_spec = lambda f: pl.BlockSpec((D_MODEL, EXPERT_BLOCK), lambda i, s: (0, f(s)))
    even = lambda s: 2 * s
    odd = lambda s: 2 * s + 1
    return pl.pallas_call(
        functools.partial(_peer_kernel, alpha=alpha),
        grid=(t // tm, nblk // 2),
        in_specs=[pl.BlockSpec((tm, D_MODEL), row), full_spec, full_spec,
                  rows_spec(even), rows_spec(even), rows_spec(odd), rows_spec(odd),
                  u_spec(even), u_spec(odd),
                  vt_spec(lambda s: jnp.maximum(2 * s - 1, 0)), vt_spec(even), vt_spec(lambda s: nblk - 1),
                  pl.BlockSpec(g2.shape, fixed2), pl.BlockSpec(b2.shape, fixed2)],
        out_specs=pl.BlockSpec((tm, D_MODEL), row),
        out_shape=jax.ShapeDtypeStruct((t, D_MODEL), F32),
        scratch_shapes=[pltpu.VMEM((D_MODEL, tm), BF16),
                        pltpu.VMEM((EXPERT_BLOCK, tm), BF16),
                        pltpu.VMEM((EXPERT_BLOCK, tm), BF16),
                        pltpu.VMEM((D_MODEL, tm), F32)],
        compiler_params=_cparams(("parallel", "arbitrary")),
        name="peer_experts",
    )(hid, rank, b, a, cnt, a, cnt, u_b, u_b, vt_b, vt_b, vt_b, g2, b2)


def _finish(x, att, mls, lw, alpha):
    hid, rank, b, a, cnt = _mid(att, mls, x, lw["wo_a"], lw["wo_m"], lw["g1"], lw["b1"], lw["wq_t"], lw["keys"], alpha)
    return _peer(hid, rank, b, a, cnt, lw["u"], lw["v_t"], lw["g2"], lw["b2"], alpha)


def _prompt_bias_index():
    q = np.arange(WINDOW)[:, None]
    kband = np.arange(2 * WINDOW)[None, :]
    dist = q + WINDOW - kband
    return _bias_index_table(dist, (dist >= 0) & (dist < WINDOW))


def _sample_bias_index(steps):
    q = np.arange(SAMPLE_PAD)[:, None]
    c = np.arange(WINDOW)[None, :]
    dist_c = WINDOW - c + q
    idx_c = _bias_index_table(dist_c, (dist_c >= 0) & (dist_c < WINDOW))
    jn = np.arange(LANES)[None, :]
    dist_n = q - jn
    idx_n = _bias_index_table(dist_n, (dist_n >= 0) & (dist_n < WINDOW) & (jn < steps))
    return np.concatenate([idx_c, idx_n], axis=1)


def _per_row_head_layout(tab):
    _, steps, nk = tab.shape
    t = tab.reshape(N_KV_HEADS, 2, 2, steps, nk)
    return t.transpose(0, 2, 1, 3, 4).reshape(N_KV_HEADS, 2, 2 * steps, nk)


def kernel(x_prompt, x_sample, cache_k_win, cache_v_win, state_C, state_n, state_m, w_in, gate_bias,
           attn_sinks, rel_bias, mlstm_norm_w, w_out, ln1_g, ln1_b, peer_wq, peer_keys, peer_u, peer_v,
           ln2_g, ln2_b):
    depth = w_in.shape[0]
    alpha = (2 * depth) ** 0.25
    bsz, seq, _ = x_prompt.shape
    db, ds, _ = x_sample.shape
    assert bsz == 1 and seq % TOKEN_TILE == 0 and ds <= SAMPLE_PAD and (db * ds) % TOKEN_TILE == 0
    assert PAST_LEN >= WINDOW

    bias_p = _bias_table(_prompt_bias_index(), rel_bias)
    bias_s = _bias_table(_sample_bias_index(ds), rel_bias)
    bias_sc = _per_row_head_layout(bias_s[:, :, :WINDOW])
    bias_sn = _per_row_head_layout(bias_s[:, :, WINDOW:])

    xp = x_prompt[0]
    xs = x_sample
    pk, pv, pc, pn, pm = [], [], [], [], []
    sk, sv, sc, sn, sm = [], [], [], [], []
    for l in range(depth):
        w_main, w_gate = _projection_weights(w_in[l])
        gb_row = jnp.pad(gate_bias[l], (0, LANES - 2 * N_HEADS_M))[None, :].astype(F32)
        nw_row = mlstm_norm_w[l].reshape(1, MLSTM_WIDTH).astype(F32)
        sinks = attn_sinks[l].astype(F32)
        sinks_p = jnp.broadcast_to(sinks[:, None, None], (N_HEADS_ATT, 1, LANES))
        sinks_s = _per_row_head_layout(
            jnp.broadcast_to(sinks[:, None, None], (N_HEADS_ATT, SAMPLE_PAD, LANES)))
        lw = dict(
            wo_a=w_out[l, :ATT_WIDTH].astype(BF16), wo_m=w_out[l, ATT_WIDTH:].astype(BF16),
            g1=ln1_g[l][None, :], b1=ln1_b[l][None, :],
            wq_t=peer_wq[l].T.astype(BF16),
            keys=peer_keys[l].reshape(2 * PEER_HEADS, N_KEYS, PEER_HALF).astype(BF16),
            u=peer_u[l].astype(BF16), v_t=peer_v[l].T.astype(BF16),
            g2=ln2_g[l][None, :], b2=ln2_b[l][None, :])

        q_a, k_x, v_x, q_m, k_m, v_m, o_m, gates = _project(xp, w_main, w_gate)
        att = _attn_prompt(q_a, k_x, v_x, bias_p, sinks_p)
        zeros_c = jnp.zeros((1, N_HEADS_M, M_HEAD_DIM, M_HEAD_DIM), F32)
        zeros_v = jnp.zeros((1, N_HEADS_M, 1, LANES), F32)
        mls, c_p, n_p, m_p = _mlstm(q_m, k_m, v_m, o_m, gates, gb_row, nw_row, zeros_c, zeros_v, zeros_v,
                                    rows=M_HEAD_DIM, valid=M_HEAD_DIM, seqs=1)
        tail_k = k_x[seq - WINDOW:].reshape(WINDOW, N_KV_HEADS, 2, LANES)[:, :, 0, :HEAD_DIM]
        tail_v = v_x[seq - WINDOW:].reshape(WINDOW, N_KV_HEADS, 2, LANES)[:, :, 0, :HEAD_DIM]
        pk.append(tail_k[None])
        pv.append(tail_v[None])
        pc.append(c_p)
        pn.append(n_p[:, :, 0, :])
        pm.append(m_p[:, :, 0, 0])
        xp = _finish(xp, att, mls, lw, alpha)

        xs_pad = jnp.pad(xs, ((0, 0), (0, SAMPLE_PAD - ds), (0, 0))).reshape(db * SAMPLE_PAD, D_MODEL)
        q_a, k_x, v_x, q_m, k_m, v_m, o_m, gates = _project(xs_pad, w_main, w_gate)
        q_rows = q_a.reshape(db, SAMPLE_PAD, 4, LANES).transpose(0, 2, 1, 3).reshape(db, 4 * SAMPLE_PAD, LANES)
        att_rows = _attn_sample(q_rows, k_x.reshape(db, SAMPLE_PAD, 512), v_x.reshape(db, SAMPLE_PAD, 512),
                                cache_k_win[l].reshape(db, WINDOW, LANES).astype(F32),
                                cache_v_win[l].reshape(db, WINDOW, LANES).astype(F32),
                                bias_sc, bias_sn, sinks_s)
        att = att_rows.reshape(db, 4, SAMPLE_PAD, LANES).transpose(0, 2, 1, 3)[:, :ds].reshape(db * ds, ATT_WIDTH)
        c0 = state_C[l].astype(F32)
        n0 = state_n[l].astype(F32)[:, :, None, :]
        m0 = jnp.broadcast_to(state_m[l].astype(F32)[:, :, None, None], (db, N_HEADS_M, 1, LANES))
        mls, c_s, n_s, m_s = _mlstm(q_m, k_m, v_m, o_m, gates, gb_row, nw_row, c0, n0, m0,
                                    rows=SAMPLE_PAD, valid=ds, seqs=SAMPLE_SEQS)
        mls = mls.reshape(db, SAMPLE_PAD, MLSTM_WIDTH)[:, :ds].reshape(db * ds, MLSTM_WIDTH)
        k_new = k_x.reshape(db, SAMPLE_PAD, N_KV_HEADS, 2, LANES)[:, :ds, :, 0, :HEAD_DIM]
        v_new = v_x.reshape(db, SAMPLE_PAD, N_KV_HEADS, 2, LANES)[:, :ds, :, 0, :HEAD_DIM]
        sk.append(jnp.concatenate([cache_k_win[l].astype(F32), k_new], axis=1)[:, -WINDOW:])
        sv.append(jnp.concatenate([cache_v_win[l].astype(F32), v_new], axis=1)[:, -WINDOW:])
        sc.append(c_s)
        sn.append(n_s[:, :, 0, :])
        sm.append(m_s[:, :, 0, 0])
        xs = _finish(xs.reshape(db * ds, D_MODEL), att, mls, lw, alpha).reshape(db, ds, D_MODEL)

    return (xp[None], xs, jnp.stack(pk), jnp.stack(pv), jnp.stack(pc), jnp.stack(pn), jnp.stack(pm),
            jnp.stack(sk), jnp.stack(sv), jnp.stack(sc), jnp.stack(sn), jnp.stack(sm))
```

```python
import functools
import math

import numpy as np
import jax
import jax.numpy as jnp
from jax import lax
from jax.experimental import pallas as pl
from jax.experimental.pallas import tpu as pltpu

F32 = jnp.float32
BF16 = jnp.bfloat16

D_MODEL = 1024
HEAD_DIM = 64
N_HEADS_ATT = 8
N_KV_HEADS = 2
GQA_GROUP = N_HEADS_ATT // N_KV_HEADS
WINDOW = 128
NUM_BUCKETS = 32
MAX_DISTANCE = 128
N_HEADS_M = 4
M_HEAD_DIM = 128
MLSTM_WIDTH = N_HEADS_M * M_HEAD_DIM
ATT_WIDTH = N_HEADS_ATT * HEAD_DIM
PEER_HEADS = 8
N_KEYS = 128
PEER_TOPK = 16
PEER_HALF = 128
LN_EPS = 1e-5
NEG_INF = -1e30
PAST_LEN = 16384

LANES = 128
SUBLANES = 8
VMEM_LIMIT = 56 * 1024 * 1024

TOKEN_TILE = 512
EXPERT_BLOCK = 1024
SAMPLE_PAD = 8


def _cparams(sem, flags=None):
    return pltpu.CompilerParams(dimension_semantics=sem, vmem_limit_bytes=VMEM_LIMIT, flags=flags)


def _dot(a, b):
    return jnp.dot(a, b, preferred_element_type=F32)


def _dot_nt(a, b):
    return lax.dot_general(a, b, (((1,), (1,)), ((), ())), preferred_element_type=F32)


def _dot_tn(a, b):
    return lax.dot_general(a, b, (((0,), (0,)), ((), ())), preferred_element_type=F32)


def _layer_norm(x, g, b):
    mu = jnp.mean(x, axis=-1, keepdims=True)
    xc = x - mu
    var = jnp.mean(xc * xc, axis=-1, keepdims=True)
    return xc * lax.rsqrt(var + LN_EPS) * g + b


def _t5_bucket_np(dist):
    n = np.maximum(dist, 0)
    exact = NUM_BUCKETS // 2
    nf = np.maximum(n, 1).astype(np.float64)
    val = np.log(nf / exact) / math.log(MAX_DISTANCE / exact) * (NUM_BUCKETS - exact)
    frac = np.abs(val - np.round(val))
    assert not np.any((frac < 1e-6) & (n > exact) & (n != MAX_DISTANCE)), "bucket boundary is rounding sensitive"
    large = np.minimum(exact + np.floor(val + 1e-9).astype(np.int64), NUM_BUCKETS - 1)
    return np.where(n < exact, n, large).astype(np.int32)


def _bias_index_table(dist, valid):
    return np.where(valid, _t5_bucket_np(dist), -1).astype(np.int32)


def _bias_table_kernel(idx_ref, rb_ref, out_ref):
    idx = idx_ref[...]
    for h in range(N_HEADS_ATT):
        acc = jnp.full(idx.shape, NEG_INF, F32)
        for b in range(NUM_BUCKETS):
            acc = jnp.where(idx == b, rb_ref[b, h], acc)
        out_ref[h] = acc


def _bias_table(idx_np, rel_bias):
    r, c = idx_np.shape
    return pl.pallas_call(
        _bias_table_kernel,
        out_shape=jax.ShapeDtypeStruct((N_HEADS_ATT, r, c), F32),
        in_specs=[pl.BlockSpec(memory_space=pltpu.VMEM), pl.BlockSpec(memory_space=pltpu.SMEM)],
        out_specs=pl.BlockSpec(memory_space=pltpu.VMEM),
        name="bias_table",
    )(jnp.asarray(idx_np), rel_bias.astype(F32))


def _proj_kernel(x_ref, w_ref, wg_ref, q_ref, k_ref, v_ref, qm_ref, km_ref, vm_ref, om_ref, g_ref):
    x = x_ref[...]
    xb = x.astype(BF16)
    outs = (q_ref, k_ref, v_ref, qm_ref, km_ref, vm_ref, om_ref)
    for n, o_ref in enumerate(outs):
        z = _dot(xb, w_ref[:, n * 512:(n + 1) * 512])
        if o_ref is km_ref:
            z = z * (M_HEAD_DIM ** -0.5)
        o_ref[...] = z
    g_ref[...] = jnp.dot(x, wg_ref[...], precision=lax.Precision.HIGHEST, preferred_element_type=F32)


def _project(x, w_main, w_gate):
    t = x.shape[0]
    tm = TOKEN_TILE
    wide = jax.ShapeDtypeStruct((t, 512), F32)
    row = lambda i: (i, 0)
    fixed = lambda i: (0, 0)
    return pl.pallas_call(
        _proj_kernel,
        grid=(t // tm,),
        in_specs=[pl.BlockSpec((tm, D_MODEL), row),
                  pl.BlockSpec(w_main.shape, fixed),
                  pl.BlockSpec(w_gate.shape, fixed)],
        out_specs=[pl.BlockSpec((tm, 512), row)] * 7 + [pl.BlockSpec((tm, LANES), row)],
        out_shape=[wide] * 7 + [jax.ShapeDtypeStruct((t, LANES), F32)],
        compiler_params=_cparams(("parallel",)),
        name="in_proj",
    )(x, w_main, w_gate)


def _projection_weights(w_in):
    q_a = w_in[:, 0:512]
    k_a = w_in[:, 512:640]
    v_a = w_in[:, 640:768]
    rest = w_in[:, 768:768 + 4 * MLSTM_WIDTH]
    gates = w_in[:, 768 + 4 * MLSTM_WIDTH:]
    zero = jnp.zeros((D_MODEL, HEAD_DIM), w_in.dtype)

    def lo_hi(w):
        cols = []
        for kv in range(N_KV_HEADS):
            wk = w[:, kv * HEAD_DIM:(kv + 1) * HEAD_DIM]
            cols += [wk, zero, zero, wk]
        return jnp.concatenate(cols, axis=1)

    w_main = jnp.concatenate([q_a, lo_hi(k_a), lo_hi(v_a), rest], axis=1).astype(BF16)
    w_gate = jnp.pad(gates, ((0, 0), (0, LANES - gates.shape[1]))).astype(F32)
    return w_main, w_gate


def _softmax_pieces(pieces, sink):
    mx = sink
    for s in pieces:
        mx = jnp.maximum(mx, jnp.max(s, axis=-1, keepdims=True))
    es = [jnp.exp(s - mx) for s in pieces]
    den = jnp.exp(sink - mx)
    for e in es:
        den = den + jnp.sum(e, axis=-1, keepdims=True)
    inv = 1.0 / den
    return [e * inv for e in es]


def _attn_prompt_kernel(q_ref, kp_ref, kc_ref, vp_ref, vc_ref, bias_ref, sink_ref, o_ref):
    n = pl.program_id(0)
    scale = HEAD_DIM ** -0.5
    nh = N_HEADS_ATT
    bmm = functools.partial(jnp.einsum, preferred_element_type=F32)

    def per_head(ref, col_of):
        x = ref[...].astype(BF16)
        return jnp.stack([x[:, col_of(h) * LANES:(col_of(h) + 1) * LANES] for h in range(nh)], axis=0)

    kv_col = lambda h: (h // GQA_GROUP) * 2 + h % 2
    q = per_head(q_ref, lambda h: h // 2)
    kp = per_head(kp_ref, kv_col)
    kc = per_head(kc_ref, kv_col)
    vp = per_head(vp_ref, kv_col)
    vc = per_head(vc_ref, kv_col)
    bias = bias_ref[...]
    sink = sink_ref[...][:, :, 0:1]
    s_prev = bmm("hqd,hkd->hqk", q, kp) * scale + bias[:, :, :WINDOW]
    s_prev = jnp.where(n > 0, s_prev, NEG_INF)
    s_cur = bmm("hqd,hkd->hqk", q, kc) * scale + bias[:, :, WINDOW:]
    p_prev, p_cur = _softmax_pieces([s_prev, s_cur], sink)
    out = bmm("hqk,hkd->hqd", p_prev.astype(BF16), vp) + bmm("hqk,hkd->hqd", p_cur.astype(BF16), vc)
    for pair in range(nh // 2):
        o_ref[:, pair * LANES:(pair + 1) * LANES] = out[2 * pair] + out[2 * pair + 1]


def _attn_prompt(q, kx, vx, bias, sinks):
    s = q.shape[0]
    nb = s // WINDOW
    cur = lambda n: (n, 0)
    prev = lambda n: (jnp.maximum(n - 1, 0), 0)
    blk = (WINDOW, 512)
    return pl.pallas_call(
        _attn_prompt_kernel,
        grid=(nb,),
        in_specs=[pl.BlockSpec(blk, cur), pl.BlockSpec(blk, prev), pl.BlockSpec(blk, cur),
                  pl.BlockSpec(blk, prev), pl.BlockSpec(blk, cur),
                  pl.BlockSpec(bias.shape, lambda n: (0, 0, 0)),
                  pl.BlockSpec(sinks.shape, lambda n: (0, 0, 0))],
        out_specs=pl.BlockSpec(blk, cur),
        out_shape=jax.ShapeDtypeStruct((s, ATT_WIDTH), F32),
        compiler_params=_cparams(("parallel",)),
        name="attn_prompt",
    )(q, kx, kx, vx, vx, bias, sinks)


SAMPLE_SEQS = 8


def _attn_sample_kernel(q_ref, kn_ref, vn_ref, ck_ref, cv_ref, bc_ref, bn_ref, sink_ref, o_ref):
    scale = HEAD_DIM ** -0.5
    lane = lax.broadcasted_iota(jnp.int32, (WINDOW, LANES), 1)
    low = lane < HEAD_DIM
    pad = jnp.zeros((WINDOW - SAMPLE_PAD, LANES), F32)
    rows = 2 * SAMPLE_PAD
    ns = q_ref.shape[0]
    qk_t = lambda a, b: jnp.einsum("bqd,bkd->bqk", a, b, preferred_element_type=F32)
    pv = lambda a, b: jnp.einsum("bqk,bkd->bqd", a, b, preferred_element_type=F32)
    q_all = q_ref[...].astype(BF16)
    kn = kn_ref[...]
    vn = vn_ref[...]
    ck = ck_ref[...]
    cv = cv_ref[...]
    ck_sw = pltpu.roll(ck, HEAD_DIM, 2)
    cv_sw = pltpu.roll(cv, HEAD_DIM, 2)
    pad = jnp.zeros((ns, WINDOW - SAMPLE_PAD, LANES), F32)
    for kv in range(N_KV_HEADS):
        qk = q_all[:, kv * rows:(kv + 1) * rows]
        acc = jnp.zeros((ns, rows, LANES), F32)
        for x in range(2):
            col = (kv * 2 + x) * LANES
            keep = low if x == 0 else jnp.logical_not(low)
            src_k, src_v = (ck, cv) if kv == x else (ck_sw, cv_sw)
            kc = jnp.where(keep, src_k, 0.0).astype(BF16)
            vc = jnp.where(keep, src_v, 0.0).astype(BF16)
            knx = jnp.concatenate([kn[:, :, col:col + LANES], pad], axis=1).astype(BF16)
            vnx = jnp.concatenate([vn[:, :, col:col + LANES], pad], axis=1).astype(BF16)
            s_c = qk_t(qk, kc) * scale + bc_ref[kv, x]
            s_n = qk_t(qk, knx) * scale + bn_ref[kv, x]
            p_c, p_n = _softmax_pieces([s_c, s_n], sink_ref[kv, x][:, 0:1])
            acc = acc + pv(p_c.astype(BF16), vc) + pv(p_n.astype(BF16), vnx)
        o_ref[:, kv * rows:(kv + 1) * rows, :] = acc


def _attn_sample(q, knew, vnew, cache_k, cache_v, bias_c, bias_n, sinks):
    db = q.shape[0]
    ns = SAMPLE_SEQS
    assert db % ns == 0
    b3 = lambda b: (b, 0, 0)
    f4 = lambda b: (0, 0, 0, 0)
    return pl.pallas_call(
        _attn_sample_kernel,
        grid=(db // ns,),
        in_specs=[pl.BlockSpec((ns,) + q.shape[1:], b3),
                  pl.BlockSpec((ns,) + knew.shape[1:], b3),
                  pl.BlockSpec((ns,) + vnew.shape[1:], b3),
                  pl.BlockSpec((ns, WINDOW, LANES), b3),
                  pl.BlockSpec((ns, WINDOW, LANES), b3),
                  pl.BlockSpec(bias_c.shape, f4),
                  pl.BlockSpec(bias_n.shape, f4),
                  pl.BlockSpec(sinks.shape, f4)],
        out_specs=pl.BlockSpec((ns,) + q.shape[1:], b3),
        out_shape=jax.ShapeDtypeStruct(q.shape, F32),
        compiler_params=_cparams(("parallel",)),
        name="attn_sample",
    )(q, knew, vnew, cache_k, cache_v, bias_c, bias_n, sinks)


def _log_sigmoid(x):
    return jnp.minimum(x, 0.0) - jnp.log1p(jnp.exp(-jnp.abs(x)))


def _cumsum_rows(x):
    axis = x.ndim - 2
    rows = x.shape[axis]
    idx = lax.broadcasted_iota(jnp.int32, x.shape, axis)
    k = 1
    while k < rows:
        x = x + jnp.where(idx >= k, pltpu.roll(x, k, axis), 0.0)
        k *= 2
    return x


def _mlstm_kernel(q_ref, k_ref, v_ref, o_ref, g_ref, gb_ref, nw_ref, c0_ref, n0_ref, m0_ref,
                  h_ref, c_out_ref, n_out_ref, m_out_ref, c_scr, n_scr, m_scr, *, rows, valid):
    c_idx = pl.program_id(1)

    @pl.when(c_idx == 0)
    def _():
        c_scr[...] = c0_ref[...]
        n_scr[...] = n0_ref[...]
        m_scr[...] = m0_ref[...]

    ns = c_scr.shape[0]
    nh = N_HEADS_M
    keys = M_HEAD_DIM
    bmm = functools.partial(jnp.einsum, preferred_element_type=F32)

    def heads(ref):
        x = ref[...].reshape(ns, rows, nh * M_HEAD_DIM)
        return jnp.concatenate([x[:, :, h * M_HEAD_DIM:(h + 1) * M_HEAD_DIM] for h in range(nh)], axis=0)

    def pad_rows(x):
        if rows == keys:
            return x
        return jnp.concatenate([x, jnp.zeros((x.shape[0], keys - rows, x.shape[2]), x.dtype)], axis=1)

    g = g_ref[...].reshape(ns, rows, LANES) + gb_ref[...]
    ridx = lax.broadcasted_iota(jnp.int32, g.shape, 1)
    ig_all = g
    lf_all = _log_sigmoid(g)
    if valid < rows:
        ig_all = jnp.where(ridx < valid, ig_all, NEG_INF)
        lf_all = jnp.where(ridx < valid, lf_all, 0.0)
    b_all = _cumsum_rows(lf_all)
    bc = jnp.concatenate([b_all[:, :, nh + h:nh + h + 1] for h in range(nh)], axis=0)
    igc = jnp.concatenate([ig_all[:, :, h:h + 1] for h in range(nh)], axis=0)

    t_idx = lax.broadcasted_iota(jnp.int32, (rows, keys), 0)
    s_idx = lax.broadcasted_iota(jnp.int32, (rows, keys), 1)
    eye = t_idx == s_idx
    causal = s_idx <= t_idx

    q = heads(q_ref)
    k = heads(k_ref)
    v = heads(v_ref)
    qb = q.astype(BF16)
    kb = pad_rows(k).astype(BF16)
    vb = pad_rows(v).astype(BF16)
    brow = jnp.sum(jnp.where(eye, bc, 0.0), axis=1, keepdims=True)
    igrow = jnp.sum(jnp.where(eye, igc, 0.0), axis=1, keepdims=True)
    logd = jnp.where(causal, bc - brow + igrow, -jnp.inf)
    c_old = jnp.concatenate([c_scr[:, h] for h in range(nh)], axis=0)
    n_old = jnp.concatenate([n_scr[:, h] for h in range(nh)], axis=0)
    m_prev = jnp.concatenate([m_scr[:, h] for h in range(nh)], axis=0)[:, :, 0:1]
    inter = bc + m_prev
    mt = jnp.maximum(inter, jnp.max(logd, axis=-1, keepdims=True))
    w_inter = jnp.exp(inter - mt)
    a = jnp.exp(logd - mt) * bmm("bqd,bkd->bqk", qb, kb)
    num = bmm("bqk,bkd->bqd", a.astype(BF16), vb) + w_inter * bmm("bqd,bvd->bqv", qb, c_old.astype(BF16))
    den = jnp.sum(a, axis=-1, keepdims=True) + w_inter * jnp.sum(q * n_old, axis=-1, keepdims=True)
    hid = num / jnp.maximum(jnp.abs(den), jnp.exp(-mt))
    mu = jnp.mean(hid, axis=-1, keepdims=True)
    hc = hid - mu
    var = jnp.mean(hc * hc, axis=-1, keepdims=True)
    hn = hc * lax.rsqrt(var + LN_EPS)
    for h in range(nh):
        sl = slice(h * M_HEAD_DIM, (h + 1) * M_HEAD_DIM)
        out = hn[h * ns:(h + 1) * ns].reshape(ns * rows, M_HEAD_DIM) * nw_ref[:, sl]
        h_ref[:, sl] = out * jax.nn.sigmoid(o_ref[:, sl])
    m_new = mt[:, rows - 1:rows]
    w_c = jnp.exp(inter[:, rows - 1:rows] - m_new)
    w_s = jnp.exp(bc[:, rows - 1:rows] - bc + igc - m_new)
    vs_t = jnp.swapaxes(pad_rows(v * w_s), 1, 2).astype(BF16)
    c_new = w_c * c_old + bmm("bvs,bsd->bvd", vs_t, kb)
    n_new = w_c * n_old + jnp.sum(w_s * k, axis=1, keepdims=True)
    m_row = jnp.broadcast_to(m_new, (nh * ns, 1, LANES))
    for h in range(nh):
        c_scr[:, h] = c_new[h * ns:(h + 1) * ns]
        n_scr[:, h] = n_new[h * ns:(h + 1) * ns]
        m_scr[:, h] = m_row[h * ns:(h + 1) * ns]

    @pl.when(c_idx == pl.num_programs(1) - 1)
    def _():
        c_out_ref[...] = c_scr[...]
        n_out_ref[...] = n_scr[...]
        m_out_ref[...] = m_scr[...]


def _mlstm(q, k, v, o, gates, gate_bias_row, norm_w_row, c0, n0, m0, *, rows, valid, seqs):
    batch = c0.shape[0]
    total = q.shape[0]
    nc = total // (batch * rows)
    assert batch % seqs == 0 and (seqs == 1 or nc == 1)
    seq = lambda b, c: (b * nc + c, 0)
    fixed = lambda b, c: (0, 0)
    st4 = lambda b, c: (b, 0, 0, 0)
    wide = pl.BlockSpec((seqs * rows, MLSTM_WIDTH), seq)
    c_spec = pl.BlockSpec((seqs, N_HEADS_M, M_HEAD_DIM, M_HEAD_DIM), st4)
    v_spec = pl.BlockSpec((seqs, N_HEADS_M, 1, LANES), st4)
    return pl.pallas_call(
        functools.partial(_mlstm_kernel, rows=rows, valid=valid),
        grid=(batch // seqs, nc),
        in_specs=[wide, wide, wide, wide,
                  pl.BlockSpec((seqs * rows, LANES), seq),
                  pl.BlockSpec((1, LANES), fixed),
                  pl.BlockSpec((1, MLSTM_WIDTH), fixed),
                  c_spec, v_spec, v_spec],
        out_specs=[wide, c_spec, v_spec, v_spec],
        out_shape=[jax.ShapeDtypeStruct((total, MLSTM_WIDTH), F32),
                   jax.ShapeDtypeStruct(c0.shape, F32),
                   jax.ShapeDtypeStruct(n0.shape, F32),
                   jax.ShapeDtypeStruct(m0.shape, F32)],
        scratch_shapes=[pltpu.VMEM((seqs, N_HEADS_M, M_HEAD_DIM, M_HEAD_DIM), F32),
                        pltpu.VMEM((seqs, N_HEADS_M, 1, LANES), F32),
                        pltpu.VMEM((seqs, N_HEADS_M, 1, LANES), F32)],
        compiler_params=_cparams(("parallel", "arbitrary")),
        name="mlstm",
    )(q, k, v, o, gates, gate_bias_row, norm_w_row, c0, n0, m0)


def _sorting_network(n):
    pairs = []

    def merge(lo, m, r):
        step = r * 2
        if step < m:
            merge(lo, m, step)
            merge(lo + r, m, step)
            pairs.extend((i, i + r) for i in range(lo + r, lo + m - r, step))
        else:
            pairs.append((lo, lo + r))

    def sort(lo, m):
        if m > 1:
            sort(lo, m // 2)
            sort(lo + m // 2, m // 2)
            merge(lo, m, 1)

    sort(0, n)
    return pairs


_SORT_PAIRS = _sorting_network(N_KEYS // SUBLANES)
TOP_ROWS = 24


def _top_values(x, count):
    v = [x[k * SUBLANES:(k + 1) * SUBLANES] for k in range(x.shape[0] // SUBLANES)]
    for i, j in _SORT_PAIRS:
        v[i], v[j] = jnp.maximum(v[i], v[j]), jnp.minimum(v[i], v[j])
    tops = []
    for r in range(count):
        m = jnp.max(v[0], axis=0, keepdims=True)
        tops.append(m)
        left = count - 1 - r
        if left:
            took = v[0] == m
            nxt = v[1:] + [jnp.full_like(v[0], -jnp.inf)]
            v = [jnp.where(took, nxt[k], v[k]) for k in range(min(len(v), left))]
    return tops


def _rows(vals):
    return jnp.concatenate(vals, axis=0)


def _mid_kernel(att_ref, mls_ref, x_ref, woa_ref, wom_ref, g1_ref, b1_ref, wqt_ref, keys_ref,
                h_ref, rank_ref, b_ref, a_ref, l_ref, s_scr, top_scr, *, alpha):
    tm = x_ref.shape[0]
    nch = tm // LANES
    mix = _dot(att_ref[...].astype(BF16), woa_ref[...]) + _dot(mls_ref[...].astype(BF16), wom_ref[...])
    hid = _layer_norm(alpha * x_ref[...] + mix, g1_ref[...], b1_ref[...])
    h_ref[...] = hid
    q_t = _dot_nt(wqt_ref[...], hid.astype(BF16))
    for hc in range(2 * PEER_HEADS):
        s_t = _dot(keys_ref[hc], q_t[hc * PEER_HALF:(hc + 1) * PEER_HALF].astype(BF16))
        for tc in range(nch):
            s_scr[hc, tc] = s_t[:, tc * LANES:(tc + 1) * LANES]

    n_items = PEER_HEADS * nch
    pad = [jnp.full((TOP_ROWS - PEER_TOPK - 1, LANES), -jnp.inf, F32)]

    def find_tops(item, slot):
        h = item // nch
        tc = item % nch
        for c in range(2):
            top_scr[slot, c] = _rows(_top_values(s_scr[2 * h + c, tc], PEER_TOPK + 1) + pad)

    find_tops(0, 0)

    def route_body(it, carry):
        find_tops(jnp.minimum(it + 1, n_items - 1), (it + 1) % 2)
        h = it // nch
        tc = it % nch
        s1 = s_scr[2 * h, tc]
        s2 = s_scr[2 * h + 1, tc]
        tops1 = top_scr[it % 2, 0]
        tops2 = top_scr[it % 2, 1]
        t1 = [tops1[r:r + 1] for r in range(PEER_TOPK + 1)]
        t2 = [tops2[r:r + 1] for r in range(PEER_TOPK + 1)]
        wide = [tops1[:SUBLANES] + t2[r] for r in range(PEER_TOPK + 1)]
        tall = [t1[SUBLANES + k] + t2[0] for k in range(PEER_TOPK + 1 - SUBLANES)]
        cmax = t1[0] + t2[0]
        z = jnp.zeros_like(cmax)
        tau = below = cmax
        for r in range(PEER_TOPK + 1):
            head = jnp.maximum(jnp.max(wide[0], axis=0, keepdims=True), tall[0])
            if r < PEER_TOPK:
                tau = head
                z = z + jnp.exp(head - cmax)
            else:
                below = head
            left = PEER_TOPK - r
            if left:
                took_w = wide[0] == head
                took_t = tall[0] == head
                nxt_w = wide[1:] + [jnp.full_like(wide[0], -jnp.inf)]
                nxt_t = tall[1:] + [jnp.full_like(tall[0], -jnp.inf)]
                wide = [jnp.where(took_w, nxt_w[k], wide[k]) for k in range(min(len(wide), left))]
                tall = [jnp.where(took_t, nxt_t[k], tall[k]) for k in range(min(len(tall), left))]
        cut = 0.5 * (tau + below)
        rank = jnp.zeros_like(s2)
        count = jnp.zeros_like(s1)
        for r in range(PEER_TOPK):
            rank = jnp.where(t2[r] > s2, (r + 1) * RANK_SCALE, rank)
            count = jnp.where(s1 >= cut - t2[r], (r + 1) * RANK_SCALE, count)
        rank_ref[h, tc] = rank.astype(BF16)
        l_ref[h, tc] = count
        b_ref[h, tc] = jnp.exp(s2 - t2[0]).astype(BF16)
        a_ref[h, tc] = jnp.exp(s1 - (t1[0] + jnp.log(z)))
        return carry

    lax.fori_loop(0, n_items, route_body, 0)


def _mid(att, mls, x, wo_a, wo_m, g1, b1, wq_t, keys, alpha):
    t = x.shape[0]
    tm = TOKEN_TILE
    nch = tm // LANES
    row = lambda i: (i, 0)
    fixed2 = lambda i: (0, 0)
    fixed3 = lambda i: (0, 0, 0)
    tile4 = lambda i: (0, i, 0, 0)
    s_shape = lambda dt: jax.ShapeDtypeStruct((PEER_HEADS, t // LANES, N_KEYS, LANES), dt)
    s_spec = pl.BlockSpec((PEER_HEADS, nch, N_KEYS, LANES), tile4)
    return pl.pallas_call(
        functools.partial(_mid_kernel, alpha=alpha),
        grid=(t // tm,),
        in_specs=[pl.BlockSpec((tm, ATT_WIDTH), row), pl.BlockSpec((tm, MLSTM_WIDTH), row),
                  pl.BlockSpec((tm, D_MODEL), row),
                  pl.BlockSpec(wo_a.shape, fixed2), pl.BlockSpec(wo_m.shape, fixed2),
                  pl.BlockSpec(g1.shape, fixed2), pl.BlockSpec(b1.shape, fixed2),
                  pl.BlockSpec(wq_t.shape, fixed2), pl.BlockSpec(keys.shape, fixed3)],
        out_specs=[pl.BlockSpec((tm, D_MODEL), row), s_spec, s_spec, s_spec, s_spec],
        out_shape=[jax.ShapeDtypeStruct((t, D_MODEL), F32), s_shape(BF16), s_shape(BF16), s_shape(F32), s_shape(F32)],
        scratch_shapes=[pltpu.VMEM((2 * PEER_HEADS, nch, N_KEYS, LANES), F32),
                        pltpu.VMEM((2, 2, TOP_ROWS, LANES), F32)],
        compiler_params=_cparams(("parallel",)),
        name="out_proj_router",
    )(att, mls, x, wo_a, wo_m, g1, b1, wq_t, keys)


def _gelu(x):
    return 0.5 * x * (1.0 + lax.erf(x * (2.0 ** -0.5)))


ACT_TOKENS = 256
ACT_EXPERTS = (512, 512)
assert sum(ACT_EXPERTS) == EXPERT_BLOCK


RANK_SCALE = 256.0


def _row_tile(row):
    packed = jnp.broadcast_to(row, (2 * SUBLANES, row.shape[1])).astype(BF16)
    reps = N_KEYS // (2 * SUBLANES)
    return jnp.broadcast_to(packed[None], (reps,) + packed.shape).reshape(N_KEYS, row.shape[1])


def _zero_after(x):
    bits = lax.bitcast_convert_type(x, jnp.uint32)
    bits = lax.shift_right_logical(lax.shift_right_logical(bits, jnp.uint32(16)), jnp.uint32(16))
    return lax.bitcast_convert_type(bits, F32)


def _peer_block(u_ref, vt_ref, a_ref, l_ref, rank_ref, b_ref, ht_scr, w_new, w_old, acc_scr):
    tm = ht_scr.shape[1]
    n_tp = tm // ACT_TOKENS
    starts = [sum(ACT_EXPERTS[:g]) for g in range(len(ACT_EXPERTS))]
    tiles = [(r0, nr, tp) for r0, nr in zip(starts, ACT_EXPERTS) for tp in range(n_tp)]
    slice_at = {(sl + 1) * len(tiles) // (n_tp + 1): sl for sl in range(n_tp)}
    for k, (r0, nr, tp) in enumerate(tiles):
        zero = None
        if k in slice_at:
            cols = slice(slice_at[k] * ACT_TOKENS, (slice_at[k] + 1) * ACT_TOKENS)
            upd = acc_scr[:, cols] + _dot(vt_ref[...], w_old[:, cols])
            acc_scr[:, cols] = upd
            zero = _zero_after(upd[0:1, 0:LANES])
        act = _dot(u_ref[r0:r0 + nr, :], ht_scr[:, tp * ACT_TOKENS:(tp + 1) * ACT_TOKENS])
        for d in range(nr // N_KEYS):
            il = r0 // N_KEYS + d
            for half in range(ACT_TOKENS // LANES):
                tc = tp * (ACT_TOKENS // LANES) + half
                gate = jnp.zeros((N_KEYS, LANES), BF16)
                if zero is not None and d == 0 and half == 0:
                    gate = gate + _row_tile(zero)
                for h in range(PEER_HEADS):
                    room = jnp.maximum(_row_tile(l_ref[h, tc, il:il + 1, :]) - rank_ref[h, tc], 0)
                    gate = gate + jnp.minimum(_row_tile(a_ref[h, tc, il:il + 1, :]) * b_ref[h, tc], room)
                piece = act[d * N_KEYS:(d + 1) * N_KEYS, half * LANES:(half + 1) * LANES]
                w_new[il * N_KEYS:(il + 1) * N_KEYS, tc * LANES:(tc + 1) * LANES] = gate * _gelu(piece).astype(BF16)


def _peer_kernel(h_ref, rank_ref, b_ref, a0_ref, l0_ref, a1_ref, l1_ref, u0_ref, u1_ref,
                 vtp_ref, vt0_ref, vtl_ref, g2_ref, b2_ref, y_ref, ht_scr, w0, w1, acc_scr, *, alpha):
    s = pl.program_id(1)

    @pl.when(s == 0)
    def _():
        ht_scr[...] = h_ref[...].T.astype(BF16)
        acc_scr[...] = jnp.zeros_like(acc_scr)
        w1[...] = jnp.zeros_like(w1)

    _peer_block(u0_ref, vtp_ref, a0_ref, l0_ref, rank_ref, b_ref, ht_scr, w0, w1, acc_scr)
    _peer_block(u1_ref, vt0_ref, a1_ref, l1_ref, rank_ref, b_ref, ht_scr, w1, w0, acc_scr)

    @pl.when(s == pl.num_programs(1) - 1)
    def _():
        f = (acc_scr[...] + _dot(vtl_ref[...], w1[...])).T
        y_ref[...] = _layer_norm(alpha * h_ref[...] + f, g2_ref[...], b2_ref[...])


def _peer(hid, rank, b, a, cnt, u_b, vt_b, g2, b2, alpha):
    t = hid.shape[0]
    tm = TOKEN_TILE
    nch = tm // LANES
    nblk = u_b.shape[0] // EXPERT_BLOCK
    rows = EXPERT_BLOCK // N_KEYS
    row = lambda i, s: (i, 0)
    fixed2 = lambda i, s: (0, 0)
    full_spec = pl.BlockSpec((PEER_HEADS, nch, N_KEYS, LANES), lambda i, s: (0, i, 0, 0))
    rows_spec = lambda f: pl.BlockSpec((PEER_HEADS, nch, rows, LANES), lambda i, s: (0, i, f(s), 0))
    u_spec = lambda f: pl.BlockSpec((EXPERT_BLOCK, D_MODEL), lambda i, s: (f(s), 0))
    vt_spec = lambda f: pl.BlockSpec((D_MODEL, EXPERT_BLOCK), lambda i, s: (0, f(s)))
    even = lambda s: 2 * s
    odd = lambda s: 2 * s + 1
    return pl.pallas_call(
        functools.partial(_peer_kernel, alpha=alpha),
        grid=(t // tm, nblk // 2),
        in_specs=[pl.BlockSpec((tm, D_MODEL), row), full_spec, full_spec,
                  rows_spec(even), rows_spec(even), rows_spec(odd), rows_spec(odd),
                  u_spec(even), u_spec(odd),
                  vt_spec(lambda s: jnp.maximum(2 * s - 1, 0)), vt_spec(even), vt_spec(lambda s: nblk - 1),
                  pl.BlockSpec(g2.shape, fixed2), pl.BlockSpec(b2.shape, fixed2)],
        out_specs=pl.BlockSpec((tm, D_MODEL), row),
        out_shape=jax.ShapeDtypeStruct((t, D_MODEL), F32),
        scratch_shapes=[pltpu.VMEM((D_MODEL, tm), BF16),
                        pltpu.VMEM((EXPERT_BLOCK, tm), BF16),
                        pltpu.VMEM((EXPERT_BLOCK, tm), BF16),
                        pltpu.VMEM((D_MODEL, tm), F32)],
        compiler_params=_cparams(("parallel", "arbitrary")),
        name="peer_experts",
    )(hid, rank, b, a, cnt, a, cnt, u_b, u_b, vt_b, vt_b, vt_b, g2, b2)


def _finish(x, att, mls, lw, alpha):
    hid, rank, b, a, cnt = _mid(att, mls, x, lw["wo_a"], lw["wo_m"], lw["g1"], lw["b1"], lw["wq_t"], lw["keys"], alpha)
    return _peer(hid, rank, b, a, cnt, lw["u"], lw["v_t"], lw["g2"], lw["b2"], alpha)


def _prompt_bias_index():
    q = np.arange(WINDOW)[:, None]
    kband = np.arange(2 * WINDOW)[None, :]
    dist = q + WINDOW - kband
    return _bias_index_table(dist, (dist >= 0) & (dist < WINDOW))


def _sample_bias_index(steps):
    q = np.arange(SAMPLE_PAD)[:, None]
    c = np.arange(WINDOW)[None, :]
    dist_c = WINDOW - c + q
    idx_c = _bias_index_table(dist_c, (dist_c >= 0) & (dist_c < WINDOW))
    jn = np.arange(LANES)[None, :]
    dist_n = q - jn
    idx_n = _bias_index_table(dist_n, (dist_n >= 0) & (dist_n < WINDOW) & (jn < steps))
    return np.concatenate([idx_c, idx_n], axis=1)


def _per_row_head_layout(tab):
    _, steps, nk = tab.shape
    t = tab.reshape(N_KV_HEADS, 2, 2, steps, nk)
    return t.transpose(0, 2, 1, 3, 4).reshape(N_KV_HEADS, 2, 2 * steps, nk)


def kernel(x_prompt, x_sample, cache_k_win, cache_v_win, state_C, state_n, state_m, w_in, gate_bias,
           attn_sinks, rel_bias, mlstm_norm_w, w_out, ln1_g, ln1_b, peer_wq, peer_keys, peer_u, peer_v,
           ln2_g, ln2_b):
    depth = w_in.shape[0]
    alpha = (2 * depth) ** 0.25
    bsz, seq, _ = x_prompt.shape
    db, ds, _ = x_sample.shape
    assert bsz == 1 and seq % TOKEN_TILE == 0 and ds <= SAMPLE_PAD and (db * ds) % TOKEN_TILE == 0
    assert PAST_LEN >= WINDOW

    bias_p = _bias_table(_prompt_bias_index(), rel_bias)
    bias_s = _bias_table(_sample_bias_index(ds), rel_bias)
    bias_sc = _per_row_head_layout(bias_s[:, :, :WINDOW])
    bias_sn = _per_row_head_layout(bias_s[:, :, WINDOW:])

    xp = x_prompt[0]
    xs = x_sample
    pk, pv, pc, pn, pm = [], [], [], [], []
    sk, sv, sc, sn, sm = [], [], [], [], []
    for l in range(depth):
        w_main, w_gate = _projection_weights(w_in[l])
        gb_row = jnp.pad(gate_bias[l], (0, LANES - 2 * N_HEADS_M))[None, :].astype(F32)
        nw_row = mlstm_norm_w[l].reshape(1, MLSTM_WIDTH).astype(F32)
        sinks = attn_sinks[l].astype(F32)
        sinks_p = jnp.broadcast_to(sinks[:, None, None], (N_HEADS_ATT, 1, LANES))
        sinks_s = _per_row_head_layout(
            jnp.broadcast_to(sinks[:, None, None], (N_HEADS_ATT, SAMPLE_PAD, LANES)))
        lw = dict(
            wo_a=w_out[l, :ATT_WIDTH].astype(BF16), wo_m=w_out[l, ATT_WIDTH:].astype(BF16),
            g1=ln1_g[l][None, :], b1=ln1_b[l][None, :],
            wq_t=peer_wq[l].T.astype(BF16),
            keys=peer_keys[l].reshape(2 * PEER_HEADS, N_KEYS, PEER_HALF).astype(BF16),
            u=peer_u[l].astype(BF16), v_t=peer_v[l].T.astype(BF16),
            g2=ln2_g[l][None, :], b2=ln2_b[l][None, :])

        q_a, k_x, v_x, q_m, k_m, v_m, o_m, gates = _project(xp, w_main, w_gate)
        att = _attn_prompt(q_a, k_x, v_x, bias_p, sinks_p)
        zeros_c = jnp.zeros((1, N_HEADS_M, M_HEAD_DIM, M_HEAD_DIM), F32)
        zeros_v = jnp.zeros((1, N_HEADS_M, 1, LANES), F32)
        mls, c_p, n_p, m_p = _mlstm(q_m, k_m, v_m, o_m, gates, gb_row, nw_row, zeros_c, zeros_v, zeros_v,
                                    rows=M_HEAD_DIM, valid=M_HEAD_DIM, seqs=1)
        tail_k = k_x[seq - WINDOW:].reshape(WINDOW, N_KV_HEADS, 2, LANES)[:, :, 0, :HEAD_DIM]
        tail_v = v_x[seq - WINDOW:].reshape(WINDOW, N_KV_HEADS, 2, LANES)[:, :, 0, :HEAD_DIM]
        pk.append(tail_k[None])
        pv.append(tail_v[None])
        pc.append(c_p)
        pn.append(n_p[:, :, 0, :])
        pm.append(m_p[:, :, 0, 0])
        xp = _finish(xp, att, mls, lw, alpha)

        xs_pad = jnp.pad(xs, ((0, 0), (0, SAMPLE_PAD - ds), (0, 0))).reshape(db * SAMPLE_PAD, D_MODEL)
        q_a, k_x, v_x, q_m, k_m, v_m, o_m, gates = _project(xs_pad, w_main, w_gate)
        q_rows = q_a.reshape(db, SAMPLE_PAD, 4, LANES).transpose(0, 2, 1, 3).reshape(db, 4 * SAMPLE_PAD, LANES)
        att_rows = _attn_sample(q_rows, k_x.reshape(db, SAMPLE_PAD, 512), v_x.reshape(db, SAMPLE_PAD, 512),
                                cache_k_win[l].reshape(db, WINDOW, LANES).astype(F32),
                                cache_v_win[l].reshape(db, WINDOW, LANES).astype(F32),
                                bias_sc, bias_sn, sinks_s)
        att = att_rows.reshape(db, 4, SAMPLE_PAD, LANES).transpose(0, 2, 1, 3)[:, :ds].reshape(db * ds, ATT_WIDTH)
        c0 = state_C[l].astype(F32)
        n0 = state_n[l].astype(F32)[:, :, None, :]
        m0 = jnp.broadcast_to(state_m[l].astype(F32)[:, :, None, None], (db, N_HEADS_M, 1, LANES))
        mls, c_s, n_s, m_s = _mlstm(q_m, k_m, v_m, o_m, gates, gb_row, nw_row, c0, n0, m0,
                                    rows=SAMPLE_PAD, valid=ds, seqs=SAMPLE_SEQS)
        mls = mls.reshape(db, SAMPLE_PAD, MLSTM_WIDTH)[:, :ds].reshape(db * ds, MLSTM_WIDTH)
        k_new = k_x.reshape(db, SAMPLE_PAD, N_KV_HEADS, 2, LANES)[:, :ds, :, 0, :HEAD_DIM]
        v_new = v_x.reshape(db, SAMPLE_PAD, N_KV_HEADS, 2, LANES)[:, :ds, :, 0, :HEAD_DIM]
        sk.append(jnp.concatenate([cache_k_win[l].astype(F32), k_new], axis=1)[:, -WINDOW:])
        sv.append(jnp.concatenate([cache_v_win[l].astype(F32), v_new], axis=1)[:, -WINDOW:])
        sc.append(c_s)
        sn.append(n_s[:, :, 0, :])
        sm.append(m_s[:, :, 0, 0])
        xs = _finish(xs.reshape(db * ds, D_MODEL), att, mls, lw, alpha).reshape(db, ds, D_MODEL)

    return (xp[None], xs, jnp.stack(pk), jnp.stack(pv), jnp.stack(pc), jnp.stack(pn), jnp.stack(pm),
            jnp.stack(sk), jnp.stack(sv), jnp.stack(sc), jnp.stack(sn), jnp.stack(sm))
```

```python
import functools
import math

import numpy as np
import jax
import jax.numpy as jnp
from jax import lax
from jax.experimental import pallas as pl
from jax.experimental.pallas import tpu as pltpu

F32 = jnp.float32
BF16 = jnp.bfloat16

D_MODEL = 1024
HEAD_DIM = 64
N_HEADS_ATT = 8
N_KV_HEADS = 2
GQA_GROUP = N_HEADS_ATT // N_KV_HEADS
WINDOW = 128
NUM_BUCKETS = 32
MAX_DISTANCE = 128
N_HEADS_M = 4
M_HEAD_DIM = 128
MLSTM_WIDTH = N_HEADS_M * M_HEAD_DIM
ATT_WIDTH = N_HEADS_ATT * HEAD_DIM
PEER_HEADS = 8
N_KEYS = 128
PEER_TOPK = 16
PEER_HALF = 128
LN_EPS = 1e-5
NEG_INF = -1e30
PAST_LEN = 16384

LANES = 128
SUBLANES = 8
VMEM_LIMIT = 56 * 1024 * 1024

TOKEN_TILE = 512
EXPERT_BLOCK = 1024
SAMPLE_PAD = 8


def _cparams(sem, flags=None):
    return pltpu.CompilerParams(dimension_semantics=sem, vmem_limit_bytes=VMEM_LIMIT, flags=flags)


def _dot(a, b):
    return jnp.dot(a, b, preferred_element_type=F32)


def _dot_nt(a, b):
    return lax.dot_general(a, b, (((1,), (1,)), ((), ())), preferred_element_type=F32)


def _dot_tn(a, b):
    return lax.dot_general(a, b, (((0,), (0,)), ((), ())), preferred_element_type=F32)


def _layer_norm(x, g, b):
    mu = jnp.mean(x, axis=-1, keepdims=True)
    xc = x - mu
    var = jnp.mean(xc * xc, axis=-1, keepdims=True)
    return xc * lax.rsqrt(var + LN_EPS) * g + b


def _t5_bucket_np(dist):
    n = np.maximum(dist, 0)
    exact = NUM_BUCKETS // 2
    nf = np.maximum(n, 1).astype(np.float64)
    val = np.log(nf / exact) / math.log(MAX_DISTANCE / exact) * (NUM_BUCKETS - exact)
    frac = np.abs(val - np.round(val))
    assert not np.any((frac < 1e-6) & (n > exact) & (n != MAX_DISTANCE)), "bucket boundary is rounding sensitive"
    large = np.minimum(exact + np.floor(val + 1e-9).astype(np.int64), NUM_BUCKETS - 1)
    return np.where(n < exact, n, large).astype(np.int32)


def _bias_index_table(dist, valid):
    return np.where(valid, _t5_bucket_np(dist), -1).astype(np.int32)


def _bias_table_kernel(idx_ref, rb_ref, out_ref):
    idx = idx_ref[...]
    for h in range(N_HEADS_ATT):
        acc = jnp.full(idx.shape, NEG_INF, F32)
        for b in range(NUM_BUCKETS):
            acc = jnp.where(idx == b, rb_ref[b, h], acc)
        out_ref[h] = acc


def _bias_table(idx_np, rel_bias):
    r, c = idx_np.shape
    return pl.pallas_call(
        _bias_table_kernel,
        out_shape=jax.ShapeDtypeStruct((N_HEADS_ATT, r, c), F32),
        in_specs=[pl.BlockSpec(memory_space=pltpu.VMEM), pl.BlockSpec(memory_space=pltpu.SMEM)],
        out_specs=pl.BlockSpec(memory_space=pltpu.VMEM),
        name="bias_table",
    )(jnp.asarray(idx_np), rel_bias.astype(F32))


def _proj_kernel(x_ref, w_ref, wg_ref, q_ref, k_ref, v_ref, qm_ref, km_ref, vm_ref, om_ref, g_ref):
    x = x_ref[...]
    xb = x.astype(BF16)
    outs = (q_ref, k_ref, v_ref, qm_ref, km_ref, vm_ref, om_ref)
    for n, o_ref in enumerate(outs):
        z = _dot(xb, w_ref[:, n * 512:(n + 1) * 512])
        if o_ref is km_ref:
            z = z * (M_HEAD_DIM ** -0.5)
        o_ref[...] = z
    g_ref[...] = jnp.dot(x, wg_ref[...], precision=lax.Precision.HIGHEST, preferred_element_type=F32)


def _project(x, w_main, w_gate):
    t = x.shape[0]
    tm = TOKEN_TILE
    wide = jax.ShapeDtypeStruct((t, 512), F32)
    row = lambda i: (i, 0)
    fixed = lambda i: (0, 0)
    return pl.pallas_call(
        _proj_kernel,
        grid=(t // tm,),
        in_specs=[pl.BlockSpec((tm, D_MODEL), row),
                  pl.BlockSpec(w_main.shape, fixed),
                  pl.BlockSpec(w_gate.shape, fixed)],
        out_specs=[pl.BlockSpec((tm, 512), row)] * 7 + [pl.BlockSpec((tm, LANES), row)],
        out_shape=[wide] * 7 + [jax.ShapeDtypeStruct((t, LANES), F32)],
        compiler_params=_cparams(("parallel",)),
        name="in_proj",
    )(x, w_main, w_gate)


def _projection_weights(w_in):
    q_a = w_in[:, 0:512]
    k_a = w_in[:, 512:640]
    v_a = w_in[:, 640:768]
    rest = w_in[:, 768:768 + 4 * MLSTM_WIDTH]
    gates = w_in[:, 768 + 4 * MLSTM_WIDTH:]
    zero = jnp.zeros((D_MODEL, HEAD_DIM), w_in.dtype)

    def lo_hi(w):
        cols = []
        for kv in range(N_KV_HEADS):
            wk = w[:, kv * HEAD_DIM:(kv + 1) * HEAD_DIM]
            cols += [wk, zero, zero, wk]
        return jnp.concatenate(cols, axis=1)

    w_main = jnp.concatenate([q_a, lo_hi(k_a), lo_hi(v_a), rest], axis=1).astype(BF16)
    w_gate = jnp.pad(gates, ((0, 0), (0, LANES - gates.shape[1]))).astype(F32)
    return w_main, w_gate


def _softmax_pieces(pieces, sink):
    mx = sink
    for s in pieces:
        mx = jnp.maximum(mx, jnp.max(s, axis=-1, keepdims=True))
    es = [jnp.exp(s - mx) for s in pieces]
    den = jnp.exp(sink - mx)
    for e in es:
        den = den + jnp.sum(e, axis=-1, keepdims=True)
    inv = 1.0 / den
    return [e * inv for e in es]


def _attn_prompt_kernel(q_ref, kp_ref, kc_ref, vp_ref, vc_ref, bias_ref, sink_ref, o_ref):
    n = pl.program_id(0)
    scale = HEAD_DIM ** -0.5
    nh = N_HEADS_ATT
    bmm = functools.partial(jnp.einsum, preferred_element_type=F32)

    def per_head(ref, col_of):
        x = ref[...].astype(BF16)
        return jnp.stack([x[:, col_of(h) * LANES:(col_of(h) + 1) * LANES] for h in range(nh)], axis=0)

    kv_col = lambda h: (h // GQA_GROUP) * 2 + h % 2
    q = per_head(q_ref, lambda h: h // 2)
    kp = per_head(kp_ref, kv_col)
    kc = per_head(kc_ref, kv_col)
    vp = per_head(vp_ref, kv_col)
    vc = per_head(vc_ref, kv_col)
    bias = bias_ref[...]
    sink = sink_ref[...][:, :, 0:1]
    s_prev = bmm("hqd,hkd->hqk", q, kp) * scale + bias[:, :, :WINDOW]
    s_prev = jnp.where(n > 0, s_prev, NEG_INF)
    s_cur = bmm("hqd,hkd->hqk", q, kc) * scale + bias[:, :, WINDOW:]
    (p,) = _softmax_pieces([jnp.maximum(s_prev, s_cur)], sink)
    rows_i = lax.broadcasted_iota(jnp.int32, (WINDOW, WINDOW), 0)
    keys_i = lax.broadcasted_iota(jnp.int32, (WINDOW, WINDOW), 1)
    from_prev = keys_i > rows_i
    out = (bmm("hqk,hkd->hqd", jnp.where(from_prev, p, 0.0).astype(BF16), vp)
           + bmm("hqk,hkd->hqd", jnp.where(from_prev, 0.0, p).astype(BF16), vc))
    for pair in range(nh // 2):
        o_ref[:, pair * LANES:(pair + 1) * LANES] = out[2 * pair] + out[2 * pair + 1]


def _attn_prompt(q, kx, vx, bias, sinks):
    s = q.shape[0]
    nb = s // WINDOW
    cur = lambda n: (n, 0)
    prev = lambda n: (jnp.maximum(n - 1, 0), 0)
    blk = (WINDOW, 512)
    return pl.pallas_call(
        _attn_prompt_kernel,
        grid=(nb,),
        in_specs=[pl.BlockSpec(blk, cur), pl.BlockSpec(blk, prev), pl.BlockSpec(blk, cur),
                  pl.BlockSpec(blk, prev), pl.BlockSpec(blk, cur),
                  pl.BlockSpec(bias.shape, lambda n: (0, 0, 0)),
                  pl.BlockSpec(sinks.shape, lambda n: (0, 0, 0))],
        out_specs=pl.BlockSpec(blk, cur),
        out_shape=jax.ShapeDtypeStruct((s, ATT_WIDTH), F32),
        compiler_params=_cparams(("parallel",)),
        name="attn_prompt",
    )(q, kx, kx, vx, vx, bias, sinks)


SAMPLE_SEQS = 8


def _attn_sample_kernel(q_ref, kn_ref, vn_ref, ck_ref, cv_ref, bc_ref, bn_ref, sink_ref, o_ref):
    scale = HEAD_DIM ** -0.5
    lane = lax.broadcasted_iota(jnp.int32, (WINDOW, LANES), 1)
    low = lane < HEAD_DIM
    pad = jnp.zeros((WINDOW - SAMPLE_PAD, LANES), F32)
    rows = 2 * SAMPLE_PAD
    ns = q_ref.shape[0]
    qk_t = lambda a, b: jnp.einsum("bqd,bkd->bqk", a, b, preferred_element_type=F32)
    pv = lambda a, b: jnp.einsum("bqk,bkd->bqd", a, b, preferred_element_type=F32)
    q_all = q_ref[...].astype(BF16)
    kn = kn_ref[...]
    vn = vn_ref[...]
    ck = ck_ref[...]
    cv = cv_ref[...]
    ck_sw = pltpu.roll(ck, HEAD_DIM, 2)
    cv_sw = pltpu.roll(cv, HEAD_DIM, 2)
    pad = jnp.zeros((ns, WINDOW - SAMPLE_PAD, LANES), F32)
    for kv in range(N_KV_HEADS):
        qk = q_all[:, kv * rows:(kv + 1) * rows]
        acc = jnp.zeros((ns, rows, LANES), F32)
        for x in range(2):
            col = (kv * 2 + x) * LANES
            keep = low if x == 0 else jnp.logical_not(low)
            src_k, src_v = (ck, cv) if kv == x else (ck_sw, cv_sw)
            kc = jnp.where(keep, src_k, 0.0).astype(BF16)
            vc = jnp.where(keep, src_v, 0.0).astype(BF16)
            knx = jnp.concatenate([kn[:, :, col:col + LANES], pad], axis=1).astype(BF16)
            vnx = jnp.concatenate([vn[:, :, col:col + LANES], pad], axis=1).astype(BF16)
            s_c = qk_t(qk, kc) * scale + bc_ref[kv, x]
            s_n = qk_t(qk, knx) * scale + bn_ref[kv, x]
            p_c, p_n = _softmax_pieces([s_c, s_n], sink_ref[kv, x][:, 0:1])
            acc = acc + pv(p_c.astype(BF16), vc) + pv(p_n.astype(BF16), vnx)
        o_ref[:, kv * rows:(kv + 1) * rows, :] = acc


def _attn_sample(q, knew, vnew, cache_k, cache_v, bias_c, bias_n, sinks):
    db = q.shape[0]
    ns = SAMPLE_SEQS
    assert db % ns == 0
    b3 = lambda b: (b, 0, 0)
    f4 = lambda b: (0, 0, 0, 0)
    return pl.pallas_call(
        _attn_sample_kernel,
        grid=(db // ns,),
        in_specs=[pl.BlockSpec((ns,) + q.shape[1:], b3),
                  pl.BlockSpec((ns,) + knew.shape[1:], b3),
                  pl.BlockSpec((ns,) + vnew.shape[1:], b3),
                  pl.BlockSpec((ns, WINDOW, LANES), b3),
                  pl.BlockSpec((ns, WINDOW, LANES), b3),
                  pl.BlockSpec(bias_c.shape, f4),
                  pl.BlockSpec(bias_n.shape, f4),
                  pl.BlockSpec(sinks.shape, f4)],
        out_specs=pl.BlockSpec((ns,) + q.shape[1:], b3),
        out_shape=jax.ShapeDtypeStruct(q.shape, F32),
        compiler_params=_cparams(("parallel",)),
        name="attn_sample",
    )(q, knew, vnew, cache_k, cache_v, bias_c, bias_n, sinks)


def _log_sigmoid(x):
    return jnp.minimum(x, 0.0) - jnp.log1p(jnp.exp(-jnp.abs(x)))


def _cumsum_rows(x):
    axis = x.ndim - 2
    rows = x.shape[axis]
    idx = lax.broadcasted_iota(jnp.int32, x.shape, axis)
    k = 1
    while k < rows:
        x = x + jnp.where(idx >= k, pltpu.roll(x, k, axis), 0.0)
        k *= 2
    return x


def _mlstm_kernel(q_ref, k_ref, v_ref, o_ref, g_ref, gb_ref, nw_ref, c0_ref, n0_ref, m0_ref,
                  h_ref, c_out_ref, n_out_ref, m_out_ref, c_scr, n_scr, m_scr, *, rows, valid):
    c_idx = pl.program_id(1)

    @pl.when(c_idx == 0)
    def _():
        c_scr[...] = c0_ref[...]
        n_scr[...] = n0_ref[...]
        m_scr[...] = m0_ref[...]

    ns = c_scr.shape[0]
    nh = N_HEADS_M
    keys = M_HEAD_DIM
    bmm = functools.partial(jnp.einsum, preferred_element_type=F32)

    def heads(ref):
        x = ref[...].reshape(ns, rows, nh * M_HEAD_DIM)
        return jnp.concatenate([x[:, :, h * M_HEAD_DIM:(h + 1) * M_HEAD_DIM] for h in range(nh)], axis=0)

    def pad_rows(x):
        if rows == keys:
            return x
        return jnp.concatenate([x, jnp.zeros((x.shape[0], keys - rows, x.shape[2]), x.dtype)], axis=1)

    g = g_ref[...].reshape(ns, rows, LANES) + gb_ref[...]
    ridx = lax.broadcasted_iota(jnp.int32, g.shape, 1)
    ig_all = g
    lf_all = _log_sigmoid(g)
    if valid < rows:
        ig_all = jnp.where(ridx < valid, ig_all, NEG_INF)
        lf_all = jnp.where(ridx < valid, lf_all, 0.0)
    b_all = _cumsum_rows(lf_all)
    bc = jnp.concatenate([b_all[:, :, nh + h:nh + h + 1] for h in range(nh)], axis=0)
    igc = jnp.concatenate([ig_all[:, :, h:h + 1] for h in range(nh)], axis=0)

    t_idx = lax.broadcasted_iota(jnp.int32, (rows, keys), 0)
    s_idx = lax.broadcasted_iota(jnp.int32, (rows, keys), 1)
    eye = t_idx == s_idx
    causal = s_idx <= t_idx

    q = heads(q_ref)
    k = heads(k_ref)
    v = heads(v_ref)
    qb = q.astype(BF16)
    kb = pad_rows(k).astype(BF16)
    vb = pad_rows(v).astype(BF16)
    brow = jnp.sum(jnp.where(eye, bc, 0.0), axis=1, keepdims=True)
    igrow = jnp.sum(jnp.where(eye, igc, 0.0), axis=1, keepdims=True)
    logd = jnp.where(causal, bc - brow + igrow, -jnp.inf)
    c_old = jnp.concatenate([c_scr[:, h] for h in range(nh)], axis=0)
    n_old = jnp.concatenate([n_scr[:, h] for h in range(nh)], axis=0)
    m_prev = jnp.concatenate([m_scr[:, h] for h in range(nh)], axis=0)[:, :, 0:1]
    inter = bc + m_prev
    mt = jnp.maximum(inter, jnp.max(logd, axis=-1, keepdims=True))
    w_inter = jnp.exp(inter - mt)
    a = jnp.exp(logd - mt) * bmm("bqd,bkd->bqk", qb, kb)
    num = bmm("bqk,bkd->bqd", a.astype(BF16), vb) + w_inter * bmm("bqd,bvd->bqv", qb, c_old.astype(BF16))
    den = jnp.sum(a, axis=-1, keepdims=True) + w_inter * jnp.sum(q * n_old, axis=-1, keepdims=True)
    hid = num / jnp.maximum(jnp.abs(den), jnp.exp(-mt))
    mu = jnp.mean(hid, axis=-1, keepdims=True)
    hc = hid - mu
    var = jnp.mean(hc * hc, axis=-1, keepdims=True)
    hn = hc * lax.rsqrt(var + LN_EPS)
    for h in range(nh):
        sl = slice(h * M_HEAD_DIM, (h + 1) * M_HEAD_DIM)
        out = hn[h * ns:(h + 1) * ns].reshape(ns * rows, M_HEAD_DIM) * nw_ref[:, sl]
        h_ref[:, sl] = out * jax.nn.sigmoid(o_ref[:, sl])
    m_new = mt[:, rows - 1:rows]
    w_c = jnp.exp(inter[:, rows - 1:rows] - m_new)
    w_s = jnp.exp(bc[:, rows - 1:rows] - bc + igc - m_new)
    vs_t = jnp.swapaxes(pad_rows(v * w_s), 1, 2).astype(BF16)
    c_new = w_c * c_old + bmm("bvs,bsd->bvd", vs_t, kb)
    n_new = w_c * n_old + jnp.sum(w_s * k, axis=1, keepdims=True)
    m_row = jnp.broadcast_to(m_new, (nh * ns, 1, LANES))
    for h in range(nh):
        c_scr[:, h] = c_new[h * ns:(h + 1) * ns]
        n_scr[:, h] = n_new[h * ns:(h + 1) * ns]
        m_scr[:, h] = m_row[h * ns:(h + 1) * ns]

    @pl.when(c_idx == pl.num_programs(1) - 1)
    def _():
        c_out_ref[...] = c_scr[...]
        n_out_ref[...] = n_scr[...]
        m_out_ref[...] = m_scr[...]


def _mlstm(q, k, v, o, gates, gate_bias_row, norm_w_row, c0, n0, m0, *, rows, valid, seqs):
    batch = c0.shape[0]
    total = q.shape[0]
    nc = total // (batch * rows)
    assert batch % seqs == 0 and (seqs == 1 or nc == 1)
    seq = lambda b, c: (b * nc + c, 0)
    fixed = lambda b, c: (0, 0)
    st4 = lambda b, c: (b, 0, 0, 0)
    wide = pl.BlockSpec((seqs * rows, MLSTM_WIDTH), seq)
    c_spec = pl.BlockSpec((seqs, N_HEADS_M, M_HEAD_DIM, M_HEAD_DIM), st4)
    v_spec = pl.BlockSpec((seqs, N_HEADS_M, 1, LANES), st4)
    return pl.pallas_call(
        functools.partial(_mlstm_kernel, rows=rows, valid=valid),
        grid=(batch // seqs, nc),
        in_specs=[wide, wide, wide, wide,
                  pl.BlockSpec((seqs * rows, LANES), seq),
                  pl.BlockSpec((1, LANES), fixed),
                  pl.BlockSpec((1, MLSTM_WIDTH), fixed),
                  c_spec, v_spec, v_spec],
        out_specs=[wide, c_spec, v_spec, v_spec],
        out_shape=[jax.ShapeDtypeStruct((total, MLSTM_WIDTH), F32),
                   jax.ShapeDtypeStruct(c0.shape, F32),
                   jax.ShapeDtypeStruct(n0.shape, F32),
                   jax.ShapeDtypeStruct(m0.shape, F32)],
        scratch_shapes=[pltpu.VMEM((seqs, N_HEADS_M, M_HEAD_DIM, M_HEAD_DIM), F32),
                        pltpu.VMEM((seqs, N_HEADS_M, 1, LANES), F32),
                        pltpu.VMEM((seqs, N_HEADS_M, 1, LANES), F32)],
        compiler_params=_cparams(("parallel", "arbitrary")),
        name="mlstm",
    )(q, k, v, o, gates, gate_bias_row, norm_w_row, c0, n0, m0)


def _sorting_network(n):
    pairs = []

    def merge(lo, m, r):
        step = r * 2
        if step < m:
            merge(lo, m, step)
            merge(lo + r, m, step)
            pairs.extend((i, i + r) for i in range(lo + r, lo + m - r, step))
        else:
            pairs.append((lo, lo + r))

    def sort(lo, m):
        if m > 1:
            sort(lo, m // 2)
            sort(lo + m // 2, m // 2)
            merge(lo, m, 1)

    sort(0, n)
    return pairs


_SORT_PAIRS = _sorting_network(N_KEYS // SUBLANES)
TOP_ROWS = 24


def _top_values(x, count):
    v = [x[k * SUBLANES:(k + 1) * SUBLANES] for k in range(x.shape[0] // SUBLANES)]
    for i, j in _SORT_PAIRS:
        v[i], v[j] = jnp.maximum(v[i], v[j]), jnp.minimum(v[i], v[j])
    tops = []
    for r in range(count):
        m = jnp.max(v[0], axis=0, keepdims=True)
        tops.append(m)
        left = count - 1 - r
        if left:
            took = v[0] == m
            nxt = v[1:] + [jnp.full_like(v[0], -jnp.inf)]
            v = [jnp.where(took, nxt[k], v[k]) for k in range(min(len(v), left))]
    return tops


def _rows(vals):
    return jnp.concatenate(vals, axis=0)


def _mid_kernel(att_ref, mls_ref, x_ref, woa_ref, wom_ref, g1_ref, b1_ref, wqt_ref, keys_ref,
                h_ref, rank_ref, b_ref, a_ref, l_ref, s_scr, top_scr, *, alpha):
    tm = x_ref.shape[0]
    nch = tm // LANES
    mix = _dot(att_ref[...].astype(BF16), woa_ref[...]) + _dot(mls_ref[...].astype(BF16), wom_ref[...])
    hid = _layer_norm(alpha * x_ref[...] + mix, g1_ref[...], b1_ref[...])
    h_ref[...] = hid
    q_t = _dot_nt(wqt_ref[...], hid.astype(BF16))
    for hc in range(2 * PEER_HEADS):
        s_t = _dot(keys_ref[hc], q_t[hc * PEER_HALF:(hc + 1) * PEER_HALF].astype(BF16))
        for tc in range(nch):
            s_scr[hc, tc] = s_t[:, tc * LANES:(tc + 1) * LANES]

    n_items = PEER_HEADS * nch
    pad = [jnp.full((TOP_ROWS - PEER_TOPK - 1, LANES), -jnp.inf, F32)]

    def find_tops(item, slot):
        h = item // nch
        tc = item % nch
        for c in range(2):
            top_scr[slot, c] = _rows(_top_values(s_scr[2 * h + c, tc], PEER_TOPK + 1) + pad)

    find_tops(0, 0)

    def route_body(it, carry):
        find_tops(jnp.minimum(it + 1, n_items - 1), (it + 1) % 2)
        h = it // nch
        tc = it % nch
        s1 = s_scr[2 * h, tc]
        s2 = s_scr[2 * h + 1, tc]
        tops1 = top_scr[it % 2, 0]
        tops2 = top_scr[it % 2, 1]
        t1 = [tops1[r:r + 1] for r in range(PEER_TOPK + 1)]
        t2 = [tops2[r:r + 1] for r in range(PEER_TOPK + 1)]
        wide = [tops1[:SUBLANES] + t2[r] for r in range(PEER_TOPK + 1)]
        tall = [t1[SUBLANES + k] + t2[0] for k in range(PEER_TOPK + 1 - SUBLANES)]
        cmax = t1[0] + t2[0]
        z = jnp.zeros_like(cmax)
        tau = below = cmax
        for r in range(PEER_TOPK + 1):
            head = jnp.maximum(jnp.max(wide[0], axis=0, keepdims=True), tall[0])
            if r < PEER_TOPK:
                tau = head
                z = z + jnp.exp(head - cmax)
            else:
                below = head
            left = PEER_TOPK - r
            if left:
                took_w = wide[0] == head
                took_t = tall[0] == head
                nxt_w = wide[1:] + [jnp.full_like(wide[0], -jnp.inf)]
                nxt_t = tall[1:] + [jnp.full_like(tall[0], -jnp.inf)]
                wide = [jnp.where(took_w, nxt_w[k], wide[k]) for k in range(min(len(wide), left))]
                tall = [jnp.where(took_t, nxt_t[k], tall[k]) for k in range(min(len(tall), left))]
        cut = 0.5 * (tau + below)
        rank = jnp.zeros_like(s2)
        count = jnp.zeros_like(s1)
        for r in range(PEER_TOPK):
            rank = jnp.where(t2[r] > s2, (r + 1) * RANK_SCALE, rank)
            count = jnp.where(s1 >= cut - t2[r], (r + 1) * RANK_SCALE, count)
        rank_ref[h, tc] = rank.astype(BF16)
        l_ref[h, tc] = count
        b_ref[h, tc] = jnp.exp(s2 - t2[0]).astype(BF16)
        a_ref[h, tc] = jnp.exp(s1 - (t1[0] + jnp.log(z)))
        return carry

    lax.fori_loop(0, n_items, route_body, 0)


def _mid(att, mls, x, wo_a, wo_m, g1, b1, wq_t, keys, alpha):
    t = x.shape[0]
    tm = TOKEN_TILE
    nch = tm // LANES
    row = lambda i: (i, 0)
    fixed2 = lambda i: (0, 0)
    fixed3 = lambda i: (0, 0, 0)
    tile4 = lambda i: (0, i, 0, 0)
    s_shape = lambda dt: jax.ShapeDtypeStruct((PEER_HEADS, t // LANES, N_KEYS, LANES), dt)
    s_spec = pl.BlockSpec((PEER_HEADS, nch, N_KEYS, LANES), tile4)
    return pl.pallas_call(
        functools.partial(_mid_kernel, alpha=alpha),
        grid=(t // tm,),
        in_specs=[pl.BlockSpec((tm, ATT_WIDTH), row), pl.BlockSpec((tm, MLSTM_WIDTH), row),
                  pl.BlockSpec((tm, D_MODEL), row),
                  pl.BlockSpec(wo_a.shape, fixed2), pl.BlockSpec(wo_m.shape, fixed2),
                  pl.BlockSpec(g1.shape, fixed2), pl.BlockSpec(b1.shape, fixed2),
                  pl.BlockSpec(wq_t.shape, fixed2), pl.BlockSpec(keys.shape, fixed3)],
        out_specs=[pl.BlockSpec((tm, D_MODEL), row), s_spec, s_spec, s_spec, s_spec],
        out_shape=[jax.ShapeDtypeStruct((t, D_MODEL), F32), s_shape(BF16), s_shape(BF16), s_shape(F32), s_shape(F32)],
        scratch_shapes=[pltpu.VMEM((2 * PEER_HEADS, nch, N_KEYS, LANES), F32),
                        pltpu.VMEM((2, 2, TOP_ROWS, LANES), F32)],
        compiler_params=_cparams(("parallel",)),
        name="out_proj_router",
    )(att, mls, x, wo_a, wo_m, g1, b1, wq_t, keys)


def _gelu(x):
    return 0.5 * x * (1.0 + lax.erf(x * (2.0 ** -0.5)))


ACT_TOKENS = 256
ACT_EXPERTS = (512, 512)
assert sum(ACT_EXPERTS) == EXPERT_BLOCK


RANK_SCALE = 256.0


def _row_tile(row):
    packed = jnp.broadcast_to(row, (2 * SUBLANES, row.shape[1])).astype(BF16)
    reps = N_KEYS // (2 * SUBLANES)
    return jnp.broadcast_to(packed[None], (reps,) + packed.shape).reshape(N_KEYS, row.shape[1])


def _zero_after(x):
    bits = lax.bitcast_convert_type(x, jnp.uint32)
    bits = lax.shift_right_logical(lax.shift_right_logical(bits, jnp.uint32(16)), jnp.uint32(16))
    return lax.bitcast_convert_type(bits, F32)


def _peer_block(u_ref, vt_ref, a_ref, l_ref, rank_ref, b_ref, ht_scr, w_new, w_old, acc_scr):
    tm = ht_scr.shape[1]
    n_tp = tm // ACT_TOKENS
    starts = [sum(ACT_EXPERTS[:g]) for g in range(len(ACT_EXPERTS))]
    tiles = [(r0, nr, tp) for r0, nr in zip(starts, ACT_EXPERTS) for tp in range(n_tp)]
    slice_at = {(sl + 1) * len(tiles) // (n_tp + 1): sl for sl in range(n_tp)}
    for il in range(EXPERT_BLOCK // N_KEYS):
        for tc in range(tm // LANES):
            gate = jnp.zeros((N_KEYS, LANES), BF16)
            for h in range(PEER_HEADS):
                room = jnp.maximum(_row_tile(l_ref[h, tc, il:il + 1, :]) - rank_ref[h, tc], 0)
                gate = gate + jnp.minimum(_row_tile(a_ref[h, tc, il:il + 1, :]) * b_ref[h, tc], room)
            w_new[il * N_KEYS:(il + 1) * N_KEYS, tc * LANES:(tc + 1) * LANES] = gate
    for k, (r0, nr, tp) in enumerate(tiles):
        zero = None
        if k in slice_at:
            cols = slice(slice_at[k] * ACT_TOKENS, (slice_at[k] + 1) * ACT_TOKENS)
            upd = acc_scr[:, cols] + _dot(vt_ref[...], w_old[:, cols])
            acc_scr[:, cols] = upd
            zero = _zero_after(upd[0:1, 0:LANES])
        act = _dot(u_ref[r0:r0 + nr, :], ht_scr[:, tp * ACT_TOKENS:(tp + 1) * ACT_TOKENS])
        for d in range(nr // N_KEYS):
            il = r0 // N_KEYS + d
            for half in range(ACT_TOKENS // LANES):
                tc = tp * (ACT_TOKENS // LANES) + half
                blk = (slice(il * N_KEYS, (il + 1) * N_KEYS), slice(tc * LANES, (tc + 1) * LANES))
                piece = _gelu(act[d * N_KEYS:(d + 1) * N_KEYS, half * LANES:(half + 1) * LANES]).astype(BF16)
                if zero is not None and d == 0 and half == 0:
                    piece = piece + _row_tile(zero)
                w_new[blk] = w_new[blk] * piece


def _peer_kernel(h_ref, rank_ref, b_ref, a0_ref, l0_ref, a1_ref, l1_ref, u0_ref, u1_ref,
                 vtp_ref, vt0_ref, vtl_ref, g2_ref, b2_ref, y_ref, ht_scr, w0, w1, acc_scr, *, alpha):
    s = pl.program_id(1)

    @pl.when(s == 0)
    def _():
        ht_scr[...] = h_ref[...].T.astype(BF16)
        acc_scr[...] = jnp.zeros_like(acc_scr)
        w1[...] = jnp.zeros_like(w1)

    _peer_block(u0_ref, vtp_ref, a0_ref, l0_ref, rank_ref, b_ref, ht_scr, w0, w1, acc_scr)
    _peer_block(u1_ref, vt0_ref, a1_ref, l1_ref, rank_ref, b_ref, ht_scr, w1, w0, acc_scr)

    @pl.when(s == pl.num_programs(1) - 1)
    def _():
        f = (acc_scr[...] + _dot(vtl_ref[...], w1[...])).T
        y_ref[...] = _layer_norm(alpha * h_ref[...] + f, g2_ref[...], b2_ref[...])


def _peer(hid, rank, b, a, cnt, u_b, vt_b, g2, b2, alpha):
    t = hid.shape[0]
    tm = TOKEN_TILE
    nch = tm // LANES
    nblk = u_b.shape[0] // EXPERT_BLOCK
    rows = EXPERT_BLOCK // N_KEYS
    row = lambda i, s: (i, 0)
    fixed2 = lambda i, s: (0, 0)
    full_spec = pl.BlockSpec((PEER_HEADS, nch, N_KEYS, LANES), lambda i, s: (0, i, 0, 0))
    rows_spec = lambda f: pl.BlockSpec((PEER_HEADS, nch, rows, LANES), lambda i, s: (0, i, f(s), 0))
    u_spec = lambda f: pl.BlockSpec((EXPERT_BLOCK, D_MODEL), lambda i, s: (f(s), 0))
    vt_spec = lambda f: pl.BlockSpec((D_MODEL, EXPERT_BLOCK), lambda i, s: (0, f(s)))
    even = lambda s: 2 * s
    odd = lambda s: 2 * s + 1
    return pl.pallas_call(
        functools.partial(_peer_kernel, alpha=alpha),
        grid=(t // tm, nblk // 2),
        in_specs=[pl.BlockSpec((tm, D_MODEL), row), full_spec, full_spec,
                  rows_spec(even), rows_spec(even), rows_spec(odd), rows_spec(odd),
                  u_spec(even), u_spec(odd),
                  vt_spec(lambda s: jnp.maximum(2 * s - 1, 0)), vt_spec(even), vt_spec(lambda s: nblk - 1),
                  pl.BlockSpec(g2.shape, fixed2), pl.BlockSpec(b2.shape, fixed2)],
        out_specs=pl.BlockSpec((tm, D_MODEL), row),
        out_shape=jax.ShapeDtypeStruct((t, D_MODEL), F32),
        scratch_shapes=[pltpu.VMEM((D_MODEL, tm), BF16),
                        pltpu.VMEM((EXPERT_BLOCK, tm), BF16),
                        pltpu.VMEM((EXPERT_BLOCK, tm), BF16),
                        pltpu.VMEM((D_MODEL, tm), F32)],
        compiler_params=_cparams(("parallel", "arbitrary")),
        name="peer_experts",
    )(hid, rank, b, a, cnt, a, cnt, u_b, u_b, vt_b, vt_b, vt_b, g2, b2)


def _finish(x, att, mls, lw, alpha):
    hid, rank, b, a, cnt = _mid(att, mls, x, lw["wo_a"], lw["wo_m"], lw["g1"], lw["b1"], lw["wq_t"], lw["keys"], alpha)
    return _peer(hid, rank, b, a, cnt, lw["u"], lw["v_t"], lw["g2"], lw["b2"], alpha)


def _prompt_bias_index():
    q = np.arange(WINDOW)[:, None]
    kband = np.arange(2 * WINDOW)[None, :]
    dist = q + WINDOW - kband
    return _bias_index_table(dist, (dist >= 0) & (dist < WINDOW))


def _sample_bias_index(steps):
    q = np.arange(SAMPLE_PAD)[:, None]
    c = np.arange(WINDOW)[None, :]
    dist_c = WINDOW - c + q
    idx_c = _bias_index_table(dist_c, (dist_c >= 0) & (dist_c < WINDOW))
    jn = np.arange(LANES)[None, :]
    dist_n = q - jn
    idx_n = _bias_index_table(dist_n, (dist_n >= 0) & (dist_n < WINDOW) & (jn < steps))
    return np.concatenate([idx_c, idx_n], axis=1)


def _per_row_head_layout(tab):
    _, steps, nk = tab.shape
    t = tab.reshape(N_KV_HEADS, 2, 2, steps, nk)
    return t.transpose(0, 2, 1, 3, 4).reshape(N_KV_HEADS, 2, 2 * steps, nk)


def kernel(x_prompt, x_sample, cache_k_win, cache_v_win, state_C, state_n, state_m, w_in, gate_bias,
           attn_sinks, rel_bias, mlstm_norm_w, w_out, ln1_g, ln1_b, peer_wq, peer_keys, peer_u, peer_v,
           ln2_g, ln2_b):
    depth = w_in.shape[0]
    alpha = (2 * depth) ** 0.25
    bsz, seq, _ = x_prompt.shape
    db, ds, _ = x_sample.shape
    assert bsz == 1 and seq % TOKEN_TILE == 0 and ds <= SAMPLE_PAD and (db * ds) % TOKEN_TILE == 0
    assert PAST_LEN >= WINDOW

    bias_p = _bias_table(_prompt_bias_index(), rel_bias)
    bias_s = _bias_table(_sample_bias_index(ds), rel_bias)
    bias_sc = _per_row_head_layout(bias_s[:, :, :WINDOW])
    bias_sn = _per_row_head_layout(bias_s[:, :, WINDOW:])

    xp = x_prompt[0]
    xs = x_sample
    pk, pv, pc, pn, pm = [], [], [], [], []
    sk, sv, sc, sn, sm = [], [], [], [], []
    for l in range(depth):
        w_main, w_gate = _projection_weights(w_in[l])
        gb_row = jnp.pad(gate_bias[l], (0, LANES - 2 * N_HEADS_M))[None, :].astype(F32)
        nw_row = mlstm_norm_w[l].reshape(1, MLSTM_WIDTH).astype(F32)
        sinks = attn_sinks[l].astype(F32)
        sinks_p = jnp.broadcast_to(sinks[:, None, None], (N_HEADS_ATT, 1, LANES))
        sinks_s = _per_row_head_layout(
            jnp.broadcast_to(sinks[:, None, None], (N_HEADS_ATT, SAMPLE_PAD, LANES)))
        lw = dict(
            wo_a=w_out[l, :ATT_WIDTH].astype(BF16), wo_m=w_out[l, ATT_WIDTH:].astype(BF16),
            g1=ln1_g[l][None, :], b1=ln1_b[l][None, :],
            wq_t=peer_wq[l].T.astype(BF16),
            keys=peer_keys[l].reshape(2 * PEER_HEADS, N_KEYS, PEER_HALF).astype(BF16),
            u=peer_u[l].astype(BF16), v_t=peer_v[l].T.astype(BF16),
            g2=ln2_g[l][None, :], b2=ln2_b[l][None, :])

        q_a, k_x, v_x, q_m, k_m, v_m, o_m, gates = _project(xp, w_main, w_gate)
        att = _attn_prompt(q_a, k_x, v_x, bias_p, sinks_p)
        zeros_c = jnp.zeros((1, N_HEADS_M, M_HEAD_DIM, M_HEAD_DIM), F32)
        zeros_v = jnp.zeros((1, N_HEADS_M, 1, LANES), F32)
        mls, c_p, n_p, m_p = _mlstm(q_m, k_m, v_m, o_m, gates, gb_row, nw_row, zeros_c, zeros_v, zeros_v,
                                    rows=M_HEAD_DIM, valid=M_HEAD_DIM, seqs=1)
        tail_k = k_x[seq - WINDOW:].reshape(WINDOW, N_KV_HEADS, 2, LANES)[:, :, 0, :HEAD_DIM]
        tail_v = v_x[seq - WINDOW:].reshape(WINDOW, N_KV_HEADS, 2, LANES)[:, :, 0, :HEAD_DIM]
        pk.append(tail_k[None])
        pv.append(tail_v[None])
        pc.append(c_p)
        pn.append(n_p[:, :, 0, :])
        pm.append(m_p[:, :, 0, 0])
        xp = _finish(xp, att, mls, lw, alpha)

        xs_pad = jnp.pad(xs, ((0, 0), (0, SAMPLE_PAD - ds), (0, 0))).reshape(db * SAMPLE_PAD, D_MODEL)
        q_a, k_x, v_x, q_m, k_m, v_m, o_m, gates = _project(xs_pad, w_main, w_gate)
        q_rows = q_a.reshape(db, SAMPLE_PAD, 4, LANES).transpose(0, 2, 1, 3).reshape(db, 4 * SAMPLE_PAD, LANES)
        att_rows = _attn_sample(q_rows, k_x.reshape(db, SAMPLE_PAD, 512), v_x.reshape(db, SAMPLE_PAD, 512),
                                cache_k_win[l].reshape(db, WINDOW, LANES).astype(F32),
                                cache_v_win[l].reshape(db, WINDOW, LANES).astype(F32),
                                bias_sc, bias_sn, sinks_s)
        att = att_rows.reshape(db, 4, SAMPLE_PAD, LANES).transpose(0, 2, 1, 3)[:, :ds].reshape(db * ds, ATT_WIDTH)
        c0 = state_C[l].astype(F32)
        n0 = state_n[l].astype(F32)[:, :, None, :]
        m0 = jnp.broadcast_to(state_m[l].astype(F32)[:, :, None, None], (db, N_HEADS_M, 1, LANES))
        mls, c_s, n_s, m_s = _mlstm(q_m, k_m, v_m, o_m, gates, gb_row, nw_row, c0, n0, m0,
                                    rows=SAMPLE_PAD, valid=ds, seqs=SAMPLE_SEQS)
        mls = mls.reshape(db, SAMPLE_PAD, MLSTM_WIDTH)[:, :ds].reshape(db * ds, MLSTM_WIDTH)
        k_new = k_x.reshape(db, SAMPLE_PAD, N_KV_HEADS, 2, LANES)[:, :ds, :, 0, :HEAD_DIM]
        v_new = v_x.reshape(db, SAMPLE_PAD, N_KV_HEADS, 2, LANES)[:, :ds, :, 0, :HEAD_DIM]
        sk.append(jnp.concatenate([cache_k_win[l].astype(F32), k_new], axis=1)[:, -WINDOW:])
        sv.append(jnp.concatenate([cache_v_win[l].astype(F32), v_new], axis=1)[:, -WINDOW:])
        sc.append(c_s)
        sn.append(n_s[:, :, 0, :])
        sm.append(m_s[:, :, 0, 0])
        xs = _finish(xs.reshape(db * ds, D_MODEL), att, mls, lw, alpha).reshape(db, ds, D_MODEL)

    return (xp[None], xs, jnp.stack(pk), jnp.stack(pv), jnp.stack(pc), jnp.stack(pn), jnp.stack(pm),
            jnp.stack(sk), jnp.stack(sv), jnp.stack(sc), jnp.stack(sn), jnp.stack(sm))
```

```python
import functools
import math

import numpy as np
import jax
import jax.numpy as jnp
from jax import lax
from jax.experimental import pallas as pl
from jax.experimental.pallas import tpu as pltpu

F32 = jnp.float32
BF16 = jnp.bfloat16

D_MODEL = 1024
HEAD_DIM = 64
N_HEADS_ATT = 8
N_KV_HEADS = 2
GQA_GROUP = N_HEADS_ATT // N_KV_HEADS
WINDOW = 128
NUM_BUCKETS = 32
MAX_DISTANCE = 128
N_HEADS_M = 4
M_HEAD_DIM = 128
MLSTM_WIDTH = N_HEADS_M * M_HEAD_DIM
ATT_WIDTH = N_HEADS_ATT * HEAD_DIM
PEER_HEADS = 8
N_KEYS = 128
PEER_TOPK = 16
PEER_HALF = 128
LN_EPS = 1e-5
NEG_INF = -1e30
PAST_LEN = 16384

LANES = 128
SUBLANES = 8
VMEM_LIMIT = 56 * 1024 * 1024

TOKEN_TILE = 512
EXPERT_BLOCK = 1024
SAMPLE_PAD = 8


def _cparams(sem, flags=None):
    return pltpu.CompilerParams(dimension_semantics=sem, vmem_limit_bytes=VMEM_LIMIT, flags=flags)


def _dot(a, b):
    return jnp.dot(a, b, preferred_element_type=F32)


def _dot_nt(a, b):
    return lax.dot_general(a, b, (((1,), (1,)), ((), ())), preferred_element_type=F32)


def _dot_tn(a, b):
    return lax.dot_general(a, b, (((0,), (0,)), ((), ())), preferred_element_type=F32)


def _layer_norm(x, g, b):
    mu = jnp.mean(x, axis=-1, keepdims=True)
    xc = x - mu
    var = jnp.mean(xc * xc, axis=-1, keepdims=True)
    return xc * lax.rsqrt(var + LN_EPS) * g + b


def _t5_bucket_np(dist):
    n = np.maximum(dist, 0)
    exact = NUM_BUCKETS // 2
    nf = np.maximum(n, 1).astype(np.float64)
    val = np.log(nf / exact) / math.log(MAX_DISTANCE / exact) * (NUM_BUCKETS - exact)
    frac = np.abs(val - np.round(val))
    assert not np.any((frac < 1e-6) & (n > exact) & (n != MAX_DISTANCE)), "bucket boundary is rounding sensitive"
    large = np.minimum(exact + np.floor(val + 1e-9).astype(np.int64), NUM_BUCKETS - 1)
    return np.where(n < exact, n, large).astype(np.int32)


def _bias_index_table(dist, valid):
    return np.where(valid, _t5_bucket_np(dist), -1).astype(np.int32)


def _bias_table_kernel(idx_ref, rb_ref, out_ref):
    idx = idx_ref[...]
    for h in range(N_HEADS_ATT):
        acc = jnp.full(idx.shape, NEG_INF, F32)
        for b in range(NUM_BUCKETS):
            acc = jnp.where(idx == b, rb_ref[b, h], acc)
        out_ref[h] = acc


def _bias_table(idx_np, rel_bias):
    r, c = idx_np.shape
    return pl.pallas_call(
        _bias_table_kernel,
        out_shape=jax.ShapeDtypeStruct((N_HEADS_ATT, r, c), F32),
        in_specs=[pl.BlockSpec(memory_space=pltpu.VMEM), pl.BlockSpec(memory_space=pltpu.SMEM)],
        out_specs=pl.BlockSpec(memory_space=pltpu.VMEM),
        name="bias_table",
    )(jnp.asarray(idx_np), rel_bias.astype(F32))


def _proj_kernel(x_ref, w_ref, wg_ref, q_ref, k_ref, v_ref, qm_ref, km_ref, vm_ref, om_ref, g_ref):
    x = x_ref[...]
    xb = x.astype(BF16)
    outs = (q_ref, k_ref, v_ref, qm_ref, km_ref, vm_ref, om_ref)
    for n, o_ref in enumerate(outs):
        z = _dot(xb, w_ref[:, n * 512:(n + 1) * 512])
        if o_ref is km_ref:
            z = z * (M_HEAD_DIM ** -0.5)
        o_ref[...] = z
    g_ref[...] = jnp.dot(x, wg_ref[...], precision=lax.Precision.HIGHEST, preferred_element_type=F32)


def _project(x, w_main, w_gate):
    t = x.shape[0]
    tm = TOKEN_TILE
    wide = jax.ShapeDtypeStruct((t, 512), F32)
    row = lambda i: (i, 0)
    fixed = lambda i: (0, 0)
    return pl.pallas_call(
        _proj_kernel,
        grid=(t // tm,),
        in_specs=[pl.BlockSpec((tm, D_MODEL), row),
                  pl.BlockSpec(w_main.shape, fixed),
                  pl.BlockSpec(w_gate.shape, fixed)],
        out_specs=[pl.BlockSpec((tm, 512), row)] * 7 + [pl.BlockSpec((tm, LANES), row)],
        out_shape=[wide] * 7 + [jax.ShapeDtypeStruct((t, LANES), F32)],
        compiler_params=_cparams(("parallel",)),
        name="in_proj",
    )(x, w_main, w_gate)


def _projection_weights(w_in):
    q_a = w_in[:, 0:512]
    k_a = w_in[:, 512:640]
    v_a = w_in[:, 640:768]
    rest = w_in[:, 768:768 + 4 * MLSTM_WIDTH]
    gates = w_in[:, 768 + 4 * MLSTM_WIDTH:]
    zero = jnp.zeros((D_MODEL, HEAD_DIM), w_in.dtype)

    def lo_hi(w):
        cols = []
        for kv in range(N_KV_HEADS):
            wk = w[:, kv * HEAD_DIM:(kv + 1) * HEAD_DIM]
            cols += [wk, zero, zero, wk]
        return jnp.concatenate(cols, axis=1)

    w_main = jnp.concatenate([q_a, lo_hi(k_a), lo_hi(v_a), rest], axis=1).astype(BF16)
    w_gate = jnp.pad(gates, ((0, 0), (0, LANES - gates.shape[1]))).astype(F32)
    return w_main, w_gate


def _softmax_pieces(pieces, sink):
    mx = sink
    for s in pieces:
        mx = jnp.maximum(mx, jnp.max(s, axis=-1, keepdims=True))
    es = [jnp.exp(s - mx) for s in pieces]
    den = jnp.exp(sink - mx)
    for e in es:
        den = den + jnp.sum(e, axis=-1, keepdims=True)
    inv = 1.0 / den
    return [e * inv for e in es]


def _attn_prompt_kernel(q_ref, kp_ref, kc_ref, vp_ref, vc_ref, bias_ref, sink_ref, o_ref):
    n = pl.program_id(0)
    scale = HEAD_DIM ** -0.5
    nh = N_HEADS_ATT
    nq = q_ref.shape[0] // WINDOW
    bmm = functools.partial(jnp.einsum, preferred_element_type=F32)

    def per_head(blocks, col_of):
        return jnp.stack([x[:, col_of(h) * LANES:(col_of(h) + 1) * LANES] for x in blocks for h in range(nh)], axis=0)

    def blocks_of(ref):
        x = ref[...].astype(BF16)
        return [x[j * WINDOW:(j + 1) * WINDOW] for j in range(x.shape[0] // WINDOW)]

    kv_col = lambda h: (h // GQA_GROUP) * 2 + h % 2
    k_cur, v_cur = blocks_of(kc_ref), blocks_of(vc_ref)
    k_prev = blocks_of(kp_ref) + k_cur[:-1]
    v_prev = blocks_of(vp_ref) + v_cur[:-1]
    q = per_head(blocks_of(q_ref), lambda h: h // 2)
    kp = per_head(k_prev, kv_col)
    kc = per_head(k_cur, kv_col)
    vp = per_head(v_prev, kv_col)
    vc = per_head(v_cur, kv_col)
    bias = jnp.concatenate([bias_ref[...]] * nq, axis=0)
    sink = jnp.concatenate([sink_ref[...][:, :, 0:1]] * nq, axis=0)
    s_prev = bmm("hqd,hkd->hqk", q, kp) * scale + bias[:, :, :WINDOW]
    first = lax.broadcasted_iota(jnp.int32, (nq * nh, 1, 1), 0) < jnp.where(n == 0, nh, 0)
    s_prev = jnp.where(first, NEG_INF, s_prev)
    s_cur = bmm("hqd,hkd->hqk", q, kc) * scale + bias[:, :, WINDOW:]
    (p,) = _softmax_pieces([jnp.maximum(s_prev, s_cur)], sink)
    rows_i = lax.broadcasted_iota(jnp.int32, (WINDOW, WINDOW), 0)
    keys_i = lax.broadcasted_iota(jnp.int32, (WINDOW, WINDOW), 1)
    from_prev = keys_i > rows_i
    out = (bmm("hqk,hkd->hqd", jnp.where(from_prev, p, 0.0).astype(BF16), vp)
           + bmm("hqk,hkd->hqd", jnp.where(from_prev, 0.0, p).astype(BF16), vc))
    for j in range(nq):
        for pair in range(nh // 2):
            o_ref[j * WINDOW:(j + 1) * WINDOW, pair * LANES:(pair + 1) * LANES] = (
                out[j * nh + 2 * pair] + out[j * nh + 2 * pair + 1])


ATT_QBLOCKS = 2


def _attn_prompt(q, kx, vx, bias, sinks):
    s = q.shape[0]
    nq = ATT_QBLOCKS
    nb = s // (nq * WINDOW)
    cur = lambda n: (n, 0)
    prev = lambda n: (jnp.maximum(nq * n - 1, 0), 0)
    blk = (nq * WINDOW, 512)
    one = (WINDOW, 512)
    return pl.pallas_call(
        _attn_prompt_kernel,
        grid=(nb,),
        in_specs=[pl.BlockSpec(blk, cur), pl.BlockSpec(one, prev), pl.BlockSpec(blk, cur),
                  pl.BlockSpec(one, prev), pl.BlockSpec(blk, cur),
                  pl.BlockSpec(bias.shape, lambda n: (0, 0, 0)),
                  pl.BlockSpec(sinks.shape, lambda n: (0, 0, 0))],
        out_specs=pl.BlockSpec(blk, cur),
        out_shape=jax.ShapeDtypeStruct((s, ATT_WIDTH), F32),
        compiler_params=_cparams(("parallel",)),
        name="attn_prompt",
    )(q, kx, kx, vx, vx, bias, sinks)


SAMPLE_SEQS = 16


def _attn_sample_kernel(q_ref, kn_ref, vn_ref, ck_ref, cv_ref, bc_ref, bn_ref, sink_ref, o_ref):
    scale = HEAD_DIM ** -0.5
    lane = lax.broadcasted_iota(jnp.int32, (WINDOW, LANES), 1)
    low = lane < HEAD_DIM
    pad = jnp.zeros((WINDOW - SAMPLE_PAD, LANES), F32)
    rows = 2 * SAMPLE_PAD
    ns = q_ref.shape[0]
    qk_t = lambda a, b: jnp.einsum("bqd,bkd->bqk", a, b, preferred_element_type=F32)
    pv = lambda a, b: jnp.einsum("bqk,bkd->bqd", a, b, preferred_element_type=F32)
    q_all = q_ref[...].astype(BF16)
    kn = kn_ref[...]
    vn = vn_ref[...]
    ck = ck_ref[...]
    cv = cv_ref[...]
    ck_sw = pltpu.roll(ck, HEAD_DIM, 2)
    cv_sw = pltpu.roll(cv, HEAD_DIM, 2)
    pad = jnp.zeros((ns, WINDOW - SAMPLE_PAD, LANES), F32)
    for kv in range(N_KV_HEADS):
        qk = q_all[:, kv * rows:(kv + 1) * rows]
        acc = jnp.zeros((ns, rows, LANES), F32)
        for x in range(2):
            col = (kv * 2 + x) * LANES
            keep = low if x == 0 else jnp.logical_not(low)
            src_k, src_v = (ck, cv) if kv == x else (ck_sw, cv_sw)
            kc = jnp.where(keep, src_k, 0.0).astype(BF16)
            vc = jnp.where(keep, src_v, 0.0).astype(BF16)
            knx = jnp.concatenate([kn[:, :, col:col + LANES], pad], axis=1).astype(BF16)
            vnx = jnp.concatenate([vn[:, :, col:col + LANES], pad], axis=1).astype(BF16)
            s_c = qk_t(qk, kc) * scale + bc_ref[kv, x]
            s_n = qk_t(qk, knx) * scale + bn_ref[kv, x]
            p_c, p_n = _softmax_pieces([s_c, s_n], sink_ref[kv, x][:, 0:1])
            acc = acc + pv(p_c.astype(BF16), vc) + pv(p_n.astype(BF16), vnx)
        o_ref[:, kv * rows:(kv + 1) * rows, :] = acc


def _attn_sample(q, knew, vnew, cache_k, cache_v, bias_c, bias_n, sinks):
    db = q.shape[0]
    ns = SAMPLE_SEQS
    assert db % ns == 0
    b3 = lambda b: (b, 0, 0)
    f4 = lambda b: (0, 0, 0, 0)
    return pl.pallas_call(
        _attn_sample_kernel,
        grid=(db // ns,),
        in_specs=[pl.BlockSpec((ns,) + q.shape[1:], b3),
                  pl.BlockSpec((ns,) + knew.shape[1:], b3),
                  pl.BlockSpec((ns,) + vnew.shape[1:], b3),
                  pl.BlockSpec((ns, WINDOW, LANES), b3),
                  pl.BlockSpec((ns, WINDOW, LANES), b3),
                  pl.BlockSpec(bias_c.shape, f4),
                  pl.BlockSpec(bias_n.shape, f4),
                  pl.BlockSpec(sinks.shape, f4)],
        out_specs=pl.BlockSpec((ns,) + q.shape[1:], b3),
        out_shape=jax.ShapeDtypeStruct(q.shape, F32),
        compiler_params=_cparams(("parallel",)),
        name="attn_sample",
    )(q, knew, vnew, cache_k, cache_v, bias_c, bias_n, sinks)


def _log_sigmoid(x):
    return jnp.minimum(x, 0.0) - jnp.log1p(jnp.exp(-jnp.abs(x)))


def _cumsum_rows(x):
    axis = x.ndim - 2
    rows = x.shape[axis]
    idx = lax.broadcasted_iota(jnp.int32, x.shape, axis)
    k = 1
    while k < rows:
        x = x + jnp.where(idx >= k, pltpu.roll(x, k, axis), 0.0)
        k *= 2
    return x


def _mlstm_kernel(q_ref, k_ref, v_ref, o_ref, g_ref, gb_ref, nw_ref, c0_ref, n0_ref, m0_ref,
                  h_ref, c_out_ref, n_out_ref, m_out_ref, c_scr, n_scr, m_scr, *, rows, valid):
    c_idx = pl.program_id(1)

    @pl.when(c_idx == 0)
    def _():
        c_scr[...] = c0_ref[...]
        n_scr[...] = n0_ref[...]
        m_scr[...] = m0_ref[...]

    ns = c_scr.shape[0]
    nh = N_HEADS_M
    keys = M_HEAD_DIM
    bmm = functools.partial(jnp.einsum, preferred_element_type=F32)

    def heads(ref):
        x = ref[...].reshape(ns, rows, nh * M_HEAD_DIM)
        return jnp.concatenate([x[:, :, h * M_HEAD_DIM:(h + 1) * M_HEAD_DIM] for h in range(nh)], axis=0)

    def pad_rows(x):
        if rows == keys:
            return x
        return jnp.concatenate([x, jnp.zeros((x.shape[0], keys - rows, x.shape[2]), x.dtype)], axis=1)

    g = g_ref[...].reshape(ns, rows, LANES) + gb_ref[...]
    ridx = lax.broadcasted_iota(jnp.int32, g.shape, 1)
    ig_all = g
    lf_all = _log_sigmoid(g)
    if valid < rows:
        ig_all = jnp.where(ridx < valid, ig_all, NEG_INF)
        lf_all = jnp.where(ridx < valid, lf_all, 0.0)
    b_all = _cumsum_rows(lf_all)
    bc = jnp.concatenate([b_all[:, :, nh + h:nh + h + 1] for h in range(nh)], axis=0)
    igc = jnp.concatenate([ig_all[:, :, h:h + 1] for h in range(nh)], axis=0)

    t_idx = lax.broadcasted_iota(jnp.int32, (rows, keys), 0)
    s_idx = lax.broadcasted_iota(jnp.int32, (rows, keys), 1)
    eye = t_idx == s_idx
    causal = s_idx <= t_idx

    q = heads(q_ref)
    k = heads(k_ref)
    v = heads(v_ref)
    qb = q.astype(BF16)
    kb = pad_rows(k).astype(BF16)
    vb = pad_rows(v).astype(BF16)
    brow = jnp.sum(jnp.where(eye, bc, 0.0), axis=1, keepdims=True)
    igrow = jnp.sum(jnp.where(eye, igc, 0.0), axis=1, keepdims=True)
    logd = jnp.where(causal, bc - brow + igrow, -jnp.inf)
    c_old = jnp.concatenate([c_scr[:, h] for h in range(nh)], axis=0)
    n_old = jnp.concatenate([n_scr[:, h] for h in range(nh)], axis=0)
    m_prev = jnp.concatenate([m_scr[:, h] for h in range(nh)], axis=0)[:, :, 0:1]
    inter = bc + m_prev
    mt = jnp.maximum(inter, jnp.max(logd, axis=-1, keepdims=True))
    w_inter = jnp.exp(inter - mt)
    a = jnp.exp(logd - mt) * bmm("bqd,bkd->bqk", qb, kb)
    num = bmm("bqk,bkd->bqd", a.astype(BF16), vb) + w_inter * bmm("bqd,bvd->bqv", qb, c_old.astype(BF16))
    den = jnp.sum(a, axis=-1, keepdims=True) + w_inter * jnp.sum(q * n_old, axis=-1, keepdims=True)
    hid = num / jnp.maximum(jnp.abs(den), jnp.exp(-mt))
    mu = jnp.mean(hid, axis=-1, keepdims=True)
    hc = hid - mu
    var = jnp.mean(hc * hc, axis=-1, keepdims=True)
    hn = hc * lax.rsqrt(var + LN_EPS)
    for h in range(nh):
        sl = slice(h * M_HEAD_DIM, (h + 1) * M_HEAD_DIM)
        out = hn[h * ns:(h + 1) * ns].reshape(ns * rows, M_HEAD_DIM) * nw_ref[:, sl]
        h_ref[:, sl] = out * jax.nn.sigmoid(o_ref[:, sl])
    m_new = mt[:, rows - 1:rows]
    w_c = jnp.exp(inter[:, rows - 1:rows] - m_new)
    w_s = jnp.exp(bc[:, rows - 1:rows] - bc + igc - m_new)
    vs_t = jnp.swapaxes(pad_rows(v * w_s), 1, 2).astype(BF16)
    c_new = w_c * c_old + bmm("bvs,bsd->bvd", vs_t, kb)
    n_new = w_c * n_old + jnp.sum(w_s * k, axis=1, keepdims=True)
    m_row = jnp.broadcast_to(m_new, (nh * ns, 1, LANES))
    for h in range(nh):
        c_scr[:, h] = c_new[h * ns:(h + 1) * ns]
        n_scr[:, h] = n_new[h * ns:(h + 1) * ns]
        m_scr[:, h] = m_row[h * ns:(h + 1) * ns]

    @pl.when(c_idx == pl.num_programs(1) - 1)
    def _():
        c_out_ref[...] = c_scr[...]
        n_out_ref[...] = n_scr[...]
        m_out_ref[...] = m_scr[...]


def _mlstm(q, k, v, o, gates, gate_bias_row, norm_w_row, c0, n0, m0, *, rows, valid, seqs):
    batch = c0.shape[0]
    total = q.shape[0]
    nc = total // (batch * rows)
    assert batch % seqs == 0 and (seqs == 1 or nc == 1)
    seq = lambda b, c: (b * nc + c, 0)
    fixed = lambda b, c: (0, 0)
    st4 = lambda b, c: (b, 0, 0, 0)
    wide = pl.BlockSpec((seqs * rows, MLSTM_WIDTH), seq)
    c_spec = pl.BlockSpec((seqs, N_HEADS_M, M_HEAD_DIM, M_HEAD_DIM), st4)
    v_spec = pl.BlockSpec((seqs, N_HEADS_M, 1, LANES), st4)
    return pl.pallas_call(
        functools.partial(_mlstm_kernel, rows=rows, valid=valid),
        grid=(batch // seqs, nc),
        in_specs=[wide, wide, wide, wide,
                  pl.BlockSpec((seqs * rows, LANES), seq),
                  pl.BlockSpec((1, LANES), fixed),
                  pl.BlockSpec((1, MLSTM_WIDTH), fixed),
                  c_spec, v_spec, v_spec],
        out_specs=[wide, c_spec, v_spec, v_spec],
        out_shape=[jax.ShapeDtypeStruct((total, MLSTM_WIDTH), F32),
                   jax.ShapeDtypeStruct(c0.shape, F32),
                   jax.ShapeDtypeStruct(n0.shape, F32),
                   jax.ShapeDtypeStruct(m0.shape, F32)],
        scratch_shapes=[pltpu.VMEM((seqs, N_HEADS_M, M_HEAD_DIM, M_HEAD_DIM), F32),
                        pltpu.VMEM((seqs, N_HEADS_M, 1, LANES), F32),
                        pltpu.VMEM((seqs, N_HEADS_M, 1, LANES), F32)],
        compiler_params=_cparams(("parallel", "arbitrary")),
        name="mlstm",
    )(q, k, v, o, gates, gate_bias_row, norm_w_row, c0, n0, m0)


def _sorting_network(n):
    pairs = []

    def merge(lo, m, r):
        step = r * 2
        if step < m:
            merge(lo, m, step)
            merge(lo + r, m, step)
            pairs.extend((i, i + r) for i in range(lo + r, lo + m - r, step))
        else:
            pairs.append((lo, lo + r))

    def sort(lo, m):
        if m > 1:
            sort(lo, m // 2)
            sort(lo + m // 2, m // 2)
            merge(lo, m, 1)

    sort(0, n)
    return pairs


_SORT_PAIRS = _sorting_network(N_KEYS // SUBLANES)
TOP_ROWS = 24


def _top_values(x, count):
    v = [x[k * SUBLANES:(k + 1) * SUBLANES] for k in range(x.shape[0] // SUBLANES)]
    for i, j in _SORT_PAIRS:
        v[i], v[j] = jnp.maximum(v[i], v[j]), jnp.minimum(v[i], v[j])
    tops = []
    for r in range(count):
        m = jnp.max(v[0], axis=0, keepdims=True)
        tops.append(m)
        left = count - 1 - r
        if left:
            took = v[0] == m
            nxt = v[1:] + [jnp.full_like(v[0], -jnp.inf)]
            v = [jnp.where(took, nxt[k], v[k]) for k in range(min(len(v), left))]
    return tops


def _rows(vals):
    return jnp.concatenate(vals, axis=0)


def _mid_kernel(att_ref, mls_ref, x_ref, woa_ref, wom_ref, g1_ref, b1_ref, wqt_ref, keys_ref,
                h_ref, rank_ref, b_ref, a_ref, l_ref, s_scr, top_scr, *, alpha):
    tm = x_ref.shape[0]
    nch = tm // LANES
    mix = _dot(att_ref[...].astype(BF16), woa_ref[...]) + _dot(mls_ref[...].astype(BF16), wom_ref[...])
    hid = _layer_norm(alpha * x_ref[...] + mix, g1_ref[...], b1_ref[...])
    h_ref[...] = hid
    q_t = _dot_nt(wqt_ref[...], hid.astype(BF16))
    for hc in range(2 * PEER_HEADS):
        s_t = _dot(keys_ref[hc], q_t[hc * PEER_HALF:(hc + 1) * PEER_HALF].astype(BF16))
        for tc in range(nch):
            s_scr[hc, tc] = s_t[:, tc * LANES:(tc + 1) * LANES]

    n_items = PEER_HEADS * nch
    pad = [jnp.full((TOP_ROWS - PEER_TOPK - 1, LANES), -jnp.inf, F32)]

    def find_tops(item, slot):
        h = item // nch
        tc = item % nch
        for c in range(2):
            top_scr[slot, c] = _rows(_top_values(s_scr[2 * h + c, tc], PEER_TOPK + 1) + pad)

    find_tops(0, 0)

    def route_body(it, carry):
        find_tops(jnp.minimum(it + 1, n_items - 1), (it + 1) % 2)
        h = it // nch
        tc = it % nch
        s1 = s_scr[2 * h, tc]
        s2 = s_scr[2 * h + 1, tc]
        tops1 = top_scr[it % 2, 0]
        tops2 = top_scr[it % 2, 1]
        t1 = [tops1[r:r + 1] for r in range(PEER_TOPK + 1)]
        t2 = [tops2[r:r + 1] for r in range(PEER_TOPK + 1)]
        wide = [tops1[:SUBLANES] + t2[r] for r in range(PEER_TOPK + 1)]
        tall = [t1[SUBLANES + k] + t2[0] for k in range(PEER_TOPK + 1 - SUBLANES)]
        cmax = t1[0] + t2[0]
        z = jnp.zeros_like(cmax)
        tau = below = cmax
        for r in range(PEER_TOPK + 1):
            head = jnp.maximum(jnp.max(wide[0], axis=0, keepdims=True), tall[0])
            if r < PEER_TOPK:
                tau = head
                z = z + jnp.exp(head - cmax)
            else:
                below = head
            left = PEER_TOPK - r
            if left:
                took_w = wide[0] == head
                took_t = tall[0] == head
                nxt_w = wide[1:] + [jnp.full_like(wide[0], -jnp.inf)]
                nxt_t = tall[1:] + [jnp.full_like(tall[0], -jnp.inf)]
                wide = [jnp.where(took_w, nxt_w[k], wide[k]) for k in range(min(len(wide), left))]
                tall = [jnp.where(took_t, nxt_t[k], tall[k]) for k in range(min(len(tall), left))]
        cut = 0.5 * (tau + below)
        rank = jnp.zeros_like(s2)
        count = jnp.zeros_like(s1)
        for r in range(PEER_TOPK):
            rank = jnp.where(t2[r] > s2, (r + 1) * RANK_SCALE, rank)
            count = jnp.where(s1 >= cut - t2[r], (r + 1) * RANK_SCALE, count)
        rank_ref[h, tc] = rank.astype(BF16)
        l_ref[h, tc] = count
        b_ref[h, tc] = jnp.exp(s2 - t2[0]).astype(BF16)
        a_ref[h, tc] = jnp.exp(s1 - (t1[0] + jnp.log(z)))
        return carry

    lax.fori_loop(0, n_items, route_body, 0)


def _mid(att, mls, x, wo_a, wo_m, g1, b1, wq_t, keys, alpha):
    t = x.shape[0]
    tm = TOKEN_TILE
    nch = tm // LANES
    row = lambda i: (i, 0)
    fixed2 = lambda i: (0, 0)
    fixed3 = lambda i: (0, 0, 0)
    tile4 = lambda i: (0, i, 0, 0)
    s_shape = lambda dt: jax.ShapeDtypeStruct((PEER_HEADS, t // LANES, N_KEYS, LANES), dt)
    s_spec = pl.BlockSpec((PEER_HEADS, nch, N_KEYS, LANES), tile4)
    return pl.pallas_call(
        functools.partial(_mid_kernel, alpha=alpha),
        grid=(t // tm,),
        in_specs=[pl.BlockSpec((tm, ATT_WIDTH), row), pl.BlockSpec((tm, MLSTM_WIDTH), row),
                  pl.BlockSpec((tm, D_MODEL), row),
                  pl.BlockSpec(wo_a.shape, fixed2), pl.BlockSpec(wo_m.shape, fixed2),
                  pl.BlockSpec(g1.shape, fixed2), pl.BlockSpec(b1.shape, fixed2),
                  pl.BlockSpec(wq_t.shape, fixed2), pl.BlockSpec(keys.shape, fixed3)],
        out_specs=[pl.BlockSpec((tm, D_MODEL), row), s_spec, s_spec, s_spec, s_spec],
        out_shape=[jax.ShapeDtypeStruct((t, D_MODEL), F32), s_shape(BF16), s_shape(BF16), s_shape(F32), s_shape(F32)],
        scratch_shapes=[pltpu.VMEM((2 * PEER_HEADS, nch, N_KEYS, LANES), F32),
                        pltpu.VMEM((2, 2, TOP_ROWS, LANES), F32)],
        compiler_params=_cparams(("parallel",)),
        name="out_proj_router",
    )(att, mls, x, wo_a, wo_m, g1, b1, wq_t, keys)


def _gelu(x):
    return 0.5 * x * (1.0 + lax.erf(x * (2.0 ** -0.5)))


ACT_TOKENS = 256
ACT_EXPERTS = (512, 512)
assert sum(ACT_EXPERTS) == EXPERT_BLOCK


RANK_SCALE = 256.0


def _row_tile(row):
    packed = jnp.broadcast_to(row, (2 * SUBLANES, row.shape[1])).astype(BF16)
    reps = N_KEYS // (2 * SUBLANES)
    return jnp.broadcast_to(packed[None], (reps,) + packed.shape).reshape(N_KEYS, row.shape[1])


def _zero_after(x):
    bits = lax.bitcast_convert_type(x, jnp.uint32)
    bits = lax.shift_right_logical(lax.shift_right_logical(bits, jnp.uint32(16)), jnp.uint32(16))
    return lax.bitcast_convert_type(bits, F32)


def _peer_block(u_ref, vt_ref, a_ref, l_ref, rank_ref, b_ref, ht_scr, w_new, w_old, acc_scr):
    tm = ht_scr.shape[1]
    n_tp = tm // ACT_TOKENS
    starts = [sum(ACT_EXPERTS[:g]) for g in range(len(ACT_EXPERTS))]
    tiles = [(r0, nr, tp) for r0, nr in zip(starts, ACT_EXPERTS) for tp in range(n_tp)]
    slice_at = {(sl + 1) * len(tiles) // (n_tp + 1): sl for sl in range(n_tp)}
    for k, (r0, nr, tp) in enumerate(tiles):
        zero = None
        if k in slice_at:
            cols = slice(slice_at[k] * ACT_TOKENS, (slice_at[k] + 1) * ACT_TOKENS)
            upd = acc_scr[:, cols] + _dot(vt_ref[...], w_old[:, cols])
            acc_scr[:, cols] = upd
            zero = _zero_after(upd[0:1, 0:LANES])
        act = _dot(u_ref[r0:r0 + nr, :], ht_scr[:, tp * ACT_TOKENS:(tp + 1) * ACT_TOKENS])
        for d in range(nr // N_KEYS):
            il = r0 // N_KEYS + d
            for half in range(ACT_TOKENS // LANES):
                tc = tp * (ACT_TOKENS // LANES) + half
                gate = jnp.zeros((N_KEYS, LANES), BF16)
                if zero is not None and d == 0 and half == 0:
                    gate = gate + _row_tile(zero)
                for h in range(PEER_HEADS):
                    room = jnp.maximum(_row_tile(l_ref[h, tc, il:il + 1, :]) - rank_ref[h, tc], 0)
                    gate = gate + jnp.minimum(_row_tile(a_ref[h, tc, il:il + 1, :]) * b_ref[h, tc], room)
                piece = act[d * N_KEYS:(d + 1) * N_KEYS, half * LANES:(half + 1) * LANES]
                w_new[il * N_KEYS:(il + 1) * N_KEYS, tc * LANES:(tc + 1) * LANES] = gate * _gelu(piece).astype(BF16)


def _peer_kernel(h_ref, rank_ref, b_ref, a0_ref, l0_ref, a1_ref, l1_ref, u0_ref, u1_ref,
                 vtp_ref, vt0_ref, vtl_ref, g2_ref, b2_ref, y_ref, ht_scr, w0, w1, acc_scr, *, alpha):
    s = pl.program_id(1)

    @pl.when(s == 0)
    def _():
        ht_scr[...] = h_ref[...].T.astype(BF16)
        acc_scr[...] = jnp.zeros_like(acc_scr)
        w1[...] = jnp.zeros_like(w1)

    _peer_block(u0_ref, vtp_ref, a0_ref, l0_ref, rank_ref, b_ref, ht_scr, w0, w1, acc_scr)
    _peer_block(u1_ref, vt0_ref, a1_ref, l1_ref, rank_ref, b_ref, ht_scr, w1, w0, acc_scr)

    @pl.when(s == pl.num_programs(1) - 1)
    def _():
        f = (acc_scr[...] + _dot(vtl_ref[...], w1[...])).T
        y_ref[...] = _layer_norm(alpha * h_ref[...] + f, g2_ref[...], b2_ref[...])


def _peer(hid, rank, b, a, cnt, u_b, vt_b, g2, b2, alpha):
    t = hid.shape[0]
    tm = TOKEN_TILE
    nch = tm // LANES
    nblk = u_b.shape[0] // EXPERT_BLOCK
    rows = EXPERT_BLOCK // N_KEYS
    row = lambda i, s: (i, 0)
    fixed2 = lambda i, s: (0, 0)
    full_spec = pl.BlockSpec((PEER_HEADS, nch, N_KEYS, LANES), lambda i, s: (0, i, 0, 0))
    rows_spec = lambda f: pl.BlockSpec((PEER_HEADS, nch, rows, LANES), lambda i, s: (0, i, f(s), 0))
    u_spec = lambda f: pl.BlockSpec((EXPERT_BLOCK, D_MODEL), lambda i, s: (f(s), 0))
    vt_spec = lambda f: pl.BlockSpec((D_MODEL, EXPERT_BLOCK), lambda i, s: (0, f(s)))
    even = lambda s: 2 * s
    odd = lambda s: 2 * s + 1
    return pl.pallas_call(
        functools.partial(_peer_kernel, alpha=alpha),
        grid=(t // tm, nblk // 2),
        in_specs=[pl.BlockSpec((tm, D_MODEL), row), full_spec, full_spec,
                  rows_spec(even), rows_spec(even), rows_spec(odd), rows_spec(odd),
                  u_spec(even), u_spec(odd),
                  vt_spec(lambda s: jnp.maximum(2 * s - 1, 0)), vt_spec(even), vt_spec(lambda s: nblk - 1),
                  pl.BlockSpec(g2.shape, fixed2), pl.BlockSpec(b2.shape, fixed2)],
        out_specs=pl.BlockSpec((tm, D_MODEL), row),
        out_shape=jax.ShapeDtypeStruct((t, D_MODEL), F32),
        scratch_shapes=[pltpu.VMEM((D_MODEL, tm), BF16),
                        pltpu.VMEM((EXPERT_BLOCK, tm), BF16),
                        pltpu.VMEM((EXPERT_BLOCK, tm), BF16),
                        pltpu.VMEM((D_MODEL, tm), F32)],
        compiler_params=_cparams(("parallel", "arbitrary")),
        name="peer_experts",
    )(hid, rank, b, a, cnt, a, cnt, u_b, u_b, vt_b, vt_b, vt_b, g2, b2)


def _finish(x, att, mls, lw, alpha):
    hid, rank, b, a, cnt = _mid(att, mls, x, lw["wo_a"], lw["wo_m"], lw["g1"], lw["b1"], lw["wq_t"], lw["keys"], alpha)
    return _peer(hid, rank, b, a, cnt, lw["u"], lw["v_t"], lw["g2"], lw["b2"], alpha)


def _prompt_bias_index():
    q = np.arange(WINDOW)[:, None]
    kband = np.arange(2 * WINDOW)[None, :]
    dist = q + WINDOW - kband
    return _bias_index_table(dist, (dist >= 0) & (dist < WINDOW))


def _sample_bias_index(steps):
    q = np.arange(SAMPLE_PAD)[:, None]
    c = np.arange(WINDOW)[None, :]
    dist_c = WINDOW - c + q
    idx_c = _bias_index_table(dist_c, (dist_c >= 0) & (dist_c < WINDOW))
    jn = np.arange(LANES)[None, :]
    dist_n = q - jn
    idx_n = _bias_index_table(dist_n, (dist_n >= 0) & (dist_n < WINDOW) & (jn < steps))
    return np.concatenate([idx_c, idx_n], axis=1)


def _per_row_head_layout(tab):
    _, steps, nk = tab.shape
    t = tab.reshape(N_KV_HEADS, 2, 2, steps, nk)
    return t.transpose(0, 2, 1, 3, 4).reshape(N_KV_HEADS, 2, 2 * steps, nk)


def kernel(x_prompt, x_sample, cache_k_win, cache_v_win, state_C, state_n, state_m, w_in, gate_bias,
           attn_sinks, rel_bias, mlstm_norm_w, w_out, ln1_g, ln1_b, peer_wq, peer_keys, peer_u, peer_v,
           ln2_g, ln2_b):
    depth = w_in.shape[0]
    alpha = (2 * depth) ** 0.25
    bsz, seq, _ = x_prompt.shape
    db, ds, _ = x_sample.shape
    assert bsz == 1 and seq % TOKEN_TILE == 0 and ds <= SAMPLE_PAD and (db * ds) % TOKEN_TILE == 0
    assert PAST_LEN >= WINDOW

    bias_p = _bias_table(_prompt_bias_index(), rel_bias)
    bias_s = _bias_table(_sample_bias_index(ds), rel_bias)
    bias_sc = _per_row_head_layout(bias_s[:, :, :WINDOW])
    bias_sn = _per_row_head_layout(bias_s[:, :, WINDOW:])

    xp = x_prompt[0]
    xs = x_sample
    pk, pv, pc, pn, pm = [], [], [], [], []
    sk, sv, sc, sn, sm = [], [], [], [], []
    for l in range(depth):
        w_main, w_gate = _projection_weights(w_in[l])
        gb_row = jnp.pad(gate_bias[l], (0, LANES - 2 * N_HEADS_M))[None, :].astype(F32)
        nw_row = mlstm_norm_w[l].reshape(1, MLSTM_WIDTH).astype(F32)
        sinks = attn_sinks[l].astype(F32)
        sinks_p = jnp.broadcast_to(sinks[:, None, None], (N_HEADS_ATT, 1, LANES))
        sinks_s = _per_row_head_layout(
            jnp.broadcast_to(sinks[:, None, None], (N_HEADS_ATT, SAMPLE_PAD, LANES)))
        lw = dict(
            wo_a=w_out[l, :ATT_WIDTH].astype(BF16), wo_m=w_out[l, ATT_WIDTH:].astype(BF16),
            g1=ln1_g[l][None, :], b1=ln1_b[l][None, :],
            wq_t=peer_wq[l].T.astype(BF16),
            keys=peer_keys[l].reshape(2 * PEER_HEADS, N_KEYS, PEER_HALF).astype(BF16),
            u=peer_u[l].astype(BF16), v_t=peer_v[l].T.astype(BF16),
            g2=ln2_g[l][None, :], b2=ln2_b[l][None, :])

        q_a, k_x, v_x, q_m, k_m, v_m, o_m, gates = _project(xp, w_main, w_gate)
        att = _attn_prompt(q_a, k_x, v_x, bias_p, sinks_p)
        zeros_c = jnp.zeros((1, N_HEADS_M, M_HEAD_DIM, M_HEAD_DIM), F32)
        zeros_v = jnp.zeros((1, N_HEADS_M, 1, LANES), F32)
        mls, c_p, n_p, m_p = _mlstm(q_m, k_m, v_m, o_m, gates, gb_row, nw_row, zeros_c, zeros_v, zeros_v,
                                    rows=M_HEAD_DIM, valid=M_HEAD_DIM, seqs=1)
        tail_k = k_x[seq - WINDOW:].reshape(WINDOW, N_KV_HEADS, 2, LANES)[:, :, 0, :HEAD_DIM]
        tail_v = v_x[seq - WINDOW:].reshape(WINDOW, N_KV_HEADS, 2, LANES)[:, :, 0, :HEAD_DIM]
        pk.append(tail_k[None])
        pv.append(tail_v[None])
        pc.append(c_p)
        pn.append(n_p[:, :, 0, :])
        pm.append(m_p[:, :, 0, 0])
        xp = _finish(xp, att, mls, lw, alpha)

        xs_pad = jnp.pad(xs, ((0, 0), (0, SAMPLE_PAD - ds), (0, 0))).reshape(db * SAMPLE_PAD, D_MODEL)
        q_a, k_x, v_x, q_m, k_m, v_m, o_m, gates = _project(xs_pad, w_main, w_gate)
        q_rows = q_a.reshape(db, SAMPLE_PAD, 4, LANES).transpose(0, 2, 1, 3).reshape(db, 4 * SAMPLE_PAD, LANES)
        att_rows = _attn_sample(q_rows, k_x.reshape(db, SAMPLE_PAD, 512), v_x.reshape(db, SAMPLE_PAD, 512),
                                cache_k_win[l].reshape(db, WINDOW, LANES).astype(F32),
                                cache_v_win[l].reshape(db, WINDOW, LANES).astype(F32),
                                bias_sc, bias_sn, sinks_s)
        att = att_rows.reshape(db, 4, SAMPLE_PAD, LANES).transpose(0, 2, 1, 3)[:, :ds].reshape(db * ds, ATT_WIDTH)
        c0 = state_C[l].astype(F32)
        n0 = state_n[l].astype(F32)[:, :, None, :]
        m0 = jnp.broadcast_to(state_m[l].astype(F32)[:, :, None, None], (db, N_HEADS_M, 1, LANES))
        mls, c_s, n_s, m_s = _mlstm(q_m, k_m, v_m, o_m, gates, gb_row, nw_row, c0, n0, m0,
                                    rows=SAMPLE_PAD, valid=ds, seqs=SAMPLE_SEQS)
        mls = mls.reshape(db, SAMPLE_PAD, MLSTM_WIDTH)[:, :ds].reshape(db * ds, MLSTM_WIDTH)
        k_new = k_x.reshape(db, SAMPLE_PAD, N_KV_HEADS, 2, LANES)[:, :ds, :, 0, :HEAD_DIM]
        v_new = v_x.reshape(db, SAMPLE_PAD, N_KV_HEADS, 2, LANES)[:, :ds, :, 0, :HEAD_DIM]
        sk.append(jnp.concatenate([cache_k_win[l].astype(F32), k_new], axis=1)[:, -WINDOW:])
        sv.append(jnp.concatenate([cache_v_win[l].astype(F32), v_new], axis=1)[:, -WINDOW:])
        sc.append(c_s)
        sn.append(n_s[:, :, 0, :])
        sm.append(m_s[:, :, 0, 0])
        xs = _finish(xs.reshape(db * ds, D_MODEL), att, mls, lw, alpha).reshape(db, ds, D_MODEL)

    return (xp[None], xs, jnp.stack(pk), jnp.stack(pv), jnp.stack(pc), jnp.stack(pn), jnp.stack(pm),
            jnp.stack(sk), jnp.stack(sv), jnp.stack(sc), jnp.stack(sn), jnp.stack(sm))
```

```python
import functools
import math

import numpy as np
import jax
import jax.numpy as jnp
from jax import lax
from jax.experimental import pallas as pl
from jax.experimental.pallas import tpu as pltpu

F32 = jnp.float32
BF16 = jnp.bfloat16

D_MODEL = 1024
HEAD_DIM = 64
N_HEADS_ATT = 8
N_KV_HEADS = 2
GQA_GROUP = N_HEADS_ATT // N_KV_HEADS
WINDOW = 128
NUM_BUCKETS = 32
MAX_DISTANCE = 128
N_HEADS_M = 4
M_HEAD_DIM = 128
MLSTM_WIDTH = N_HEADS_M * M_HEAD_DIM
ATT_WIDTH = N_HEADS_ATT * HEAD_DIM
PEER_HEADS = 8
N_KEYS = 128
PEER_TOPK = 16
PEER_HALF = 128
LN_EPS = 1e-5
NEG_INF = -1e30
PAST_LEN = 16384

LANES = 128
SUBLANES = 8
VMEM_LIMIT = 56 * 1024 * 1024

TOKEN_TILE = 512
EXPERT_BLOCK = 1024
SAMPLE_PAD = 8


def _cparams(sem):
    return pltpu.CompilerParams(dimension_semantics=sem, vmem_limit_bytes=VMEM_LIMIT)


def _dot(a, b):
    return jnp.dot(a, b, preferred_element_type=F32)


def _dot_nt(a, b):
    return lax.dot_general(a, b, (((1,), (1,)), ((), ())), preferred_element_type=F32)


def _layer_norm(x, g, b):
    mu = jnp.mean(x, axis=-1, keepdims=True)
    xc = x - mu
    var = jnp.mean(xc * xc, axis=-1, keepdims=True)
    return xc * lax.rsqrt(var + LN_EPS) * g + b


def _t5_bucket_np(dist):
    n = np.maximum(dist, 0)
    exact = NUM_BUCKETS // 2
    nf = np.maximum(n, 1).astype(np.float64)
    val = np.log(nf / exact) / math.log(MAX_DISTANCE / exact) * (NUM_BUCKETS - exact)
    frac = np.abs(val - np.round(val))
    assert not np.any((frac < 1e-6) & (n > exact) & (n != MAX_DISTANCE)), "bucket boundary is rounding sensitive"
    large = np.minimum(exact + np.floor(val + 1e-9).astype(np.int64), NUM_BUCKETS - 1)
    return np.where(n < exact, n, large).astype(np.int32)


def _bias_index_table(dist, valid):
    return np.where(valid, _t5_bucket_np(dist), -1).astype(np.int32)


def _bias_table_kernel(idx_ref, rb_ref, out_ref):
    idx = idx_ref[...]
    for h in range(N_HEADS_ATT):
        acc = jnp.full(idx.shape, NEG_INF, F32)
        for b in range(NUM_BUCKETS):
            acc = jnp.where(idx == b, rb_ref[b, h], acc)
        out_ref[h] = acc


def _bias_table(idx_np, rel_bias):
    r, c = idx_np.shape
    return pl.pallas_call(
        _bias_table_kernel,
        out_shape=jax.ShapeDtypeStruct((N_HEADS_ATT, r, c), F32),
        in_specs=[pl.BlockSpec(memory_space=pltpu.VMEM), pl.BlockSpec(memory_space=pltpu.SMEM)],
        out_specs=pl.BlockSpec(memory_space=pltpu.VMEM),
        name="bias_table",
    )(jnp.asarray(idx_np), rel_bias.astype(F32))


def _proj_kernel(x_ref, w_ref, wg_ref, q_ref, k_ref, v_ref, qm_ref, km_ref, vm_ref, om_ref, g_ref):
    x = x_ref[...]
    xb = x.astype(BF16)
    outs = (q_ref, k_ref, v_ref, qm_ref, km_ref, vm_ref, om_ref)
    for n, o_ref in enumerate(outs):
        z = _dot(xb, w_ref[:, n * 512:(n + 1) * 512])
        if o_ref is km_ref:
            z = z * (M_HEAD_DIM ** -0.5)
        o_ref[...] = z
    g_ref[...] = jnp.dot(x, wg_ref[...], precision=lax.Precision.HIGHEST, preferred_element_type=F32)


def _project(x, w_main, w_gate):
    t = x.shape[0]
    tm = TOKEN_TILE
    wide = jax.ShapeDtypeStruct((t, 512), F32)
    row = lambda i: (i, 0)
    fixed = lambda i: (0, 0)
    return pl.pallas_call(
        _proj_kernel,
        grid=(t // tm,),
        in_specs=[pl.BlockSpec((tm, D_MODEL), row),
                  pl.BlockSpec(w_main.shape, fixed),
                  pl.BlockSpec(w_gate.shape, fixed)],
        out_specs=[pl.BlockSpec((tm, 512), row)] * 7 + [pl.BlockSpec((tm, LANES), row)],
        out_shape=[wide] * 7 + [jax.ShapeDtypeStruct((t, LANES), F32)],
        compiler_params=_cparams(("parallel",)),
        name="in_proj",
    )(x, w_main, w_gate)


def _projection_weights(w_in):
    q_a = w_in[:, 0:512]
    k_a = w_in[:, 512:640]
    v_a = w_in[:, 640:768]
    rest = w_in[:, 768:768 + 4 * MLSTM_WIDTH]
    gates = w_in[:, 768 + 4 * MLSTM_WIDTH:]
    zero = jnp.zeros((D_MODEL, HEAD_DIM), w_in.dtype)

    def lo_hi(w):
        cols = []
        for kv in range(N_KV_HEADS):
            wk = w[:, kv * HEAD_DIM:(kv + 1) * HEAD_DIM]
            cols += [wk, zero, zero, wk]
        return jnp.concatenate(cols, axis=1)

    w_main = jnp.concatenate([q_a, lo_hi(k_a), lo_hi(v_a), rest], axis=1).astype(BF16)
    w_gate = jnp.pad(gates, ((0, 0), (0, LANES - gates.shape[1]))).astype(F32)
    return w_main, w_gate


def _softmax_pieces(pieces, sink):
    mx = sink
    for s in pieces:
        mx = jnp.maximum(mx, jnp.max(s, axis=-1, keepdims=True))
    es = [jnp.exp(s - mx) for s in pieces]
    den = jnp.exp(sink - mx)
    for e in es:
        den = den + jnp.sum(e, axis=-1, keepdims=True)
    inv = 1.0 / den
    return [e * inv for e in es]


def _attn_prompt_kernel(q_ref, kp_ref, kc_ref, vp_ref, vc_ref, bias_ref, sink_ref, o_ref):
    n = pl.program_id(0)
    scale = HEAD_DIM ** -0.5
    nh = N_HEADS_ATT
    nq = q_ref.shape[0] // WINDOW
    bmm = functools.partial(jnp.einsum, preferred_element_type=F32)

    def per_head(blocks, col_of):
        return jnp.stack([x[:, col_of(h) * LANES:(col_of(h) + 1) * LANES] for x in blocks for h in range(nh)], axis=0)

    def blocks_of(ref):
        x = ref[...].astype(BF16)
        return [x[j * WINDOW:(j + 1) * WINDOW] for j in range(x.shape[0] // WINDOW)]

    kv_col = lambda h: (h // GQA_GROUP) * 2 + h % 2
    k_cur, v_cur = blocks_of(kc_ref), blocks_of(vc_ref)
    k_prev = blocks_of(kp_ref) + k_cur[:-1]
    v_prev = blocks_of(vp_ref) + v_cur[:-1]
    q = per_head(blocks_of(q_ref), lambda h: h // 2)
    kp = per_head(k_prev, kv_col)
    kc = per_head(k_cur, kv_col)
    vp = per_head(v_prev, kv_col)
    vc = per_head(v_cur, kv_col)
    bias = jnp.concatenate([bias_ref[...]] * nq, axis=0)
    sink = jnp.concatenate([sink_ref[...][:, :, 0:1]] * nq, axis=0)
    s_prev = bmm("hqd,hkd->hqk", q, kp) * scale + bias[:, :, :WINDOW]
    first = lax.broadcasted_iota(jnp.int32, (nq * nh, 1, 1), 0) < jnp.where(n == 0, nh, 0)
    s_prev = jnp.where(first, NEG_INF, s_prev)
    s_cur = bmm("hqd,hkd->hqk", q, kc) * scale + bias[:, :, WINDOW:]
    (p,) = _softmax_pieces([jnp.maximum(s_prev, s_cur)], sink)
    rows_i = lax.broadcasted_iota(jnp.int32, (WINDOW, WINDOW), 0)
    keys_i = lax.broadcasted_iota(jnp.int32, (WINDOW, WINDOW), 1)
    from_prev = keys_i > rows_i
    out = (bmm("hqk,hkd->hqd", jnp.where(from_prev, p, 0.0).astype(BF16), vp)
           + bmm("hqk,hkd->hqd", jnp.where(from_prev, 0.0, p).astype(BF16), vc))
    for j in range(nq):
        for pair in range(nh // 2):
            o_ref[j * WINDOW:(j + 1) * WINDOW, pair * LANES:(pair + 1) * LANES] = (
                out[j * nh + 2 * pair] + out[j * nh + 2 * pair + 1])


ATT_QBLOCKS = 4


def _attn_prompt(q, kx, vx, bias, sinks):
    s = q.shape[0]
    nq = ATT_QBLOCKS
    nb = s // (nq * WINDOW)
    cur = lambda n: (n, 0)
    prev = lambda n: (jnp.maximum(nq * n - 1, 0), 0)
    blk = (nq * WINDOW, 512)
    one = (WINDOW, 512)
    return pl.pallas_call(
        _attn_prompt_kernel,
        grid=(nb,),
        in_specs=[pl.BlockSpec(blk, cur), pl.BlockSpec(one, prev), pl.BlockSpec(blk, cur),
                  pl.BlockSpec(one, prev), pl.BlockSpec(blk, cur),
                  pl.BlockSpec(bias.shape, lambda n: (0, 0, 0)),
                  pl.BlockSpec(sinks.shape, lambda n: (0, 0, 0))],
        out_specs=pl.BlockSpec(blk, cur),
        out_shape=jax.ShapeDtypeStruct((s, ATT_WIDTH), F32),
        compiler_params=_cparams(("parallel",)),
        name="attn_prompt",
    )(q, kx, kx, vx, vx, bias, sinks)


SAMPLE_SEQS = 16


def _attn_sample_kernel(q_ref, kn_ref, vn_ref, ck_ref, cv_ref, bc_ref, bn_ref, sink_ref, o_ref):
    scale = HEAD_DIM ** -0.5
    lane = lax.broadcasted_iota(jnp.int32, (WINDOW, LANES), 1)
    low = lane < HEAD_DIM
    pad = jnp.zeros((WINDOW - SAMPLE_PAD, LANES), F32)
    rows = 2 * SAMPLE_PAD
    ns = q_ref.shape[0]
    qk_t = lambda a, b: jnp.einsum("bqd,bkd->bqk", a, b, preferred_element_type=F32)
    pv = lambda a, b: jnp.einsum("bqk,bkd->bqd", a, b, preferred_element_type=F32)
    q_all = q_ref[...].astype(BF16)
    kn = kn_ref[...]
    vn = vn_ref[...]
    ck = ck_ref[...]
    cv = cv_ref[...]
    ck_sw = pltpu.roll(ck, HEAD_DIM, 2)
    cv_sw = pltpu.roll(cv, HEAD_DIM, 2)
    pad = jnp.zeros((ns, WINDOW - SAMPLE_PAD, LANES), F32)
    for kv in range(N_KV_HEADS):
        qk = q_all[:, kv * rows:(kv + 1) * rows]
        acc = jnp.zeros((ns, rows, LANES), F32)
        for x in range(2):
            col = (kv * 2 + x) * LANES
            keep = low if x == 0 else jnp.logical_not(low)
            src_k, src_v = (ck, cv) if kv == x else (ck_sw, cv_sw)
            kc = jnp.where(keep, src_k, 0.0).astype(BF16)
            vc = jnp.where(keep, src_v, 0.0).astype(BF16)
            knx = jnp.concatenate([kn[:, :, col:col + LANES], pad], axis=1).astype(BF16)
            vnx = jnp.concatenate([vn[:, :, col:col + LANES], pad], axis=1).astype(BF16)
            s_c = qk_t(qk, kc) * scale + bc_ref[kv, x]
            s_n = qk_t(qk, knx) * scale + bn_ref[kv, x]
            p_c, p_n = _softmax_pieces([s_c, s_n], sink_ref[kv, x][:, 0:1])
            acc = acc + pv(p_c.astype(BF16), vc) + pv(p_n.astype(BF16), vnx)
        o_ref[:, kv * rows:(kv + 1) * rows, :] = acc


def _attn_sample(q, knew, vnew, cache_k, cache_v, bias_c, bias_n, sinks):
    db = q.shape[0]
    ns = SAMPLE_SEQS
    assert db % ns == 0
    b3 = lambda b: (b, 0, 0)
    f4 = lambda b: (0, 0, 0, 0)
    return pl.pallas_call(
        _attn_sample_kernel,
        grid=(db // ns,),
        in_specs=[pl.BlockSpec((ns,) + q.shape[1:], b3),
                  pl.BlockSpec((ns,) + knew.shape[1:], b3),
                  pl.BlockSpec((ns,) + vnew.shape[1:], b3),
                  pl.BlockSpec((ns, WINDOW, LANES), b3),
                  pl.BlockSpec((ns, WINDOW, LANES), b3),
                  pl.BlockSpec(bias_c.shape, f4),
                  pl.BlockSpec(bias_n.shape, f4),
                  pl.BlockSpec(sinks.shape, f4)],
        out_specs=pl.BlockSpec((ns,) + q.shape[1:], b3),
        out_shape=jax.ShapeDtypeStruct(q.shape, F32),
        compiler_params=_cparams(("parallel",)),
        name="attn_sample",
    )(q, knew, vnew, cache_k, cache_v, bias_c, bias_n, sinks)


def _log_sigmoid(x):
    return jnp.minimum(x, 0.0) - jnp.log1p(jnp.exp(-jnp.abs(x)))


def _cumsum_rows(x):
    axis = x.ndim - 2
    rows = x.shape[axis]
    idx = lax.broadcasted_iota(jnp.int32, x.shape, axis)
    k = 1
    while k < rows:
        x = x + jnp.where(idx >= k, pltpu.roll(x, k, axis), 0.0)
        k *= 2
    return x


def _mlstm_kernel(q_ref, k_ref, v_ref, o_ref, g_ref, gb_ref, nw_ref, c0_ref, n0_ref, m0_ref,
                  h_ref, c_out_ref, n_out_ref, m_out_ref, c_scr, n_scr, m_scr, *, rows, valid):
    c_idx = pl.program_id(1)

    @pl.when(c_idx == 0)
    def _():
        c_scr[...] = c0_ref[...]
        n_scr[...] = n0_ref[...]
        m_scr[...] = m0_ref[...]

    ns = c_scr.shape[0]
    nh = N_HEADS_M
    keys = M_HEAD_DIM
    bmm = functools.partial(jnp.einsum, preferred_element_type=F32)

    def heads(ref):
        x = ref[...].reshape(ns, rows, nh * M_HEAD_DIM)
        return jnp.concatenate([x[:, :, h * M_HEAD_DIM:(h + 1) * M_HEAD_DIM] for h in range(nh)], axis=0)

    def pad_rows(x):
        if rows == keys:
            return x
        return jnp.concatenate([x, jnp.zeros((x.shape[0], keys - rows, x.shape[2]), x.dtype)], axis=1)

    g = g_ref[...].reshape(ns, rows, LANES) + gb_ref[...]
    ridx = lax.broadcasted_iota(jnp.int32, g.shape, 1)
    ig_all = g
    lf_all = _log_sigmoid(g)
    if valid < rows:
        ig_all = jnp.where(ridx < valid, ig_all, NEG_INF)
        lf_all = jnp.where(ridx < valid, lf_all, 0.0)
    b_all = _cumsum_rows(lf_all)
    bc = jnp.concatenate([b_all[:, :, nh + h:nh + h + 1] for h in range(nh)], axis=0)
    igc = jnp.concatenate([ig_all[:, :, h:h + 1] for h in range(nh)], axis=0)

    t_idx = lax.broadcasted_iota(jnp.int32, (rows, keys), 0)
    s_idx = lax.broadcasted_iota(jnp.int32, (rows, keys), 1)
    eye = t_idx == s_idx
    causal = s_idx <= t_idx

    q = heads(q_ref)
    k = heads(k_ref)
    v = heads(v_ref)
    qb = q.astype(BF16)
    kb = pad_rows(k).astype(BF16)
    vb = pad_rows(v).astype(BF16)
    brow = jnp.sum(jnp.where(eye, bc, 0.0), axis=1, keepdims=True)
    igrow = jnp.sum(jnp.where(eye, igc, 0.0), axis=1, keepdims=True)
    logd = jnp.where(causal, bc - brow + igrow, -jnp.inf)
    c_old = jnp.concatenate([c_scr[:, h] for h in range(nh)], axis=0)
    n_old = jnp.concatenate([n_scr[:, h] for h in range(nh)], axis=0)
    m_prev = jnp.concatenate([m_scr[:, h] for h in range(nh)], axis=0)[:, :, 0:1]
    inter = bc + m_prev
    mt = jnp.maximum(inter, jnp.max(logd, axis=-1, keepdims=True))
    w_inter = jnp.exp(inter - mt)
    a = jnp.exp(logd - mt) * bmm("bqd,bkd->bqk", qb, kb)
    num = bmm("bqk,bkd->bqd", a.astype(BF16), vb) + w_inter * bmm("bqd,bvd->bqv", qb, c_old.astype(BF16))
    den = jnp.sum(a, axis=-1, keepdims=True) + w_inter * jnp.sum(q * n_old, axis=-1, keepdims=True)
    hid = num / jnp.maximum(jnp.abs(den), jnp.exp(-mt))
    mu = jnp.mean(hid, axis=-1, keepdims=True)
    hc = hid - mu
    var = jnp.mean(hc * hc, axis=-1, keepdims=True)
    hn = hc * lax.rsqrt(var + LN_EPS)
    for h in range(nh):
        sl = slice(h * M_HEAD_DIM, (h + 1) * M_HEAD_DIM)
        out = hn[h * ns:(h + 1) * ns].reshape(ns * rows, M_HEAD_DIM) * nw_ref[:, sl]
        h_ref[:, sl] = out * jax.nn.sigmoid(o_ref[:, sl])
    m_new = mt[:, rows - 1:rows]
    w_c = jnp.exp(inter[:, rows - 1:rows] - m_new)
    w_s = jnp.exp(bc[:, rows - 1:rows] - bc + igc - m_new)
    vs_t = jnp.swapaxes(pad_rows(v * w_s), 1, 2).astype(BF16)
    c_new = w_c * c_old + bmm("bvs,bsd->bvd", vs_t, kb)
    n_new = w_c * n_old + jnp.sum(w_s * k, axis=1, keepdims=True)
    m_row = jnp.broadcast_to(m_new, (nh * ns, 1, LANES))
    for h in range(nh):
        c_scr[:, h] = c_new[h * ns:(h + 1) * ns]
        n_scr[:, h] = n_new[h * ns:(h + 1) * ns]
        m_scr[:, h] = m_row[h * ns:(h + 1) * ns]

    @pl.when(c_idx == pl.num_programs(1) - 1)
    def _():
        c_out_ref[...] = c_scr[...]
        n_out_ref[...] = n_scr[...]
        m_out_ref[...] = m_scr[...]


def _mlstm(q, k, v, o, gates, gate_bias_row, norm_w_row, c0, n0, m0, *, rows, valid, seqs):
    batch = c0.shape[0]
    total = q.shape[0]
    nc = total // (batch * rows)
    assert batch % seqs == 0 and (seqs == 1 or nc == 1)
    seq = lambda b, c: (b * nc + c, 0)
    fixed = lambda b, c: (0, 0)
    st4 = lambda b, c: (b, 0, 0, 0)
    wide = pl.BlockSpec((seqs * rows, MLSTM_WIDTH), seq)
    c_spec = pl.BlockSpec((seqs, N_HEADS_M, M_HEAD_DIM, M_HEAD_DIM), st4)
    v_spec = pl.BlockSpec((seqs, N_HEADS_M, 1, LANES), st4)
    return pl.pallas_call(
        functools.partial(_mlstm_kernel, rows=rows, valid=valid),
        grid=(batch // seqs, nc),
        in_specs=[wide, wide, wide, wide,
                  pl.BlockSpec((seqs * rows, LANES), seq),
                  pl.BlockSpec((1, LANES), fixed),
                  pl.BlockSpec((1, MLSTM_WIDTH), fixed),
                  c_spec, v_spec, v_spec],
        out_specs=[wide, c_spec, v_spec, v_spec],
        out_shape=[jax.ShapeDtypeStruct((total, MLSTM_WIDTH), F32),
                   jax.ShapeDtypeStruct(c0.shape, F32),
                   jax.ShapeDtypeStruct(n0.shape, F32),
                   jax.ShapeDtypeStruct(m0.shape, F32)],
        scratch_shapes=[pltpu.VMEM((seqs, N_HEADS_M, M_HEAD_DIM, M_HEAD_DIM), F32),
                        pltpu.VMEM((seqs, N_HEADS_M, 1, LANES), F32),
                        pltpu.VMEM((seqs, N_HEADS_M, 1, LANES), F32)],
        compiler_params=_cparams(("parallel", "arbitrary")),
        name="mlstm",
    )(q, k, v, o, gates, gate_bias_row, norm_w_row, c0, n0, m0)


def _sorting_network(n):
    pairs = []

    def merge(lo, m, r):
        step = r * 2
        if step < m:
            merge(lo, m, step)
            merge(lo + r, m, step)
            pairs.extend((i, i + r) for i in range(lo + r, lo + m - r, step))
        else:
            pairs.append((lo, lo + r))

    def sort(lo, m):
        if m > 1:
            sort(lo, m // 2)
            sort(lo + m // 2, m // 2)
            merge(lo, m, 1)

    sort(0, n)
    return pairs


_SORT_PAIRS = _sorting_network(N_KEYS // SUBLANES)
TOP_ROWS = 24


def _top_values(x, count):
    v = [x[k * SUBLANES:(k + 1) * SUBLANES] for k in range(x.shape[0] // SUBLANES)]
    for i, j in _SORT_PAIRS:
        v[i], v[j] = jnp.maximum(v[i], v[j]), jnp.minimum(v[i], v[j])
    tops = []
    for r in range(count):
        m = jnp.max(v[0], axis=0, keepdims=True)
        tops.append(m)
        left = count - 1 - r
        if left:
            took = v[0] == m
            nxt = v[1:] + [jnp.full_like(v[0], -jnp.inf)]
            v = [jnp.where(took, nxt[k], v[k]) for k in range(min(len(v), left))]
    return tops


def _rows(vals):
    return jnp.concatenate(vals, axis=0)


def _mid_kernel(att_ref, mls_ref, x_ref, woa_ref, wom_ref, g1_ref, b1_ref, wqt_ref, keys_ref,
                h_ref, rank_ref, b_ref, a_ref, l_ref, s_scr, top_scr, *, alpha):
    tm = x_ref.shape[0]
    nch = tm // LANES
    mix = _dot(att_ref[...].astype(BF16), woa_ref[...]) + _dot(mls_ref[...].astype(BF16), wom_ref[...])
    hid = _layer_norm(alpha * x_ref[...] + mix, g1_ref[...], b1_ref[...])
    h_ref[...] = hid
    q_t = _dot_nt(wqt_ref[...], hid.astype(BF16))
    for hc in range(2 * PEER_HEADS):
        s_t = _dot(keys_ref[hc], q_t[hc * PEER_HALF:(hc + 1) * PEER_HALF].astype(BF16))
        for tc in range(nch):
            s_scr[hc, tc] = s_t[:, tc * LANES:(tc + 1) * LANES]

    n_items = PEER_HEADS * nch
    pad = [jnp.full((TOP_ROWS - PEER_TOPK - 1, LANES), -jnp.inf, F32)]

    def find_tops(item, slot):
        h = item // nch
        tc = item % nch
        for c in range(2):
            top_scr[slot, c] = _rows(_top_values(s_scr[2 * h + c, tc], PEER_TOPK + 1) + pad)

    find_tops(0, 0)

    def route_body(it, carry):
        find_tops(jnp.minimum(it + 1, n_items - 1), (it + 1) % 2)
        h = it // nch
        tc = it % nch
        s1 = s_scr[2 * h, tc]
        s2 = s_scr[2 * h + 1, tc]
        tops1 = top_scr[it % 2, 0]
        tops2 = top_scr[it % 2, 1]
        t1 = [tops1[r:r + 1] for r in range(PEER_TOPK + 1)]
        t2 = [tops2[r:r + 1] for r in range(PEER_TOPK + 1)]
        wide = [tops1[:SUBLANES] + t2[r] for r in range(PEER_TOPK + 1)]
        tall = [t1[SUBLANES + k] + t2[0] for k in range(PEER_TOPK + 1 - SUBLANES)]
        cmax = t1[0] + t2[0]
        z = jnp.zeros_like(cmax)
        tau = below = cmax
        for r in range(PEER_TOPK + 1):
            head = jnp.maximum(jnp.max(wide[0], axis=0, keepdims=True), tall[0])
            if r < PEER_TOPK:
                tau = head
                z = z + jnp.exp(head - cmax)
            else:
                below = head
            left = PEER_TOPK - r
            if left:
                took_w = wide[0] == head
                took_t = tall[0] == head
                nxt_w = wide[1:] + [jnp.full_like(wide[0], -jnp.inf)]
                nxt_t = tall[1:] + [jnp.full_like(tall[0], -jnp.inf)]
                wide = [jnp.where(took_w, nxt_w[k], wide[k]) for k in range(min(len(wide), left))]
                tall = [jnp.where(took_t, nxt_t[k], tall[k]) for k in range(min(len(tall), left))]
        cut = 0.5 * (tau + below)
        rank = jnp.zeros_like(s2)
        count = jnp.zeros_like(s1)
        for r in range(PEER_TOPK):
            rank = jnp.where(t2[r] > s2, (r + 1) * RANK_SCALE, rank)
            count = jnp.where(s1 >= cut - t2[r], (r + 1) * RANK_SCALE, count)
        rank_ref[h, tc] = rank.astype(BF16)
        l_ref[h, tc] = count
        b_ref[h, tc] = jnp.exp(s2 - t2[0]).astype(BF16)
        a_ref[h, tc] = jnp.exp(s1 - (t1[0] + jnp.log(z)))
        return carry

    lax.fori_loop(0, n_items, route_body, 0)


def _mid(att, mls, x, wo_a, wo_m, g1, b1, wq_t, keys, alpha):
    t = x.shape[0]
    tm = TOKEN_TILE
    nch = tm // LANES
    row = lambda i: (i, 0)
    fixed2 = lambda i: (0, 0)
    fixed3 = lambda i: (0, 0, 0)
    tile4 = lambda i: (0, i, 0, 0)
    s_shape = lambda dt: jax.ShapeDtypeStruct((PEER_HEADS, t // LANES, N_KEYS, LANES), dt)
    s_spec = pl.BlockSpec((PEER_HEADS, nch, N_KEYS, LANES), tile4)
    return pl.pallas_call(
        functools.partial(_mid_kernel, alpha=alpha),
        grid=(t // tm,),
        in_specs=[pl.BlockSpec((tm, ATT_WIDTH), row), pl.BlockSpec((tm, MLSTM_WIDTH), row),
                  pl.BlockSpec((tm, D_MODEL), row),
                  pl.BlockSpec(wo_a.shape, fixed2), pl.BlockSpec(wo_m.shape, fixed2),
                  pl.BlockSpec(g1.shape, fixed2), pl.BlockSpec(b1.shape, fixed2),
                  pl.BlockSpec(wq_t.shape, fixed2), pl.BlockSpec(keys.shape, fixed3)],
        out_specs=[pl.BlockSpec((tm, D_MODEL), row), s_spec, s_spec, s_spec, s_spec],
        out_shape=[jax.ShapeDtypeStruct((t, D_MODEL), F32), s_shape(BF16), s_shape(BF16), s_shape(F32), s_shape(F32)],
        scratch_shapes=[pltpu.VMEM((2 * PEER_HEADS, nch, N_KEYS, LANES), F32),
                        pltpu.VMEM((2, 2, TOP_ROWS, LANES), F32)],
        compiler_params=_cparams(("parallel",)),
        name="out_proj_router",
    )(att, mls, x, wo_a, wo_m, g1, b1, wq_t, keys)


def _gelu(x):
    return 0.5 * x * (1.0 + lax.erf(x * (2.0 ** -0.5)))


ACT_TOKENS = 256
ACT_EXPERTS = (512, 512)
assert sum(ACT_EXPERTS) == EXPERT_BLOCK


RANK_SCALE = 256.0


def _row_tile(row):
    packed = jnp.broadcast_to(row, (2 * SUBLANES, row.shape[1])).astype(BF16)
    reps = N_KEYS // (2 * SUBLANES)
    return jnp.broadcast_to(packed[None], (reps,) + packed.shape).reshape(N_KEYS, row.shape[1])


def _zero_after(x):
    bits = lax.bitcast_convert_type(x, jnp.uint32)
    bits = lax.shift_right_logical(lax.shift_right_logical(bits, jnp.uint32(16)), jnp.uint32(16))
    return lax.bitcast_convert_type(bits, F32)


def _peer_block(u_ref, vt_ref, a_ref, l_ref, rank_ref, b_ref, ht_scr, w_new, w_old, acc_scr):
    tm = ht_scr.shape[1]
    n_tp = tm // ACT_TOKENS
    starts = [sum(ACT_EXPERTS[:g]) for g in range(len(ACT_EXPERTS))]
    tiles = [(r0, nr, tp) for r0, nr in zip(starts, ACT_EXPERTS) for tp in range(n_tp)]
    slice_at = {(sl + 1) * len(tiles) // (n_tp + 1): sl for sl in range(n_tp)}
    for k, (r0, nr, tp) in enumerate(tiles):
        zero = None
        if k in slice_at:
            cols = slice(slice_at[k] * ACT_TOKENS, (slice_at[k] + 1) * ACT_TOKENS)
            upd = acc_scr[:, cols] + _dot(vt_ref[...], w_old[:, cols])
            acc_scr[:, cols] = upd
            zero = _zero_after(upd[0:1, 0:LANES])
        act = _dot(u_ref[r0:r0 + nr, :], ht_scr[:, tp * ACT_TOKENS:(tp + 1) * ACT_TOKENS])
        for d in range(nr // N_KEYS):
            il = r0 // N_KEYS + d
            for half in range(ACT_TOKENS // LANES):
                tc = tp * (ACT_TOKENS // LANES) + half
                gate = jnp.zeros((N_KEYS, LANES), BF16)
                if zero is not None and d == 0 and half == 0:
                    gate = gate + _row_tile(zero)
                for h in range(PEER_HEADS):
                    room = jnp.maximum(_row_tile(l_ref[h, tc, il:il + 1, :]) - rank_ref[h, tc], 0)
                    gate = gate + jnp.minimum(_row_tile(a_ref[h, tc, il:il + 1, :]) * b_ref[h, tc], room)
                piece = act[d * N_KEYS:(d + 1) * N_KEYS, half * LANES:(half + 1) * LANES]
                w_new[il * N_KEYS:(il + 1) * N_KEYS, tc * LANES:(tc + 1) * LANES] = gate * _gelu(piece).astype(BF16)


def _peer_kernel(h_ref, rank_ref, b_ref, a0_ref, l0_ref, a1_ref, l1_ref, u0_ref, u1_ref,
                 vtp_ref, vt0_ref, vtl_ref, g2_ref, b2_ref, y_ref, ht_scr, w0, w1, acc_scr, *, alpha):
    s = pl.program_id(1)

    @pl.when(s == 0)
    def _():
        ht_scr[...] = h_ref[...].T.astype(BF16)
        acc_scr[...] = jnp.zeros_like(acc_scr)
        w1[...] = jnp.zeros_like(w1)

    _peer_block(u0_ref, vtp_ref, a0_ref, l0_ref, rank_ref, b_ref, ht_scr, w0, w1, acc_scr)
    _peer_block(u1_ref, vt0_ref, a1_ref, l1_ref, rank_ref, b_ref, ht_scr, w1, w0, acc_scr)

    @pl.when(s == pl.num_programs(1) - 1)
    def _():
        f = (acc_scr[...] + _dot(vtl_ref[...], w1[...])).T
        y_ref[...] = _layer_norm(alpha * h_ref[...] + f, g2_ref[...], b2_ref[...])


def _peer(hid, rank, b, a, cnt, u_b, vt_b, g2, b2, alpha):
    t = hid.shape[0]
    tm = TOKEN_TILE
    nch = tm // LANES
    nblk = u_b.shape[0] // EXPERT_BLOCK
    rows = EXPERT_BLOCK // N_KEYS
    row = lambda i, s: (i, 0)
    fixed2 = lambda i, s: (0, 0)
    full_spec = pl.BlockSpec((PEER_HEADS, nch, N_KEYS, LANES), lambda i, s: (0, i, 0, 0))
    rows_spec = lambda f: pl.BlockSpec((PEER_HEADS, nch, rows, LANES), lambda i, s: (0, i, f(s), 0))
    u_spec = lambda f: pl.BlockSpec((EXPERT_BLOCK, D_MODEL), lambda i, s: (f(s), 0))
    vt_spec = lambda f: pl.BlockSpec((D_MODEL, EXPERT_BLOCK), lambda i, s: (0, f(s)))
    even = lambda s: 2 * s
    odd = lambda s: 2 * s + 1
    return pl.pallas_call(
        functools.partial(_peer_kernel, alpha=alpha),
        grid=(t // tm, nblk // 2),
        in_specs=[pl.BlockSpec((tm, D_MODEL), row), full_spec, full_spec,
                  rows_spec(even), rows_spec(even), rows_spec(odd), rows_spec(odd),
                  u_spec(even), u_spec(odd),
                  vt_spec(lambda s: jnp.maximum(2 * s - 1, 0)), vt_spec(even), vt_spec(lambda s: nblk - 1),
                  pl.BlockSpec(g2.shape, fixed2), pl.BlockSpec(b2.shape, fixed2)],
        out_specs=pl.BlockSpec((tm, D_MODEL), row),
        out_shape=jax.ShapeDtypeStruct((t, D_MODEL), F32),
        scratch_shapes=[pltpu.VMEM((D_MODEL, tm), BF16),
                        pltpu.VMEM((EXPERT_BLOCK, tm), BF16),
                        pltpu.VMEM((EXPERT_BLOCK, tm), BF16),
                        pltpu.VMEM((D_MODEL, tm), F32)],
        compiler_params=_cparams(("parallel", "arbitrary")),
        name="peer_experts",
    )(hid, rank, b, a, cnt, a, cnt, u_b, u_b, vt_b, vt_b, vt_b, g2, b2)


def _finish(x, att, mls, lw, alpha):
    hid, rank, b, a, cnt = _mid(att, mls, x, lw["wo_a"], lw["wo_m"], lw["g1"], lw["b1"], lw["wq_t"], lw["keys"], alpha)
    return _peer(hid, rank, b, a, cnt, lw["u"], lw["v_t"], lw["g2"], lw["b2"], alpha)


def _prompt_bias_index():
    q = np.arange(WINDOW)[:, None]
    kband = np.arange(2 * WINDOW)[None, :]
    dist = q + WINDOW - kband
    return _bias_index_table(dist, (dist >= 0) & (dist < WINDOW))


def _sample_bias_index(steps):
    q = np.arange(SAMPLE_PAD)[:, None]
    c = np.arange(WINDOW)[None, :]
    dist_c = WINDOW - c + q
    idx_c = _bias_index_table(dist_c, (dist_c >= 0) & (dist_c < WINDOW))
    jn = np.arange(LANES)[None, :]
    dist_n = q - jn
    idx_n = _bias_index_table(dist_n, (dist_n >= 0) & (dist_n < WINDOW) & (jn < steps))
    return np.concatenate([idx_c, idx_n], axis=1)


def _per_row_head_layout(tab):
    _, steps, nk = tab.shape
    t = tab.reshape(N_KV_HEADS, 2, 2, steps, nk)
    return t.transpose(0, 2, 1, 3, 4).reshape(N_KV_HEADS, 2, 2 * steps, nk)


def kernel(x_prompt, x_sample, cache_k_win, cache_v_win, state_C, state_n, state_m, w_in, gate_bias,
           attn_sinks, rel_bias, mlstm_norm_w, w_out, ln1_g, ln1_b, peer_wq, peer_keys, peer_u, peer_v,
           ln2_g, ln2_b):
    depth = w_in.shape[0]
    alpha = (2 * depth) ** 0.25
    bsz, seq, _ = x_prompt.shape
    db, ds, _ = x_sample.shape
    assert bsz == 1 and seq % TOKEN_TILE == 0 and ds <= SAMPLE_PAD and (db * ds) % TOKEN_TILE == 0
    assert PAST_LEN >= WINDOW

    bias_p = _bias_table(_prompt_bias_index(), rel_bias)
    bias_s = _bias_table(_sample_bias_index(ds), rel_bias)
    bias_sc = _per_row_head_layout(bias_s[:, :, :WINDOW])
    bias_sn = _per_row_head_layout(bias_s[:, :, WINDOW:])

    xp = x_prompt[0]
    xs = x_sample
    pk, pv, pc, pn, pm = [], [], [], [], []
    sk, sv, sc, sn, sm = [], [], [], [], []
    for l in range(depth):
        w_main, w_gate = _projection_weights(w_in[l])
        gb_row = jnp.pad(gate_bias[l], (0, LANES - 2 * N_HEADS_M))[None, :].astype(F32)
        nw_row = mlstm_norm_w[l].reshape(1, MLSTM_WIDTH).astype(F32)
        sinks = attn_sinks[l].astype(F32)
        sinks_p = jnp.broadcast_to(sinks[:, None, None], (N_HEADS_ATT, 1, LANES))
        sinks_s = _per_row_head_layout(
            jnp.broadcast_to(sinks[:, None, None], (N_HEADS_ATT, SAMPLE_PAD, LANES)))
        lw = dict(
            wo_a=w_out[l, :ATT_WIDTH].astype(BF16), wo_m=w_out[l, ATT_WIDTH:].astype(BF16),
            g1=ln1_g[l][None, :], b1=ln1_b[l][None, :],
            wq_t=peer_wq[l].T.astype(BF16),
            keys=peer_keys[l].reshape(2 * PEER_HEADS, N_KEYS, PEER_HALF).astype(BF16),
            u=peer_u[l].astype(BF16), v_t=peer_v[l].T.astype(BF16),
            g2=ln2_g[l][None, :], b2=ln2_b[l][None, :])

        q_a, k_x, v_x, q_m, k_m, v_m, o_m, gates = _project(xp, w_main, w_gate)
        att = _attn_prompt(q_a, k_x, v_x, bias_p, sinks_p)
        zeros_c = jnp.zeros((1, N_HEADS_M, M_HEAD_DIM, M_HEAD_DIM), F32)
        zeros_v = jnp.zeros((1, N_HEADS_M, 1, LANES), F32)
        mls, c_p, n_p, m_p = _mlstm(q_m, k_m, v_m, o_m, gates, gb_row, nw_row, zeros_c, zeros_v, zeros_v,
                                    rows=M_HEAD_DIM, valid=M_HEAD_DIM, seqs=1)
        tail_k = k_x[seq - WINDOW:].reshape(WINDOW, N_KV_HEADS, 2, LANES)[:, :, 0, :HEAD_DIM]
        tail_v = v_x[seq - WINDOW:].reshape(WINDOW, N_KV_HEADS, 2, LANES)[:, :, 0, :HEAD_DIM]
        pk.append(tail_k[None])
        pv.append(tail_v[None])
        pc.append(c_p)
        pn.append(n_p[:, :, 0, :])
        pm.append(m_p[:, :, 0, 0])
        xp = _finish(xp, att, mls, lw, alpha)

        xs_pad = jnp.pad(xs, ((0, 0), (0, SAMPLE_PAD - ds), (0, 0))).reshape(db * SAMPLE_PAD, D_MODEL)
        q_a, k_x, v_x, q_m, k_m, v_m, o_m, gates = _project(xs_pad, w_main, w_gate)
        q_rows = q_a.reshape(db, SAMPLE_PAD, 4, LANES).transpose(0, 2, 1, 3).reshape(db, 4 * SAMPLE_PAD, LANES)
        att_rows = _attn_sample(q_rows, k_x.reshape(db, SAMPLE_PAD, 512), v_x.reshape(db, SAMPLE_PAD, 512),
                                cache_k_win[l].reshape(db, WINDOW, LANES).astype(F32),
                                cache_v_win[l].reshape(db, WINDOW, LANES).astype(F32),
                                bias_sc, bias_sn, sinks_s)
        att = att_rows.reshape(db, 4, SAMPLE_PAD, LANES).transpose(0, 2, 1, 3)[:, :ds].reshape(db * ds, ATT_WIDTH)
        c0 = state_C[l].astype(F32)
        n0 = state_n[l].astype(F32)[:, :, None, :]
        m0 = jnp.broadcast_to(state_m[l].astype(F32)[:, :, None, None], (db, N_HEADS_M, 1, LANES))
        mls, c_s, n_s, m_s = _mlstm(q_m, k_m, v_m, o_m, gates, gb_row, nw_row, c0, n0, m0,
                                    rows=SAMPLE_PAD, valid=ds, seqs=SAMPLE_SEQS)
        mls = mls.reshape(db, SAMPLE_PAD, MLSTM_WIDTH)[:, :ds].reshape(db * ds, MLSTM_WIDTH)
        k_new = k_x.reshape(db, SAMPLE_PAD, N_KV_HEADS, 2, LANES)[:, :ds, :, 0, :HEAD_DIM]
        v_new = v_x.reshape(db, SAMPLE_PAD, N_KV_HEADS, 2, LANES)[:, :ds, :, 0, :HEAD_DIM]
        sk.append(jnp.concatenate([cache_k_win[l].astype(F32), k_new], axis=1)[:, -WINDOW:])
        sv.append(jnp.concatenate([cache_v_win[l].astype(F32), v_new], axis=1)[:, -WINDOW:])
        sc.append(c_s)
        sn.append(n_s[:, :, 0, :])
        sm.append(m_s[:, :, 0, 0])
        xs = _finish(xs.reshape(db * ds, D_MODEL), att, mls, lw, alpha).reshape(db, ds, D_MODEL)

    return (xp[None], xs, jnp.stack(pk), jnp.stack(pv), jnp.stack(pc), jnp.stack(pn), jnp.stack(pm),
            jnp.stack(sk), jnp.stack(sv), jnp.stack(sc), jnp.stack(sn), jnp.stack(sm))
```

```python
import functools
import math

import numpy as np
import jax
import jax.numpy as jnp
from jax import lax
from jax.experimental import pallas as pl
from jax.experimental.pallas import tpu as pltpu

F32 = jnp.float32
BF16 = jnp.bfloat16

D_MODEL = 1024
HEAD_DIM = 64
N_HEADS_ATT = 8
N_KV_HEADS = 2
GQA_GROUP = N_HEADS_ATT // N_KV_HEADS
WINDOW = 128
NUM_BUCKETS = 32
MAX_DISTANCE = 128
N_HEADS_M = 4
M_HEAD_DIM = 128
MLSTM_WIDTH = N_HEADS_M * M_HEAD_DIM
ATT_WIDTH = N_HEADS_ATT * HEAD_DIM
PEER_HEADS = 8
N_KEYS = 128
PEER_TOPK = 16
PEER_HALF = 128
LN_EPS = 1e-5
NEG_INF = -1e30
PAST_LEN = 16384

LANES = 128
SUBLANES = 8
VMEM_LIMIT = 56 * 1024 * 1024

TOKEN_TILE = 512
EXPERT_BLOCK = 1024
SAMPLE_PAD = 8


def _cparams(sem):
    return pltpu.CompilerParams(dimension_semantics=sem, vmem_limit_bytes=VMEM_LIMIT)


def _dot(a, b):
    return jnp.dot(a, b, preferred_element_type=F32)


def _dot_nt(a, b):
    return lax.dot_general(a, b, (((1,), (1,)), ((), ())), preferred_element_type=F32)


def _layer_norm(x, g, b):
    mu = jnp.mean(x, axis=-1, keepdims=True)
    xc = x - mu
    var = jnp.mean(xc * xc, axis=-1, keepdims=True)
    return xc * lax.rsqrt(var + LN_EPS) * g + b


def _t5_bucket_np(dist):
    n = np.maximum(dist, 0)
    exact = NUM_BUCKETS // 2
    nf = np.maximum(n, 1).astype(np.float64)
    val = np.log(nf / exact) / math.log(MAX_DISTANCE / exact) * (NUM_BUCKETS - exact)
    frac = np.abs(val - np.round(val))
    assert not np.any((frac < 1e-6) & (n > exact) & (n != MAX_DISTANCE)), "bucket boundary is rounding sensitive"
    large = np.minimum(exact + np.floor(val + 1e-9).astype(np.int64), NUM_BUCKETS - 1)
    return np.where(n < exact, n, large).astype(np.int32)


def _bias_index_table(dist, valid):
    return np.where(valid, _t5_bucket_np(dist), -1).astype(np.int32)


def _bias_table_kernel(idx_ref, rb_ref, out_ref):
    idx = idx_ref[...]
    for h in range(N_HEADS_ATT):
        acc = jnp.full(idx.shape, NEG_INF, F32)
        for b in range(NUM_BUCKETS):
            acc = jnp.where(idx == b, rb_ref[b, h], acc)
        out_ref[h] = acc


def _bias_table(idx_np, rel_bias):
    r, c = idx_np.shape
    return pl.pallas_call(
        _bias_table_kernel,
        out_shape=jax.ShapeDtypeStruct((N_HEADS_ATT, r, c), F32),
        in_specs=[pl.BlockSpec(memory_space=pltpu.VMEM), pl.BlockSpec(memory_space=pltpu.SMEM)],
        out_specs=pl.BlockSpec(memory_space=pltpu.VMEM),
        name="bias_table",
    )(jnp.asarray(idx_np), rel_bias.astype(F32))


def _proj_kernel(x_ref, w_ref, wg_ref, q_ref, k_ref, v_ref, qm_ref, km_ref, vm_ref, om_ref, g_ref):
    x = x_ref[...]
    xb = x.astype(BF16)
    outs = (q_ref, k_ref, v_ref, qm_ref, km_ref, vm_ref, om_ref)
    for n, o_ref in enumerate(outs):
        z = _dot(xb, w_ref[:, n * 512:(n + 1) * 512])
        if o_ref is km_ref:
            z = z * (M_HEAD_DIM ** -0.5)
        o_ref[...] = z
    x_lo = (x - xb.astype(F32)).astype(BF16)
    both = _dot(xb, wg_ref[...])
    g_ref[...] = both[:, :LANES] + both[:, LANES:] + _dot(x_lo, wg_ref[:, :LANES])


def _project(x, w_main, w_gate):
    t = x.shape[0]
    tm = TOKEN_TILE
    wide = jax.ShapeDtypeStruct((t, 512), F32)
    row = lambda i: (i, 0)
    fixed = lambda i: (0, 0)
    return pl.pallas_call(
        _proj_kernel,
        grid=(t // tm,),
        in_specs=[pl.BlockSpec((tm, D_MODEL), row),
                  pl.BlockSpec(w_main.shape, fixed),
                  pl.BlockSpec(w_gate.shape, fixed)],
        out_specs=[pl.BlockSpec((tm, 512), row)] * 7 + [pl.BlockSpec((tm, LANES), row)],
        out_shape=[wide] * 7 + [jax.ShapeDtypeStruct((t, LANES), F32)],
        compiler_params=_cparams(("parallel",)),
        name="in_proj",
    )(x, w_main, w_gate)


def _projection_weights(w_in):
    q_a = w_in[:, 0:512]
    k_a = w_in[:, 512:640]
    v_a = w_in[:, 640:768]
    rest = w_in[:, 768:768 + 4 * MLSTM_WIDTH]
    gates = w_in[:, 768 + 4 * MLSTM_WIDTH:]
    zero = jnp.zeros((D_MODEL, HEAD_DIM), w_in.dtype)

    def lo_hi(w):
        cols = []
        for kv in range(N_KV_HEADS):
            wk = w[:, kv * HEAD_DIM:(kv + 1) * HEAD_DIM]
            cols += [wk, zero, zero, wk]
        return jnp.concatenate(cols, axis=1)

    w_main = jnp.concatenate([q_a, lo_hi(k_a), lo_hi(v_a), rest], axis=1).astype(BF16)
    w_gate = jnp.pad(gates, ((0, 0), (0, LANES - gates.shape[1]))).astype(F32)
    w_hi = w_gate.astype(BF16)
    w_lo = (w_gate - w_hi.astype(F32)).astype(BF16)
    return w_main, jnp.concatenate([w_hi, w_lo], axis=1)


def _softmax_pieces(pieces, sink):
    mx = sink
    for s in pieces:
        mx = jnp.maximum(mx, jnp.max(s, axis=-1, keepdims=True))
    es = [jnp.exp(s - mx) for s in pieces]
    den = jnp.exp(sink - mx)
    for e in es:
        den = den + jnp.sum(e, axis=-1, keepdims=True)
    inv = 1.0 / den
    return [e * inv for e in es]


def _attn_prompt_kernel(q_ref, kp_ref, kc_ref, vp_ref, vc_ref, bias_ref, sink_ref, o_ref):
    n = pl.program_id(0)
    scale = HEAD_DIM ** -0.5
    nh = N_HEADS_ATT
    nq = q_ref.shape[0] // WINDOW
    bmm = functools.partial(jnp.einsum, preferred_element_type=F32)

    def per_head(blocks, col_of):
        return jnp.stack([x[:, col_of(h) * LANES:(col_of(h) + 1) * LANES] for x in blocks for h in range(nh)], axis=0)

    def blocks_of(ref):
        x = ref[...].astype(BF16)
        return [x[j * WINDOW:(j + 1) * WINDOW] for j in range(x.shape[0] // WINDOW)]

    kv_col = lambda h: (h // GQA_GROUP) * 2 + h % 2
    k_cur, v_cur = blocks_of(kc_ref), blocks_of(vc_ref)
    k_prev = blocks_of(kp_ref) + k_cur[:-1]
    v_prev = blocks_of(vp_ref) + v_cur[:-1]
    q = per_head(blocks_of(q_ref), lambda h: h // 2)
    kp = per_head(k_prev, kv_col)
    kc = per_head(k_cur, kv_col)
    vp = per_head(v_prev, kv_col)
    vc = per_head(v_cur, kv_col)
    bias = jnp.concatenate([bias_ref[...]] * nq, axis=0)
    sink = jnp.concatenate([sink_ref[...][:, :, 0:1]] * nq, axis=0)
    s_prev = bmm("hqd,hkd->hqk", q, kp) * scale + bias[:, :, :WINDOW]
    first = lax.broadcasted_iota(jnp.int32, (nq * nh, 1, 1), 0) < jnp.where(n == 0, nh, 0)
    s_prev = jnp.where(first, NEG_INF, s_prev)
    s_cur = bmm("hqd,hkd->hqk", q, kc) * scale + bias[:, :, WINDOW:]
    (p,) = _softmax_pieces([jnp.maximum(s_prev, s_cur)], sink)
    rows_i = lax.broadcasted_iota(jnp.int32, (WINDOW, WINDOW), 0)
    keys_i = lax.broadcasted_iota(jnp.int32, (WINDOW, WINDOW), 1)
    from_prev = keys_i > rows_i
    out = (bmm("hqk,hkd->hqd", jnp.where(from_prev, p, 0.0).astype(BF16), vp)
           + bmm("hqk,hkd->hqd", jnp.where(from_prev, 0.0, p).astype(BF16), vc))
    for j in range(nq):
        for pair in range(nh // 2):
            o_ref[j * WINDOW:(j + 1) * WINDOW, pair * LANES:(pair + 1) * LANES] = (
                out[j * nh + 2 * pair] + out[j * nh + 2 * pair + 1])


ATT_QBLOCKS = 4


def _attn_prompt(q, kx, vx, bias, sinks):
    s = q.shape[0]
    nq = ATT_QBLOCKS
    nb = s // (nq * WINDOW)
    cur = lambda n: (n, 0)
    prev = lambda n: (jnp.maximum(nq * n - 1, 0), 0)
    blk = (nq * WINDOW, 512)
    one = (WINDOW, 512)
    return pl.pallas_call(
        _attn_prompt_kernel,
        grid=(nb,),
        in_specs=[pl.BlockSpec(blk, cur), pl.BlockSpec(one, prev), pl.BlockSpec(blk, cur),
                  pl.BlockSpec(one, prev), pl.BlockSpec(blk, cur),
                  pl.BlockSpec(bias.shape, lambda n: (0, 0, 0)),
                  pl.BlockSpec(sinks.shape, lambda n: (0, 0, 0))],
        out_specs=pl.BlockSpec(blk, cur),
        out_shape=jax.ShapeDtypeStruct((s, ATT_WIDTH), F32),
        compiler_params=_cparams(("parallel",)),
        name="attn_prompt",
    )(q, kx, kx, vx, vx, bias, sinks)


SAMPLE_SEQS = 16


def _attn_sample_kernel(q_ref, kn_ref, vn_ref, ck_ref, cv_ref, bc_ref, bn_ref, sink_ref, o_ref):
    scale = HEAD_DIM ** -0.5
    lane = lax.broadcasted_iota(jnp.int32, (WINDOW, LANES), 1)
    low = lane < HEAD_DIM
    pad = jnp.zeros((WINDOW - SAMPLE_PAD, LANES), F32)
    rows = 2 * SAMPLE_PAD
    ns = q_ref.shape[0]
    qk_t = lambda a, b: jnp.einsum("bqd,bkd->bqk", a, b, preferred_element_type=F32)
    pv = lambda a, b: jnp.einsum("bqk,bkd->bqd", a, b, preferred_element_type=F32)
    q_all = q_ref[...].astype(BF16)
    kn = kn_ref[...]
    vn = vn_ref[...]
    ck = ck_ref[...]
    cv = cv_ref[...]
    ck_sw = pltpu.roll(ck, HEAD_DIM, 2)
    cv_sw = pltpu.roll(cv, HEAD_DIM, 2)
    pad = jnp.zeros((ns, WINDOW - SAMPLE_PAD, LANES), F32)
    for kv in range(N_KV_HEADS):
        qk = q_all[:, kv * rows:(kv + 1) * rows]
        acc = jnp.zeros((ns, rows, LANES), F32)
        for x in range(2):
            col = (kv * 2 + x) * LANES
            keep = low if x == 0 else jnp.logical_not(low)
            src_k, src_v = (ck, cv) if kv == x else (ck_sw, cv_sw)
            kc = jnp.where(keep, src_k, 0.0).astype(BF16)
            vc = jnp.where(keep, src_v, 0.0).astype(BF16)
            knx = jnp.concatenate([kn[:, :, col:col + LANES], pad], axis=1).astype(BF16)
            vnx = jnp.concatenate([vn[:, :, col:col + LANES], pad], axis=1).astype(BF16)
            s_c = qk_t(qk, kc) * scale + bc_ref[kv, x]
            s_n = qk_t(qk, knx) * scale + bn_ref[kv, x]
            p_c, p_n = _softmax_pieces([s_c, s_n], sink_ref[kv, x][:, 0:1])
            acc = acc + pv(p_c.astype(BF16), vc) + pv(p_n.astype(BF16), vnx)
        o_ref[:, kv * rows:(kv + 1) * rows, :] = acc


def _attn_sample(q, knew, vnew, cache_k, cache_v, bias_c, bias_n, sinks):
    db = q.shape[0]
    ns = SAMPLE_SEQS
    assert db % ns == 0
    b3 = lambda b: (b, 0, 0)
    f4 = lambda b: (0, 0, 0, 0)
    return pl.pallas_call(
        _attn_sample_kernel,
        grid=(db // ns,),
        in_specs=[pl.BlockSpec((ns,) + q.shape[1:], b3),
                  pl.BlockSpec((ns,) + knew.shape[1:], b3),
                  pl.BlockSpec((ns,) + vnew.shape[1:], b3),
                  pl.BlockSpec((ns, WINDOW, LANES), b3),
                  pl.BlockSpec((ns, WINDOW, LANES), b3),
                  pl.BlockSpec(bias_c.shape, f4),
                  pl.BlockSpec(bias_n.shape, f4),
                  pl.BlockSpec(sinks.shape, f4)],
        out_specs=pl.BlockSpec((ns,) + q.shape[1:], b3),
        out_shape=jax.ShapeDtypeStruct(q.shape, F32),
        compiler_params=_cparams(("parallel",)),
        name="attn_sample",
    )(q, knew, vnew, cache_k, cache_v, bias_c, bias_n, sinks)


def _log_sigmoid(x):
    return jnp.minimum(x, 0.0) - jnp.log1p(jnp.exp(-jnp.abs(x)))


def _cumsum_rows(x):
    axis = x.ndim - 2
    rows = x.shape[axis]
    idx = lax.broadcasted_iota(jnp.int32, x.shape, axis)
    k = 1
    while k < rows:
        x = x + jnp.where(idx >= k, pltpu.roll(x, k, axis), 0.0)
        k *= 2
    return x


def _mlstm_kernel(q_ref, k_ref, v_ref, o_ref, g_ref, gb_ref, nw_ref, c0_ref, n0_ref, m0_ref,
                  h_ref, c_out_ref, n_out_ref, m_out_ref, c_scr, n_scr, m_scr, *, rows, valid):
    c_idx = pl.program_id(1)

    @pl.when(c_idx == 0)
    def _():
        c_scr[...] = c0_ref[...]
        n_scr[...] = n0_ref[...]
        m_scr[...] = m0_ref[...]

    ns = c_scr.shape[0]
    nh = N_HEADS_M
    keys = M_HEAD_DIM
    bmm = functools.partial(jnp.einsum, preferred_element_type=F32)

    def heads(ref):
        x = ref[...].reshape(ns, rows, nh * M_HEAD_DIM)
        return jnp.concatenate([x[:, :, h * M_HEAD_DIM:(h + 1) * M_HEAD_DIM] for h in range(nh)], axis=0)

    def pad_rows(x):
        if rows == keys:
            return x
        return jnp.concatenate([x, jnp.zeros((x.shape[0], keys - rows, x.shape[2]), x.dtype)], axis=1)

    g = g_ref[...].reshape(ns, rows, LANES) + gb_ref[...]
    ridx = lax.broadcasted_iota(jnp.int32, g.shape, 1)
    ig_all = g
    lf_all = _log_sigmoid(g)
    if valid < rows:
        ig_all = jnp.where(ridx < valid, ig_all, NEG_INF)
        lf_all = jnp.where(ridx < valid, lf_all, 0.0)
    b_all = _cumsum_rows(lf_all)
    bc = jnp.concatenate([b_all[:, :, nh + h:nh + h + 1] for h in range(nh)], axis=0)
    igc = jnp.concatenate([ig_all[:, :, h:h + 1] for h in range(nh)], axis=0)

    t_idx = lax.broadcasted_iota(jnp.int32, (rows, keys), 0)
    s_idx = lax.broadcasted_iota(jnp.int32, (rows, keys), 1)
    eye = t_idx == s_idx
    causal = s_idx <= t_idx

    q = heads(q_ref)
    k = heads(k_ref)
    v = heads(v_ref)
    qb = q.astype(BF16)
    kb = pad_rows(k).astype(BF16)
    vb = pad_rows(v).astype(BF16)
    brow = jnp.sum(jnp.where(eye, bc, 0.0), axis=1, keepdims=True)
    igrow = jnp.sum(jnp.where(eye, igc, 0.0), axis=1, keepdims=True)
    logd = jnp.where(causal, bc - brow + igrow, -jnp.inf)
    c_old = jnp.concatenate([c_scr[:, h] for h in range(nh)], axis=0)
    n_old = jnp.concatenate([n_scr[:, h] for h in range(nh)], axis=0)
    m_prev = jnp.concatenate([m_scr[:, h] for h in range(nh)], axis=0)[:, :, 0:1]
    inter = bc + m_prev
    mt = jnp.maximum(inter, jnp.max(logd, axis=-1, keepdims=True))
    w_inter = jnp.exp(inter - mt)
    a = jnp.exp(logd - mt) * bmm("bqd,bkd->bqk", qb, kb)
    num = bmm("bqk,bkd->bqd", a.astype(BF16), vb) + w_inter * bmm("bqd,bvd->bqv", qb, c_old.astype(BF16))
    den = jnp.sum(a, axis=-1, keepdims=True) + w_inter * jnp.sum(q * n_old, axis=-1, keepdims=True)
    hid = num / jnp.maximum(jnp.abs(den), jnp.exp(-mt))
    mu = jnp.mean(hid, axis=-1, keepdims=True)
    hc = hid - mu
    var = jnp.mean(hc * hc, axis=-1, keepdims=True)
    hn = hc * lax.rsqrt(var + LN_EPS)
    for h in range(nh):
        sl = slice(h * M_HEAD_DIM, (h + 1) * M_HEAD_DIM)
        out = hn[h * ns:(h + 1) * ns].reshape(ns * rows, M_HEAD_DIM) * nw_ref[:, sl]
        h_ref[:, sl] = out * jax.nn.sigmoid(o_ref[:, sl])
    m_new = mt[:, rows - 1:rows]
    w_c = jnp.exp(inter[:, rows - 1:rows] - m_new)
    w_s = jnp.exp(bc[:, rows - 1:rows] - bc + igc - m_new)
    vs_t = jnp.swapaxes(pad_rows(v * w_s), 1, 2).astype(BF16)
    c_new = w_c * c_old + bmm("bvs,bsd->bvd", vs_t, kb)
    n_new = w_c * n_old + jnp.sum(w_s * k, axis=1, keepdims=True)
    m_row = jnp.broadcast_to(m_new, (nh * ns, 1, LANES))
    for h in range(nh):
        c_scr[:, h] = c_new[h * ns:(h + 1) * ns]
        n_scr[:, h] = n_new[h * ns:(h + 1) * ns]
        m_scr[:, h] = m_row[h * ns:(h + 1) * ns]

    @pl.when(c_idx == pl.num_programs(1) - 1)
    def _():
        c_out_ref[...] = c_scr[...]
        n_out_ref[...] = n_scr[...]
        m_out_ref[...] = m_scr[...]


def _mlstm(q, k, v, o, gates, gate_bias_row, norm_w_row, c0, n0, m0, *, rows, valid, seqs):
    batch = c0.shape[0]
    total = q.shape[0]
    nc = total // (batch * rows)
    assert batch % seqs == 0 and (seqs == 1 or nc == 1)
    seq = lambda b, c: (b * nc + c, 0)
    fixed = lambda b, c: (0, 0)
    st4 = lambda b, c: (b, 0, 0, 0)
    wide = pl.BlockSpec((seqs * rows, MLSTM_WIDTH), seq)
    c_spec = pl.BlockSpec((seqs, N_HEADS_M, M_HEAD_DIM, M_HEAD_DIM), st4)
    v_spec = pl.BlockSpec((seqs, N_HEADS_M, 1, LANES), st4)
    return pl.pallas_call(
        functools.partial(_mlstm_kernel, rows=rows, valid=valid),
        grid=(batch // seqs, nc),
        in_specs=[wide, wide, wide, wide,
                  pl.BlockSpec((seqs * rows, LANES), seq),
                  pl.BlockSpec((1, LANES), fixed),
                  pl.BlockSpec((1, MLSTM_WIDTH), fixed),
                  c_spec, v_spec, v_spec],
        out_specs=[wide, c_spec, v_spec, v_spec],
        out_shape=[jax.ShapeDtypeStruct((total, MLSTM_WIDTH), F32),
                   jax.ShapeDtypeStruct(c0.shape, F32),
                   jax.ShapeDtypeStruct(n0.shape, F32),
                   jax.ShapeDtypeStruct(m0.shape, F32)],
        scratch_shapes=[pltpu.VMEM((seqs, N_HEADS_M, M_HEAD_DIM, M_HEAD_DIM), F32),
                        pltpu.VMEM((seqs, N_HEADS_M, 1, LANES), F32),
                        pltpu.VMEM((seqs, N_HEADS_M, 1, LANES), F32)],
        compiler_params=_cparams(("parallel", "arbitrary")),
        name="mlstm",
    )(q, k, v, o, gates, gate_bias_row, norm_w_row, c0, n0, m0)


def _sorting_network(n):
    pairs = []

    def merge(lo, m, r):
        step = r * 2
        if step < m:
            merge(lo, m, step)
            merge(lo + r, m, step)
            pairs.extend((i, i + r) for i in range(lo + r, lo + m - r, step))
        else:
            pairs.append((lo, lo + r))

    def sort(lo, m):
        if m > 1:
            sort(lo, m // 2)
            sort(lo + m // 2, m // 2)
            merge(lo, m, 1)

    sort(0, n)
    return pairs


_SORT_PAIRS = _sorting_network(N_KEYS // SUBLANES)
TOP_ROWS = 24


def _top_values(x, count):
    v = [x[k * SUBLANES:(k + 1) * SUBLANES] for k in range(x.shape[0] // SUBLANES)]
    for i, j in _SORT_PAIRS:
        v[i], v[j] = jnp.maximum(v[i], v[j]), jnp.minimum(v[i], v[j])
    tops = []
    for r in range(count):
        m = jnp.max(v[0], axis=0, keepdims=True)
        tops.append(m)
        left = count - 1 - r
        if left:
            took = v[0] == m
            nxt = v[1:] + [jnp.full_like(v[0], -jnp.inf)]
            v = [jnp.where(took, nxt[k], v[k]) for k in range(min(len(v), left))]
    return tops


def _rows(vals):
    return jnp.concatenate(vals, axis=0)


def _mid_kernel(att_ref, mls_ref, x_ref, woa_ref, wom_ref, g1_ref, b1_ref, wqt_ref, keys_ref,
                h_ref, rank_ref, b_ref, a_ref, l_ref, s_scr, top_scr, *, alpha):
    tm = x_ref.shape[0]
    nch = tm // LANES
    mix = _dot(att_ref[...].astype(BF16), woa_ref[...]) + _dot(mls_ref[...].astype(BF16), wom_ref[...])
    hid = _layer_norm(alpha * x_ref[...] + mix, g1_ref[...], b1_ref[...])
    h_ref[...] = hid
    q_t = _dot_nt(wqt_ref[...], hid.astype(BF16))
    for hc in range(2 * PEER_HEADS):
        s_t = _dot(keys_ref[hc], q_t[hc * PEER_HALF:(hc + 1) * PEER_HALF].astype(BF16))
        for tc in range(nch):
            s_scr[hc, tc] = s_t[:, tc * LANES:(tc + 1) * LANES]

    n_items = PEER_HEADS * nch
    pad = [jnp.full((TOP_ROWS - PEER_TOPK - 1, LANES), -jnp.inf, F32)]

    def find_tops(item, slot):
        h = item // nch
        tc = item % nch
        for c in range(2):
            top_scr[slot, c] = _rows(_top_values(s_scr[2 * h + c, tc], PEER_TOPK + 1) + pad)

    find_tops(0, 0)

    def route_body(it, carry):
        find_tops(jnp.minimum(it + 1, n_items - 1), (it + 1) % 2)
        h = it // nch
        tc = it % nch
        s1 = s_scr[2 * h, tc]
        s2 = s_scr[2 * h + 1, tc]
        tops1 = top_scr[it % 2, 0]
        tops2 = top_scr[it % 2, 1]
        t1 = [tops1[r:r + 1] for r in range(PEER_TOPK + 1)]
        t2 = [tops2[r:r + 1] for r in range(PEER_TOPK + 1)]
        wide = [tops1[:SUBLANES] + t2[r] for r in range(PEER_TOPK + 1)]
        tall = [t1[SUBLANES + k] + t2[0] for k in range(PEER_TOPK + 1 - SUBLANES)]
        cmax = t1[0] + t2[0]
        z = jnp.zeros_like(cmax)
        tau = below = cmax
        for r in range(PEER_TOPK + 1):
            head = jnp.maximum(jnp.max(wide[0], axis=0, keepdims=True), tall[0])
            if r < PEER_TOPK:
                tau = head
                z = z + jnp.exp(head - cmax)
            else:
                below = head
            left = PEER_TOPK - r
            if left:
                took_w = wide[0] == head
                took_t = tall[0] == head
                nxt_w = wide[1:] + [jnp.full_like(wide[0], -jnp.inf)]
                nxt_t = tall[1:] + [jnp.full_like(tall[0], -jnp.inf)]
                wide = [jnp.where(took_w, nxt_w[k], wide[k]) for k in range(min(len(wide), left))]
                tall = [jnp.where(took_t, nxt_t[k], tall[k]) for k in range(min(len(tall), left))]
        cut = 0.5 * (tau + below)
        rank = jnp.zeros_like(s2)
        count = jnp.zeros_like(s1)
        for r in range(PEER_TOPK):
            rank = jnp.where(t2[r] > s2, (r + 1) * RANK_SCALE, rank)
            count = jnp.where(s1 >= cut - t2[r], (r + 1) * RANK_SCALE, count)
        rank_ref[h, tc] = rank.astype(BF16)
        l_ref[h, tc] = count
        b_ref[h, tc] = jnp.exp(s2 - t2[0]).astype(BF16)
        a_ref[h, tc] = jnp.exp(s1 - (t1[0] + jnp.log(z)))
        return carry

    lax.fori_loop(0, n_items, route_body, 0)


def _mid(att, mls, x, wo_a, wo_m, g1, b1, wq_t, keys, alpha):
    t = x.shape[0]
    tm = TOKEN_TILE
    nch = tm // LANES
    row = lambda i: (i, 0)
    fixed2 = lambda i: (0, 0)
    fixed3 = lambda i: (0, 0, 0)
    tile4 = lambda i: (0, i, 0, 0)
    s_shape = lambda dt: jax.ShapeDtypeStruct((PEER_HEADS, t // LANES, N_KEYS, LANES), dt)
    s_spec = pl.BlockSpec((PEER_HEADS, nch, N_KEYS, LANES), tile4)
    return pl.pallas_call(
        functools.partial(_mid_kernel, alpha=alpha),
        grid=(t // tm,),
        in_specs=[pl.BlockSpec((tm, ATT_WIDTH), row), pl.BlockSpec((tm, MLSTM_WIDTH), row),
                  pl.BlockSpec((tm, D_MODEL), row),
                  pl.BlockSpec(wo_a.shape, fixed2), pl.BlockSpec(wo_m.shape, fixed2),
                  pl.BlockSpec(g1.shape, fixed2), pl.BlockSpec(b1.shape, fixed2),
                  pl.BlockSpec(wq_t.shape, fixed2), pl.BlockSpec(keys.shape, fixed3)],
        out_specs=[pl.BlockSpec((tm, D_MODEL), row), s_spec, s_spec, s_spec, s_spec],
        out_shape=[jax.ShapeDtypeStruct((t, D_MODEL), F32), s_shape(BF16), s_shape(BF16), s_shape(F32), s_shape(F32)],
        scratch_shapes=[pltpu.VMEM((2 * PEER_HEADS, nch, N_KEYS, LANES), F32),
                        pltpu.VMEM((2, 2, TOP_ROWS, LANES), F32)],
        compiler_params=_cparams(("parallel",)),
        name="out_proj_router",
    )(att, mls, x, wo_a, wo_m, g1, b1, wq_t, keys)


def _gelu(x):
    return 0.5 * x * (1.0 + lax.erf(x * (2.0 ** -0.5)))


ACT_TOKENS = 256
ACT_EXPERTS = (512, 512)
assert sum(ACT_EXPERTS) == EXPERT_BLOCK


RANK_SCALE = 256.0


def _row_tile(row):
    packed = jnp.broadcast_to(row, (2 * SUBLANES, row.shape[1])).astype(BF16)
    reps = N_KEYS // (2 * SUBLANES)
    return jnp.broadcast_to(packed[None], (reps,) + packed.shape).reshape(N_KEYS, row.shape[1])


def _zero_after(x):
    bits = lax.bitcast_convert_type(x, jnp.uint32)
    bits = lax.shift_right_logical(lax.shift_right_logical(bits, jnp.uint32(16)), jnp.uint32(16))
    return lax.bitcast_convert_type(bits, F32)


def _peer_block(u_ref, vt_ref, a_ref, l_ref, rank_ref, b_ref, ht_scr, w_new, w_old, acc_scr):
    tm = ht_scr.shape[1]
    n_tp = tm // ACT_TOKENS
    starts = [sum(ACT_EXPERTS[:g]) for g in range(len(ACT_EXPERTS))]
    tiles = [(r0, nr, tp) for r0, nr in zip(starts, ACT_EXPERTS) for tp in range(n_tp)]
    slice_at = {(sl + 1) * len(tiles) // (n_tp + 1): sl for sl in range(n_tp)}
    for k, (r0, nr, tp) in enumerate(tiles):
        zero = None
        if k in slice_at:
            cols = slice(slice_at[k] * ACT_TOKENS, (slice_at[k] + 1) * ACT_TOKENS)
            upd = acc_scr[:, cols] + _dot(vt_ref[...], w_old[:, cols])
            acc_scr[:, cols] = upd
            zero = _zero_after(upd[0:1, 0:LANES])
        act = _dot(u_ref[r0:r0 + nr, :], ht_scr[:, tp * ACT_TOKENS:(tp + 1) * ACT_TOKENS])
        for d in range(nr // N_KEYS):
            il = r0 // N_KEYS + d
            for half in range(ACT_TOKENS // LANES):
                tc = tp * (ACT_TOKENS // LANES) + half
                gate = jnp.zeros((N_KEYS, LANES), BF16)
                if zero is not None and d == 0 and half == 0:
                    gate = gate + _row_tile(zero)
                for h in range(PEER_HEADS):
                    room = jnp.maximum(_row_tile(l_ref[h, tc, il:il + 1, :]) - rank_ref[h, tc], 0)
                    gate = gate + jnp.minimum(_row_tile(a_ref[h, tc, il:il + 1, :]) * b_ref[h, tc], room)
                piece = act[d * N_KEYS:(d + 1) * N_KEYS, half * LANES:(half + 1) * LANES]
                w_new[il * N_KEYS:(il + 1) * N_KEYS, tc * LANES:(tc + 1) * LANES] = gate * _gelu(piece).astype(BF16)


def _peer_kernel(h_ref, rank_ref, b_ref, a0_ref, l0_ref, a1_ref, l1_ref, u0_ref, u1_ref,
                 vtp_ref, vt0_ref, vtl_ref, g2_ref, b2_ref, y_ref, ht_scr, w0, w1, acc_scr, *, alpha):
    s = pl.program_id(1)

    @pl.when(s == 0)
    def _():
        ht_scr[...] = h_ref[...].T.astype(BF16)
        acc_scr[...] = jnp.zeros_like(acc_scr)
        w1[...] = jnp.zeros_like(w1)

    _peer_block(u0_ref, vtp_ref, a0_ref, l0_ref, rank_ref, b_ref, ht_scr, w0, w1, acc_scr)
    _peer_block(u1_ref, vt0_ref, a1_ref, l1_ref, rank_ref, b_ref, ht_scr, w1, w0, acc_scr)

    @pl.when(s == pl.num_programs(1) - 1)
    def _():
        f = (acc_scr[...] + _dot(vtl_ref[...], w1[...])).T
        y_ref[...] = _layer_norm(alpha * h_ref[...] + f, g2_ref[...], b2_ref[...])


def _peer(hid, rank, b, a, cnt, u_b, vt_b, g2, b2, alpha):
    t = hid.shape[0]
    tm = TOKEN_TILE
    nch = tm // LANES
    nblk = u_b.shape[0] // EXPERT_BLOCK
    rows = EXPERT_BLOCK // N_KEYS
    row = lambda i, s: (i, 0)
    fixed2 = lambda i, s: (0, 0)
    full_spec = pl.BlockSpec((PEER_HEADS, nch, N_KEYS, LANES), lambda i, s: (0, i, 0, 0))
    rows_spec = lambda f: pl.BlockSpec((PEER_HEADS, nch, rows, LANES), lambda i, s: (0, i, f(s), 0))
    u_spec = lambda f: pl.BlockSpec((EXPERT_BLOCK, D_MODEL), lambda i, s: (f(s), 0))
    vt_spec = lambda f: pl.BlockSpec((D_MODEL, EXPERT_BLOCK), lambda i, s: (0, f(s)))
    even = lambda s: 2 * s
    odd = lambda s: 2 * s + 1
    return pl.pallas_call(
        functools.partial(_peer_kernel, alpha=alpha),
        grid=(t // tm, nblk // 2),
        in_specs=[pl.BlockSpec((tm, D_MODEL), row), full_spec, full_spec,
                  rows_spec(even), rows_spec(even), rows_spec(odd), rows_spec(odd),
                  u_spec(even), u_spec(odd),
                  vt_spec(lambda s: jnp.maximum(2 * s - 1, 0)), vt_spec(even), vt_spec(lambda s: nblk - 1),
                  pl.BlockSpec(g2.shape, fixed2), pl.BlockSpec(b2.shape, fixed2)],
        out_specs=pl.BlockSpec((tm, D_MODEL), row),
        out_shape=jax.ShapeDtypeStruct((t, D_MODEL), F32),
        scratch_shapes=[pltpu.VMEM((D_MODEL, tm), BF16),
                        pltpu.VMEM((EXPERT_BLOCK, tm), BF16),
                        pltpu.VMEM((EXPERT_BLOCK, tm), BF16),
                        pltpu.VMEM((D_MODEL, tm), F32)],
        compiler_params=_cparams(("parallel", "arbitrary")),
        name="peer_experts",
    )(hid, rank, b, a, cnt, a, cnt, u_b, u_b, vt_b, vt_b, vt_b, g2, b2)


def _finish(x, att, mls, lw, alpha):
    hid, rank, b, a, cnt = _mid(att, mls, x, lw["wo_a"], lw["wo_m"], lw["g1"], lw["b1"], lw["wq_t"], lw["keys"], alpha)
    return _peer(hid, rank, b, a, cnt, lw["u"], lw["v_t"], lw["g2"], lw["b2"], alpha)


def _prompt_bias_index():
    q = np.arange(WINDOW)[:, None]
    kband = np.arange(2 * WINDOW)[None, :]
    dist = q + WINDOW - kband
    return _bias_index_table(dist, (dist >= 0) & (dist < WINDOW))


def _sample_bias_index(steps):
    q = np.arange(SAMPLE_PAD)[:, None]
    c = np.arange(WINDOW)[None, :]
    dist_c = WINDOW - c + q
    idx_c = _bias_index_table(dist_c, (dist_c >= 0) & (dist_c < WINDOW))
    jn = np.arange(LANES)[None, :]
    dist_n = q - jn
    idx_n = _bias_index_table(dist_n, (dist_n >= 0) & (dist_n < WINDOW) & (jn < steps))
    return np.concatenate([idx_c, idx_n], axis=1)


def _per_row_head_layout(tab):
    _, steps, nk = tab.shape
    t = tab.reshape(N_KV_HEADS, 2, 2, steps, nk)
    return t.transpose(0, 2, 1, 3, 4).reshape(N_KV_HEADS, 2, 2 * steps, nk)


def kernel(x_prompt, x_sample, cache_k_win, cache_v_win, state_C, state_n, state_m, w_in, gate_bias,
           attn_sinks, rel_bias, mlstm_norm_w, w_out, ln1_g, ln1_b, peer_wq, peer_keys, peer_u, peer_v,
           ln2_g, ln2_b):
    depth = w_in.shape[0]
    alpha = (2 * depth) ** 0.25
    bsz, seq, _ = x_prompt.shape
    db, ds, _ = x_sample.shape
    assert bsz == 1 and seq % TOKEN_TILE == 0 and ds <= SAMPLE_PAD and (db * ds) % TOKEN_TILE == 0
    assert PAST_LEN >= WINDOW

    bias_p = _bias_table(_prompt_bias_index(), rel_bias)
    bias_s = _bias_table(_sample_bias_index(ds), rel_bias)
    bias_sc = _per_row_head_layout(bias_s[:, :, :WINDOW])
    bias_sn = _per_row_head_layout(bias_s[:, :, WINDOW:])

    xp = x_prompt[0]
    xs = x_sample
    pk, pv, pc, pn, pm = [], [], [], [], []
    sk, sv, sc, sn, sm = [], [], [], [], []
    for l in range(depth):
        w_main, w_gate = _projection_weights(w_in[l])
        gb_row = jnp.pad(gate_bias[l], (0, LANES - 2 * N_HEADS_M))[None, :].astype(F32)
        nw_row = mlstm_norm_w[l].reshape(1, MLSTM_WIDTH).astype(F32)
        sinks = attn_sinks[l].astype(F32)
        sinks_p = jnp.broadcast_to(sinks[:, None, None], (N_HEADS_ATT, 1, LANES))
        sinks_s = _per_row_head_layout(
            jnp.broadcast_to(sinks[:, None, None], (N_HEADS_ATT, SAMPLE_PAD, LANES)))
        lw = dict(
            wo_a=w_out[l, :ATT_WIDTH].astype(BF16), wo_m=w_out[l, ATT_WIDTH:].astype(BF16),
            g1=ln1_g[l][None, :], b1=ln1_b[l][None, :],
            wq_t=peer_wq[l].T.astype(BF16),
            keys=peer_keys[l].reshape(2 * PEER_HEADS, N_KEYS, PEER_HALF).astype(BF16),
            u=peer_u[l].astype(BF16), v_t=peer_v[l].T.astype(BF16),
            g2=ln2_g[l][None, :], b2=ln2_b[l][None, :])

        q_a, k_x, v_x, q_m, k_m, v_m, o_m, gates = _project(xp, w_main, w_gate)
        att = _attn_prompt(q_a, k_x, v_x, bias_p, sinks_p)
        zeros_c = jnp.zeros((1, N_HEADS_M, M_HEAD_DIM, M_HEAD_DIM), F32)
        zeros_v = jnp.zeros((1, N_HEADS_M, 1, LANES), F32)
        mls, c_p, n_p, m_p = _mlstm(q_m, k_m, v_m, o_m, gates, gb_row, nw_row, zeros_c, zeros_v, zeros_v,
                                    rows=M_HEAD_DIM, valid=M_HEAD_DIM, seqs=1)
        tail_k = k_x[seq - WINDOW:].reshape(WINDOW, N_KV_HEADS, 2, LANES)[:, :, 0, :HEAD_DIM]
        tail_v = v_x[seq - WINDOW:].reshape(WINDOW, N_KV_HEADS, 2, LANES)[:, :, 0, :HEAD_DIM]
        pk.append(tail_k[None])
        pv.append(tail_v[None])
        pc.append(c_p)
        pn.append(n_p[:, :, 0, :])
        pm.append(m_p[:, :, 0, 0])
        xp = _finish(xp, att, mls, lw, alpha)

        xs_pad = jnp.pad(xs, ((0, 0), (0, SAMPLE_PAD - ds), (0, 0))).reshape(db * SAMPLE_PAD, D_MODEL)
        q_a, k_x, v_x, q_m, k_m, v_m, o_m, gates = _project(xs_pad, w_main, w_gate)
        q_rows = q_a.reshape(db, SAMPLE_PAD, 4, LANES).transpose(0, 2, 1, 3).reshape(db, 4 * SAMPLE_PAD, LANES)
        att_rows = _attn_sample(q_rows, k_x.reshape(db, SAMPLE_PAD, 512), v_x.reshape(db, SAMPLE_PAD, 512),
                                cache_k_win[l].reshape(db, WINDOW, LANES).astype(F32),
                                cache_v_win[l].reshape(db, WINDOW, LANES).astype(F32),
                                bias_sc, bias_sn, sinks_s)
        att = att_rows.reshape(db, 4, SAMPLE_PAD, LANES).transpose(0, 2, 1, 3)[:, :ds].reshape(db * ds, ATT_WIDTH)
        c0 = state_C[l].astype(F32)
        n0 = state_n[l].astype(F32)[:, :, None, :]
        m0 = jnp.broadcast_to(state_m[l].astype(F32)[:, :, None, None], (db, N_HEADS_M, 1, LANES))
        mls, c_s, n_s, m_s = _mlstm(q_m, k_m, v_m, o_m, gates, gb_row, nw_row, c0, n0, m0,
                                    rows=SAMPLE_PAD, valid=ds, seqs=SAMPLE_SEQS)
        mls = mls.reshape(db, SAMPLE_PAD, MLSTM_WIDTH)[:, :ds].reshape(db * ds, MLSTM_WIDTH)
        k_new = k_x.reshape(db, SAMPLE_PAD, N_KV_HEADS, 2, LANES)[:, :ds, :, 0, :HEAD_DIM]
        v_new = v_x.reshape(db, SAMPLE_PAD, N_KV_HEADS, 2, LANES)[:, :ds, :, 0, :HEAD_DIM]
        sk.append(jnp.concatenate([cache_k_win[l].astype(F32), k_new], axis=1)[:, -WINDOW:])
        sv.append(jnp.concatenate([cache_v_win[l].astype(F32), v_new], axis=1)[:, -WINDOW:])
        sc.append(c_s)
        sn.append(n_s[:, :, 0, :])
        sm.append(m_s[:, :, 0, 0])
        xs = _finish(xs.reshape(db * ds, D_MODEL), att, mls, lw, alpha).reshape(db, ds, D_MODEL)

    return (xp[None], xs, jnp.stack(pk), jnp.stack(pv), jnp.stack(pc), jnp.stack(pn), jnp.stack(pm),
            jnp.stack(sk), jnp.stack(sv), jnp.stack(sc), jnp.stack(sn), jnp.stack(sm))
```
